```python
import math
import jax
import jax.numpy as jnp
from jax import lax
import numpy as np

D_MODEL = 4096
BATCH = 2
SEQ = 4096
DEPTH = 2

HEAD_DIM = 128
ROPE_THETA = 10000.0
NORM_EPS = 1e-6
QUERY_BLOCK = 128

MLA_HEADS = 8
MLA_Q_RANK = 768
MLA_KV_RANK = 256
MLA_NOPE_DIM = 128
MLA_ROPE_DIM = 64
MLA_V_DIM = 128

DIFF_HEADS = 8
DIFF_QK_DIM = 64
DIFF_V_DIM = 2 * DIFF_QK_DIM
DIFF_NORM_EPS = 1e-5

SWA_HEADS = 8
SWA_KV_HEADS = 2
SWA_HALF_WINDOW = 128

DIL_PAIRS = ((128, 1), (512, 4), (2048, 16))
DIL_HEADS = 4
DIL_BLOCK = 64

N_EXPERTS = 64
N_EXPERT_GROUPS = 8
TOP_GROUPS = 4
TOP_K = 8
EXPERT_FF = 256
SHARED_FF = 1024
ROUTED_SCALE = 2.5
EXPERT_BLOCK = 128

N_BRANCHES = 4
A_COLS = MLA_Q_RANK + MLA_KV_RANK + MLA_ROPE_DIM
B_COLS = 3 * DIFF_HEADS * DIFF_V_DIM
C_COLS = (SWA_HEADS + 2 * SWA_KV_HEADS) * HEAD_DIM
D_COLS = len(DIL_PAIRS) * 3 * DIL_HEADS * HEAD_DIM
GATE_COLS = N_BRANCHES * D_MODEL
IN_COLS = A_COLS + B_COLS + C_COLS + D_COLS + GATE_COLS
A_OUT = MLA_HEADS * MLA_V_DIM
B_OUT = DIFF_HEADS * DIFF_V_DIM
C_OUT = SWA_HEADS * HEAD_DIM
D_OUT = DIL_HEADS * HEAD_DIM
BRANCH_ROWS = A_OUT + B_OUT + C_OUT + D_OUT

kernel_name = 'gated_hybrid_encoder_moe'


def _split(t, sizes, axis=-1):
    return jnp.split(t, [int(v) for v in np.cumsum(sizes)[:-1]], axis=axis)


def rmsnorm(x, g, eps=NORM_EPS):
    xf = x.astype(jnp.float32)
    y = xf * lax.rsqrt(jnp.mean(xf * xf, axis=-1, keepdims=True) + eps)
    return (y * g.astype(jnp.float32)).astype(x.dtype)


def rope(x, pos):
    half = x.shape[-1] // 2
    inv_freq = ROPE_THETA ** (-jnp.arange(half, dtype=jnp.float32) / half)
    ang = pos.astype(jnp.float32)[:, :, None, None] * inv_freq
    cos, sin = jnp.cos(ang), jnp.sin(ang)
    xf = x.astype(jnp.float32)
    x1, x2 = xf[..., :half], xf[..., half:]
    return jnp.concatenate([x1 * cos - x2 * sin, x2 * cos + x1 * sin], axis=-1).astype(x.dtype)


def blocked_dense_attention(q, k, v, scale, coeff):
    M, B, S, H, dk = q.shape
    nb = S // QUERY_BLOCK
    qb = q.reshape(M, B, nb, QUERY_BLOCK, H, dk).transpose(2, 0, 1, 3, 4, 5)
    cf = coeff.astype(jnp.float32)

    def one_block(q_blk):
        s = jnp.einsum('mbqhd,mbkhd->mbhqk', q_blk, k, preferred_element_type=jnp.float32) * scale
        p = jnp.einsum('m,mbhqk->bhqk', cf, jax.nn.softmax(s, axis=-1))
        return jnp.einsum('bhqk,bkhd->bqhd', p.astype(v.dtype), v)

    out = lax.map(one_block, qb)
    return out.transpose(1, 0, 2, 3, 4).reshape(B, S, H, v.shape[-1])


def banded_attention(q, k, v, half_window, block, sink=None):
    B, L, Hq, d = q.shape
    Hkv = k.shape[2]
    grp = Hq // Hkv
    nb = -(-L // block)
    Lp = nb * block
    W = block + 2 * half_window
    qp = jnp.pad(q, ((0, 0), (0, Lp - L), (0, 0), (0, 0))).reshape(B, nb, block, Hkv, grp, d)
    kv_pad = ((0, 0), (half_window, Lp - L + half_window), (0, 0), (0, 0))
    idx = np.arange(nb)[:, None] * block + np.arange(W)[None, :]
    kb = jnp.pad(k, kv_pad)[:, idx]
    vb = jnp.pad(v, kv_pad)[:, idx]
    qpos = np.arange(Lp).reshape(nb, block)
    kpos = idx - half_window
    valid = ((np.abs(qpos[:, :, None] - kpos[:, None, :]) <= half_window)
             & (kpos[:, None, :] >= 0) & (kpos[:, None, :] < L))
    s = jnp.einsum('bnqhgd,bnkhd->bnhgqk', qp, kb, preferred_element_type=jnp.float32) * (d ** -0.5)
    s = jnp.where(valid[None, :, None, None], s, -jnp.inf)
    m = jnp.max(s, axis=-1)
    if sink is not None:
        sk = sink.astype(jnp.float32).reshape(Hkv, grp)[None, None, :, :, None]
        m = jnp.maximum(m, sk)
    p = jnp.exp(s - m[..., None])
    den = jnp.sum(p, axis=-1)
    if sink is not None:
        den = den + jnp.exp(sk - m)
    p = p / den[..., None]
    lse = m + jnp.log(den)
    out = jnp.einsum('bnhgqk,bnkhd->bnqhgd', p.astype(v.dtype), vb).reshape(B, Lp, Hq, d)[:, :L]
    lse = lse.transpose(0, 1, 4, 2, 3).reshape(B, Lp, Hq)[:, :L]
    return out, lse


def mla_mixer(a_in, pos, q_norm_g, w_uq, kv_norm_g, w_ukv):
    B, S, _ = a_in.shape
    c_q, c_kv, k_r = _split(a_in, (MLA_Q_RANK, MLA_KV_RANK, MLA_ROPE_DIM))
    q = (rmsnorm(c_q, q_norm_g) @ w_uq).reshape(B, S, MLA_HEADS, MLA_NOPE_DIM + MLA_ROPE_DIM)
    q = jnp.concatenate([q[..., :MLA_NOPE_DIM], rope(q[..., MLA_NOPE_DIM:], pos)], axis=-1)
    kv = (rmsnorm(c_kv, kv_norm_g) @ w_ukv).reshape(B, S, MLA_HEADS, MLA_NOPE_DIM + MLA_V_DIM)
    k_nope, v = kv[..., :MLA_NOPE_DIM], kv[..., MLA_NOPE_DIM:]
    k_rope = jnp.broadcast_to(rope(k_r[:, :, None, :], pos), (B, S, MLA_HEADS, MLA_ROPE_DIM))
    k = jnp.concatenate([k_nope, k_rope], axis=-1)
    o = blocked_dense_attention(q[None], k[None], v, (MLA_NOPE_DIM + MLA_ROPE_DIM) ** -0.5,
                                jnp.ones((1,), jnp.float32))
    return o.reshape(B, S, A_OUT)


def diff_mixer(b_in, pos, lam_params, subln_g, lam_init):
    B, S, _ = b_in.shape
    qd, kd, vd = _split(b_in, (B_OUT, B_OUT, B_OUT))

    def two_maps(t):
        t = rope(t.reshape(B, S, 2 * DIFF_HEADS, DIFF_QK_DIM), pos)
        return t.reshape(B, S, DIFF_HEADS, 2, DIFF_QK_DIM).transpose(3, 0, 1, 2, 4)

    v = vd.reshape(B, S, DIFF_HEADS, DIFF_V_DIM)
    lp = lam_params.astype(jnp.float32)
    lam = jnp.exp(jnp.sum(lp[0] * lp[1])) - jnp.exp(jnp.sum(lp[2] * lp[3])) + lam_init
    o = blocked_dense_attention(two_maps(qd), two_maps(kd), v, DIFF_QK_DIM ** -0.5,
                                jnp.stack([jnp.ones_like(lam), -lam]))
    o = rmsnorm(o, subln_g, DIFF_NORM_EPS) * (1.0 - lam_init)
    return o.reshape(B, S, B_OUT)


def swa_mixer(c_in, pos, sink):
    B, S, _ = c_in.shape
    qc, kc, vc = _split(c_in, (SWA_HEADS * HEAD_DIM, SWA_KV_HEADS * HEAD_DIM, SWA_KV_HEADS * HEAD_DIM))
    q = rope(qc.reshape(B, S, SWA_HEADS, HEAD_DIM), pos)
    k = rope(kc.reshape(B, S, SWA_KV_HEADS, HEAD_DIM), pos)
    v = vc.reshape(B, S, SWA_KV_HEADS, HEAD_DIM)
    o, _ = banded_attention(q, k, v, SWA_HALF_WINDOW, QUERY_BLOCK, sink)
    return o.reshape(B, S, C_OUT)


def dilated_group(q, k, v, window, dilation):
    B, S, H, d = q.shape
    L = S // dilation

    def to_sub(t):
        return t.reshape(B, L, dilation, H, d).transpose(0, 2, 1, 3, 4).reshape(B * dilation, L, H, d)

    o, lse = banded_attention(to_sub(q), to_sub(k), to_sub(v), window // (2 * dilation), DIL_BLOCK)
    o = o.reshape(B, dilation, L, H, d).transpose(0, 2, 1, 3, 4).reshape(B, S, H, d)
    lse = lse.reshape(B, dilation, L, H).transpose(0, 2, 1, 3).reshape(B, S, H)
    return o, lse


def dilated_mixer(d_in, pos):
    B, S, _ = d_in.shape
    w = DIL_HEADS * HEAD_DIM
    groups = _split(d_in, (3 * w,) * len(DIL_PAIRS))
    outs, lses = [], []
    for g_in, (window, dilation) in zip(groups, DIL_PAIRS):
        q, k, v = [t.reshape(B, S, DIL_HEADS, HEAD_DIM) for t in _split(g_in, (w, w, w))]
        o, lse = dilated_group(rope(q, pos), rope(k, pos), v, window, dilation)
        outs.append(o)
        lses.append(lse)
    wts = jax.nn.softmax(jnp.stack(lses), axis=0)
    o = jnp.einsum('gbsh,gbshd->bshd', wts.astype(outs[0].dtype), jnp.stack(outs))
    return o.reshape(B, S, D_OUT)


def moe_ffn(h, router_w, router_bias, w_gate, w_up, w_down, s_gate, s_up, s_down):
    B, S, D = h.shape
    t = h.reshape(B * S, D)
    N = t.shape[0]
    per_group = N_EXPERTS // N_EXPERT_GROUPS
    scores = jax.nn.sigmoid(jnp.matmul(t, router_w, preferred_element_type=jnp.float32))
    choice = scores + router_bias.astype(jnp.float32)
    group_score = lax.top_k(choice.reshape(N, N_EXPERT_GROUPS, per_group), 2)[0].sum(-1)
    _, top_groups = lax.top_k(group_score, TOP_GROUPS)
    group_ok = jnp.any(top_groups[:, :, None] == jnp.arange(N_EXPERT_GROUPS)[None, None, :], axis=1)
    expert_ok = jnp.repeat(group_ok, per_group, axis=1)
    _, top_idx = lax.top_k(jnp.where(expert_ok, choice, -jnp.inf), TOP_K)
    wts = jnp.take_along_axis(scores, top_idx, axis=1)
    wts = wts / jnp.sum(wts, axis=-1, keepdims=True) * ROUTED_SCALE

    A = N * TOP_K
    flat_e = top_idx.reshape(A)
    order = jnp.argsort(flat_e)
    e_sorted = flat_e[order]
    tok_sorted = (order // TOP_K).astype(jnp.int32)
    w_sorted = wts.reshape(A)[order].astype(h.dtype)
    counts = jnp.bincount(flat_e, length=N_EXPERTS)
    padded = (counts + EXPERT_BLOCK - 1) // EXPERT_BLOCK * EXPERT_BLOCK
    pad_end = jnp.cumsum(padded)
    pad_start = pad_end - padded
    start = jnp.cumsum(counts) - counts
    dest = pad_start[e_sorted] + jnp.arange(A) - start[e_sorted]
    n_slots = -(-(A + N_EXPERTS * (EXPERT_BLOCK - 1)) // EXPERT_BLOCK) * EXPERT_BLOCK
    n_blocks = n_slots // EXPERT_BLOCK
    slot_tok = jnp.full((n_slots,), N, jnp.int32).at[dest].set(tok_sorted)
    slot_w = jnp.zeros((n_slots,), h.dtype).at[dest].set(w_sorted)
    block_expert = jnp.minimum(
        jnp.searchsorted(pad_end, jnp.arange(n_blocks) * EXPERT_BLOCK, side='right'), N_EXPERTS - 1)
    t_pad = jnp.concatenate([t, jnp.zeros((1, D), t.dtype)], axis=0)

    def expert_block(acc, blk):
        tok, wt, e = blk
        xb = t_pad[tok]
        y = (jax.nn.silu(xb @ w_gate[e]) * (xb @ w_up[e])) @ w_down[e]
        return acc.at[tok].add(y * wt[:, None]), None

    routed, _ = lax.scan(expert_block, jnp.zeros((N + 1, D), h.dtype),
                         (slot_tok.reshape(n_blocks, EXPERT_BLOCK),
                          slot_w.reshape(n_blocks, EXPERT_BLOCK), block_expert))
    shared = (jax.nn.silu(t @ s_gate) * (t @ s_up)) @ s_down
    return (routed[:N] + shared).reshape(B, S, D)


def setup_inputs(seed: int = 0) -> dict:
    key = jax.random.key(seed)
    ks = jax.random.split(key, 26)
    D = D_MODEL

    def nrm(k, shape, scale):
        return jax.random.normal(k, shape, jnp.float32) * scale

    def gain(k, shape):
        return 1.0 + 0.05 * jax.random.normal(k, shape, jnp.float32)

    positions = (jax.random.randint(ks[2], (BATCH, 1), 0, 4096, dtype=jnp.int32)
                 + jnp.arange(SEQ, dtype=jnp.int32)[None, :])
    return {
        'x': nrm(ks[0], (BATCH, SEQ, D), 1.0),
        'c': nrm(ks[1], (BATCH, D), 1.0),
        'positions': positions,
        'w_ada': nrm(ks[3], (DEPTH, D, 6 * D), 0.5 * D ** -0.5),
        'b_ada': nrm(ks[4], (DEPTH, 6 * D), 0.01),
        'mix_norm_g': gain(ks[5], (DEPTH, D)),
        'ffn_norm_g': gain(ks[6], (DEPTH, D)),
        'w_in': nrm(ks[7], (DEPTH, D, IN_COLS), D ** -0.5),
        'mla_q_norm_g': gain(ks[8], (DEPTH, MLA_Q_RANK)),
        'mla_w_uq': nrm(ks[9], (DEPTH, MLA_Q_RANK, MLA_HEADS * (MLA_NOPE_DIM + MLA_ROPE_DIM)), MLA_Q_RANK ** -0.5),
        'mla_kv_norm_g': gain(ks[10], (DEPTH, MLA_KV_RANK)),
        'mla_w_ukv': nrm(ks[11], (DEPTH, MLA_KV_RANK, MLA_HEADS * (MLA_NOPE_DIM + MLA_V_DIM)), MLA_KV_RANK ** -0.5),
        'diff_lambda': nrm(ks[12], (DEPTH, 4, DIFF_QK_DIM), 0.1),
        'diff_subln_g': gain(ks[13], (DEPTH, DIFF_V_DIM)),
        'swa_sink': nrm(ks[14], (DEPTH, SWA_HEADS), 1.0),
        'w_branch': nrm(ks[15], (DEPTH, BRANCH_ROWS, D), A_OUT ** -0.5),
        'w_out': nrm(ks[16], (DEPTH, D, D), D ** -0.5),
        'router_w': nrm(ks[17], (DEPTH, D, N_EXPERTS), D ** -0.5),
        'router_bias': nrm(ks[18], (DEPTH, N_EXPERTS), 0.01),
        'expert_w_gate': nrm(ks[19], (DEPTH, N_EXPERTS, D, EXPERT_FF), D ** -0.5),
        'expert_w_up': nrm(ks[20], (DEPTH, N_EXPERTS, D, EXPERT_FF), D ** -0.5),
        'expert_w_down': nrm(ks[21], (DEPTH, N_EXPERTS, EXPERT_FF, D), EXPERT_FF ** -0.5),
        'shared_w_gate': nrm(ks[22], (DEPTH, D, SHARED_FF), D ** -0.5),
        'shared_w_up': nrm(ks[23], (DEPTH, D, SHARED_FF), D ** -0.5),
        'shared_w_down': nrm(ks[24], (DEPTH, SHARED_FF, D), SHARED_FF ** -0.5),
        'final_norm_g': gain(ks[25], (D,)),
    }


def reference(x, c, positions, w_ada, b_ada, mix_norm_g, ffn_norm_g, w_in, mla_q_norm_g, mla_w_uq,
              mla_kv_norm_g, mla_w_ukv, diff_lambda, diff_subln_g, swa_sink, w_branch, w_out,
              router_w, router_bias, expert_w_gate, expert_w_up, expert_w_down,
              shared_w_gate, shared_w_up, shared_w_down, final_norm_g):
    B, S, D = x.shape
    for l in range(DEPTH):
        mod = jax.nn.silu(c) @ w_ada[l] + b_ada[l]
        shift_m, scale_m, gate_m, shift_f, scale_f, gate_f = [m[:, None, :] for m in jnp.split(mod, 6, axis=-1)]

        h = rmsnorm(x, mix_norm_g[l]) * (1.0 + scale_m) + shift_m
        proj = h @ w_in[l]
        a_in, b_in, c_in, d_in, gate_in = _split(proj, (A_COLS, B_COLS, C_COLS, D_COLS, GATE_COLS))
        lam_init = 0.8 - 0.6 * math.exp(-0.3 * l)
        outs = (
            mla_mixer(a_in, positions, mla_q_norm_g[l], mla_w_uq[l], mla_kv_norm_g[l], mla_w_ukv[l]),
            diff_mixer(b_in, positions, diff_lambda[l], diff_subln_g[l], lam_init),
            swa_mixer(c_in, positions, swa_sink[l]),
            dilated_mixer(d_in, positions),
        )
        gates = jax.nn.sigmoid(gate_in.reshape(B, S, N_BRANCHES, D))
        branch_w = _split(w_branch[l], (A_OUT, B_OUT, C_OUT, D_OUT), axis=0)
        merged = jnp.zeros_like(x)
        for i in range(N_BRANCHES):
            merged = merged + gates[:, :, i] * (outs[i] @ branch_w[i])
        x = x + gate_m * (merged @ w_out[l])

        h = rmsnorm(x, ffn_norm_g[l]) * (1.0 + scale_f) + shift_f
        x = x + gate_f * moe_ffn(h, router_w[l], router_bias[l], expert_w_gate[l], expert_w_up[l],
                                 expert_w_down[l], shared_w_gate[l], shared_w_up[l], shared_w_down[l])
    return rmsnorm(x, final_norm_g)
```

```python
import functools
import math

import numpy as np
import jax
import jax.numpy as jnp
from jax import lax
from jax.experimental import pallas as pl
from jax.experimental.pallas import tpu as pltpu

F32 = jnp.float32
BF16 = jnp.bfloat16

D_MODEL = 4096
DEPTH = 2
HEAD_DIM = 128
ROPE_THETA = 10000.0
NORM_EPS = 1e-6

MLA_HEADS = 8
MLA_Q_RANK = 768
MLA_KV_RANK = 256
MLA_NOPE_DIM = 128
MLA_ROPE_DIM = 64
MLA_V_DIM = 128

DIFF_HEADS = 8
DIFF_QK_DIM = 64
DIFF_V_DIM = 128
DIFF_NORM_EPS = 1e-5

SWA_HEADS = 8
SWA_KV_HEADS = 2
SWA_HALF_WINDOW = 128

DIL_PAIRS = ((128, 1), (512, 4), (2048, 16))
DIL_HEADS = 4

N_EXPERTS = 64
N_EXPERT_GROUPS = 8
TOP_GROUPS = 4
TOP_K = 8
EXPERT_FF = 256
SHARED_FF = 1024
ROUTED_SCALE = 2.5

A_COLS = MLA_Q_RANK + MLA_KV_RANK + MLA_ROPE_DIM
B_COLS = 3 * DIFF_HEADS * DIFF_V_DIM
C_COLS = (SWA_HEADS + 2 * SWA_KV_HEADS) * HEAD_DIM
D_GROUP_COLS = 3 * DIL_HEADS * HEAD_DIM
D_COLS = len(DIL_PAIRS) * D_GROUP_COLS
OFF_B = A_COLS
OFF_C = OFF_B + B_COLS
OFF_D = OFF_C + C_COLS
OFF_G = OFF_D + D_COLS

LANES = 128
HW4 = DIL_HEADS * HEAD_DIM

R_DIL = 0
R_DIFF_Q = 3 * 2 * HW4
R_DIFF_K = R_DIFF_Q + 1024
R_SWA_Q = R_DIFF_K + 1024
R_SWA_K = R_SWA_Q + 1024
R_COLS = 6656
V_DIL = 0
V_DIFF = 3 * HW4
V_SWA = V_DIFF + 1024
V_COLS = 3072
A_PAD = 1152

EXPERT_BLOCK_ROWS = 256
COMBINE_TOKENS = 32
VMEM_LIMIT = 56 * 1024 * 1024


def _cparams(sem, vmem=VMEM_LIMIT):
    return pltpu.CompilerParams(dimension_semantics=sem, vmem_limit_bytes=vmem)


def _nt_dot(a, b):
    return lax.dot_general(a, b, (((1,), (1,)), ((), ())), preferred_element_type=F32)


def _ada_kernel(c_ref, w_ref, b_ref, o_ref):
    c = c_ref[...]
    s = (c * jax.nn.sigmoid(c)).astype(BF16)
    w = w_ref[0].astype(BF16)
    o_ref[0] = jnp.dot(s, w, preferred_element_type=F32) + b_ref[0]


def _ada_mod(c, w_ada, b_ada):
    B, D = c.shape
    cp = jnp.zeros((8, D), F32).at[:B].set(c)
    n6 = w_ada.shape[-1]
    tn = 512
    return pl.pallas_call(
        _ada_kernel,
        grid=(DEPTH, n6 // tn),
        in_specs=[
            pl.BlockSpec((8, D), lambda l, j: (0, 0)),
            pl.BlockSpec((1, D, tn), lambda l, j: (l, 0, j)),
            pl.BlockSpec((1, 1, tn), lambda l, j: (l, 0, j)),
        ],
        out_specs=pl.BlockSpec((1, 8, tn), lambda l, j: (l, 0, j)),
        out_shape=jax.ShapeDtypeStruct((DEPTH, 8, n6), F32),
        compiler_params=_cparams(("arbitrary", "arbitrary")),
        name="ada_mod",
    )(cp, w_ada, b_ada.reshape(DEPTH, 1, n6))


def _norm_mod_kernel(x_ref, g_ref, mod_ref, o_ref, *, shift_row, scale_row):
    x = x_ref[...]
    ms = jnp.mean(x * x, axis=-1, keepdims=True)
    y = x * lax.rsqrt(ms + NORM_EPS) * g_ref[...]
    h = y * (1.0 + mod_ref[0, scale_row:scale_row + 1, :]) + mod_ref[0, shift_row:shift_row + 1, :]
    o_ref[...] = h.astype(o_ref.dtype)


def _norm_mod(x, g, mod, shift_row, scale_row, seq, tm=256):
    N, D = x.shape
    tm = min(tm, seq)
    per_b = seq // tm
    return pl.pallas_call(
        functools.partial(_norm_mod_kernel, shift_row=shift_row, scale_row=scale_row),
        grid=(N // tm,),
        in_specs=[
            pl.BlockSpec((tm, D), lambda i: (i, 0)),
            pl.BlockSpec((1, D), lambda i: (0, 0)),
            pl.BlockSpec((1, 6, D), lambda i: (i // per_b, 0, 0)),
        ],
        out_specs=pl.BlockSpec((tm, D), lambda i: (i, 0)),
        out_shape=jax.ShapeDtypeStruct((N, D), BF16),
        compiler_params=_cparams(("arbitrary",)),
        name="norm_mod",
    )(x, g.reshape(1, D), mod)


def _final_norm_kernel(x_ref, g_ref, o_ref):
    x = x_ref[...]
    ms = jnp.mean(x * x, axis=-1, keepdims=True)
    o_ref[...] = x * lax.rsqrt(ms + NORM_EPS) * g_ref[...]


def _final_norm(x, g, tm=256):
    N, D = x.shape
    tm = min(tm, N)
    return pl.pallas_call(
        _final_norm_kernel,
        grid=(N // tm,),
        in_specs=[pl.BlockSpec((tm, D), lambda i: (i, 0)), pl.BlockSpec((1, D), lambda i: (0, 0))],
        out_specs=pl.BlockSpec((tm, D), lambda i: (i, 0)),
        out_shape=jax.ShapeDtypeStruct((N, D), F32),
        compiler_params=_cparams(("arbitrary",)),
        name="final_norm",
    )(x, g.reshape(1, D))


def _mm_kernel(a_ref, b_ref, o_ref):
    o_ref[...] = jnp.dot(a_ref[...], b_ref[...], preferred_element_type=F32).astype(o_ref.dtype)


def _matmul(a, b, tn, out_dtype=BF16, tm=1024, name="matmul"):
    M, K = a.shape
    _, Nc = b.shape
    tm = min(tm, M)
    return pl.pallas_call(
        _mm_kernel,
        grid=(Nc // tn, M // tm),
        in_specs=[pl.BlockSpec((tm, K), lambda j, i: (i, 0)), pl.BlockSpec((K, tn), lambda j, i: (0, j))],
        out_specs=pl.BlockSpec((tm, tn), lambda j, i: (i, j)),
        out_shape=jax.ShapeDtypeStruct((M, Nc), out_dtype),
        compiler_params=_cparams(("arbitrary", "arbitrary")),
        name=name,
    )(a, b)


def _mm_glu_kernel(a_ref, bg_ref, bu_ref, o_ref):
    a = a_ref[...]
    g = jnp.dot(a, bg_ref[...], preferred_element_type=F32)
    u = jnp.dot(a, bu_ref[...], preferred_element_type=F32)
    o_ref[...] = (g * jax.nn.sigmoid(g) * u).astype(o_ref.dtype)


def _matmul_glu(a, bg, bu, tn=512, tm=1024):
    M, K = a.shape
    _, Nc = bg.shape
    tm = min(tm, M)
    return pl.pallas_call(
        _mm_glu_kernel,
        grid=(Nc // tn, M // tm),
        in_specs=[
            pl.BlockSpec((tm, K), lambda j, i: (i, 0)),
            pl.BlockSpec((K, tn), lambda j, i: (0, j)),
            pl.BlockSpec((K, tn), lambda j, i: (0, j)),
        ],
        out_specs=pl.BlockSpec((tm, tn), lambda j, i: (i, j)),
        out_shape=jax.ShapeDtypeStruct((M, Nc), BF16),
        compiler_params=_cparams(("arbitrary", "arbitrary")),
        name="shared_glu",
    )(a, bg, bu)


def _mm_residual_kernel(a_ref, b_ref, x_ref, mod_ref, o_ref, *, gate_row):
    acc = jnp.dot(a_ref[...], b_ref[...], preferred_element_type=F32)
    o_ref[...] = x_ref[...] + mod_ref[0, gate_row:gate_row + 1, :] * acc


def _matmul_residual(a, b, x, mod, gate_row, seq, tn=512, tm=1024):
    M, K = a.shape
    _, Nc = b.shape
    tm = min(tm, seq)
    per_b = seq // tm
    return pl.pallas_call(
        functools.partial(_mm_residual_kernel, gate_row=gate_row),
        grid=(Nc // tn, M // tm),
        in_specs=[
            pl.BlockSpec((tm, K), lambda j, i: (i, 0)),
            pl.BlockSpec((K, tn), lambda j, i: (0, j)),
            pl.BlockSpec((tm, tn), lambda j, i: (i, j)),
            pl.BlockSpec((1, 6, tn), lambda j, i: (i // per_b, 0, j)),
        ],
        out_specs=pl.BlockSpec((tm, tn), lambda j, i: (i, j)),
        out_shape=jax.ShapeDtypeStruct((M, Nc), F32),
        compiler_params=_cparams(("arbitrary", "arbitrary")),
        name="out_proj_residual",
    )(a, b, x, mod)


def _rope_table_kernel(pos_ref, inv128_ref, inv64_ref, c128_ref, s128_ref, c64_ref, s64_ref):
    pos = pos_ref[...]
    lane = lax.broadcasted_iota(jnp.int32, pos.shape, 1)
    a = pos * inv128_ref[...]
    c128_ref[...] = jnp.cos(a)
    s128_ref[...] = jnp.where(lane < 64, -jnp.sin(a), jnp.sin(a))
    a = pos * inv64_ref[...]
    c64_ref[...] = jnp.cos(a)
    s64_ref[...] = jnp.where((lane & 63) < 32, -jnp.sin(a), jnp.sin(a))


def _rope_tables(positions, tm=512):
    N = positions.size
    tm = min(tm, N)
    pos = jnp.broadcast_to(positions.astype(F32).reshape(N, 1), (N, LANES))

    def inv(half):
        f = np.float32(ROPE_THETA) ** (-(np.arange(half, dtype=np.float32) / np.float32(half)))
        return jnp.asarray(np.tile(f.astype(np.float32), LANES // half).reshape(1, LANES))

    tab = jax.ShapeDtypeStruct((N, LANES), F32)
    row = pl.BlockSpec((tm, LANES), lambda i: (i, 0))
    one = pl.BlockSpec((1, LANES), lambda i: (0, 0))
    return pl.pallas_call(
        _rope_table_kernel,
        grid=(N // tm,),
        in_specs=[row, one, one],
        out_specs=[row, row, row, row],
        out_shape=[tab, tab, tab, tab],
        compiler_params=_cparams(("arbitrary",)),
        name="rope_tables",
    )(pos, inv(64), inv(32))


def _rot128(x):
    return pltpu.roll(x, 64, 1)


def _rot64(x):
    lane = lax.broadcasted_iota(jnp.int32, x.shape, 1)
    return jnp.where((lane & 63) < 32, pltpu.roll(x, 96, 1), pltpu.roll(x, 32, 1))


def _r_block_plan():
    plan = []
    for _ in range(3):
        plan += [(128, HEAD_DIM ** -0.5)] * 4 + [(128, 1.0)] * 4
    plan += [(64, DIFF_QK_DIM ** -0.5)] * 8 + [(64, 1.0)] * 8
    plan += [(128, HEAD_DIM ** -0.5)] * 8 + [(128, 1.0)] * 2
    plan += [None] * ((R_COLS - R_SWA_K - 256) // LANES)
    return plan


def _rope_kernel(p_ref, c128_ref, s128_ref, c64_ref, s64_ref, o_ref):
    c128, s128, c64, s64 = c128_ref[...], s128_ref[...], c64_ref[...], s64_ref[...]
    for blk, spec in enumerate(_r_block_plan()):
        cols = slice(blk * LANES, (blk + 1) * LANES)
        if spec is None:
            o_ref[:, cols] = jnp.zeros((o_ref.shape[0], LANES), o_ref.dtype)
            continue
        flavour, scale = spec
        x = p_ref[:, cols].astype(F32)
        if flavour == 128:
            y = x * c128 + _rot128(x) * s128
        else:
            y = x * c64 + _rot64(x) * s64
        if scale != 1.0:
            y = y * scale
        o_ref[:, cols] = y.astype(o_ref.dtype)


def _rope_section(proj_r, tabs, tm=256):
    N = proj_r.shape[0]
    tm = min(tm, N)
    row = pl.BlockSpec((tm, R_COLS), lambda i: (i, 0))
    tab = pl.BlockSpec((tm, LANES), lambda i: (i, 0))
    return pl.pallas_call(
        _rope_kernel,
        grid=(N // tm,),
        in_specs=[row, tab, tab, tab, tab],
        out_specs=row,
        out_shape=jax.ShapeDtypeStruct((N, R_COLS), BF16),
        compiler_params=_cparams(("arbitrary",)),
        name="rope_section",
    )(proj_r, *tabs)


MLA_QW = 256


def _mla_prep_kernel(a_ref, gq_ref, gkv_ref, wq_ref, wkv_ref, c64_ref, s64_ref,
                     q_ref, kn_ref, kr_ref, v_ref):
    a = a_ref[...].astype(F32)
    c64, s64 = c64_ref[...], s64_ref[...]
    scale = (MLA_NOPE_DIM + MLA_ROPE_DIM) ** -0.5

    cq = a[:, :MLA_Q_RANK]
    cq = cq * lax.rsqrt(jnp.mean(cq * cq, axis=-1, keepdims=True) + NORM_EPS) * gq_ref[...]
    q = jnp.dot(cq.astype(BF16), wq_ref[...], preferred_element_type=F32)
    for h in range(MLA_HEADS):
        qn = q[:, h * MLA_QW:h * MLA_QW + LANES]
        qr = q[:, h * MLA_QW + LANES:(h + 1) * MLA_QW]
        qr = qr * c64 + _rot64(qr) * s64
        q_ref[:, h * MLA_QW:h * MLA_QW + LANES] = (qn * scale).astype(q_ref.dtype)
        q_ref[:, h * MLA_QW + LANES:(h + 1) * MLA_QW] = (qr * scale).astype(q_ref.dtype)

    ckv = a[:, MLA_Q_RANK:MLA_Q_RANK + MLA_KV_RANK]
    ckv = ckv * lax.rsqrt(jnp.mean(ckv * ckv, axis=-1, keepdims=True) + NORM_EPS) * gkv_ref[...]
    kv = jnp.dot(ckv.astype(BF16), wkv_ref[...], preferred_element_type=F32)
    kn_ref[...] = kv[:, :MLA_HEADS * MLA_NOPE_DIM].astype(kn_ref.dtype)
    v_ref[...] = kv[:, MLA_HEADS * MLA_NOPE_DIM:].astype(v_ref.dtype)

    kr = a[:, MLA_Q_RANK + MLA_KV_RANK:]
    kr_ref[...] = (kr * c64 + _rot64(kr) * s64).astype(kr_ref.dtype)


def _mla_prep(proj_a, gq, gkv, wq, wkv, c64, s64, tm=256):
    N = proj_a.shape[0]
    tm = min(tm, N)
    row = lambda w: pl.BlockSpec((tm, w), lambda i: (i, 0))
    full = lambda r, c: pl.BlockSpec((r, c), lambda i: (0, 0))
    hq = MLA_HEADS * MLA_QW
    hk = MLA_HEADS * MLA_NOPE_DIM
    return pl.pallas_call(
        _mla_prep_kernel,
        grid=(N // tm,),
        in_specs=[row(A_PAD), full(1, MLA_Q_RANK), full(1, MLA_KV_RANK), full(MLA_Q_RANK, hq),
                  full(MLA_KV_RANK, 2 * hk), row(LANES), row(LANES)],
        out_specs=[row(hq), row(hk), row(LANES), row(hk)],
        out_shape=[jax.ShapeDtypeStruct((N, hq), BF16), jax.ShapeDtypeStruct((N, hk), BF16),
                   jax.ShapeDtypeStruct((N, LANES), BF16), jax.ShapeDtypeStruct((N, hk), BF16)],
        compiler_params=_cparams(("arbitrary",)),
        name="mla_prep",
    )(proj_a, gq.reshape(1, -1), gkv.reshape(1, -1), wq, wkv, c64, s64)


def _softmax_pv(q, k_ref, v_ref, kc):
    tq = q.shape[0]
    S = k_ref.shape[0]
    m = jnp.full((tq, 1), -jnp.inf, F32)
    l = jnp.zeros((tq, 1), F32)
    acc = jnp.zeros((tq, v_ref.shape[1]), F32)
    for c in range(S // kc):
        s = _nt_dot(q, k_ref[c * kc:(c + 1) * kc, :])
        m_new = jnp.maximum(m, jnp.max(s, axis=-1, keepdims=True))
        alpha = jnp.exp(m - m_new)
        p = jnp.exp(s - m_new)
        l = alpha * l + jnp.sum(p, axis=-1, keepdims=True)
        acc = alpha * acc + jnp.dot(p.astype(BF16), v_ref[c * kc:(c + 1) * kc, :],
                                    preferred_element_type=F32)
        m = m_new
    return acc / l


def _mla_attn_kernel(q_ref, kn_ref, kr_ref, v_ref, o_ref, kcat_ref, *, kc):
    @pl.when(pl.program_id(2) == 0)
    def _():
        kcat_ref[:, :LANES] = kn_ref[...]
        kcat_ref[:, LANES:] = kr_ref[...]

    o_ref[...] = _softmax_pv(q_ref[...], kcat_ref, v_ref, kc).astype(o_ref.dtype)


def _mla_attention(qm, kn, kr, vm, batch, seq, tq=512, kc=1024):
    N = qm.shape[0]
    tq = min(tq, seq)
    kc = min(kc, seq)
    nq = seq // tq
    return pl.pallas_call(
        functools.partial(_mla_attn_kernel, kc=kc),
        grid=(batch, MLA_HEADS, nq),
        in_specs=[
            pl.BlockSpec((tq, MLA_QW), lambda b, h, i: (b * nq + i, h)),
            pl.BlockSpec((seq, LANES), lambda b, h, i: (b, h)),
            pl.BlockSpec((seq, LANES), lambda b, h, i: (b, 0)),
            pl.BlockSpec((seq, LANES), lambda b, h, i: (b, h)),
        ],
        out_specs=pl.BlockSpec((tq, LANES), lambda b, h, i: (b * nq + i, h)),
        out_shape=jax.ShapeDtypeStruct((N, MLA_HEADS * MLA_V_DIM), BF16),
        scratch_shapes=[pltpu.VMEM((seq, MLA_QW), BF16)],
        compiler_params=_cparams(("arbitrary", "arbitrary", "arbitrary")),
        name="mla_attention",
    )(qm, kn, kr, vm)


def _diff_attn_kernel(q_ref, k_ref, v_ref, lam_ref, g_ref, o_ref, *, kc, lam_init):
    q = q_ref[...]
    lane = lax.broadcasted_iota(jnp.int32, q.shape, 1)
    zero = jnp.zeros_like(q)
    o0 = _softmax_pv(jnp.where(lane < DIFF_QK_DIM, q, zero), k_ref, v_ref, kc)
    o1 = _softmax_pv(jnp.where(lane >= DIFF_QK_DIM, q, zero), k_ref, v_ref, kc)
    lp = lam_ref[...]
    lam = (jnp.exp(jnp.sum(lp[0:1] * lp[1:2], axis=-1, keepdims=True))
           - jnp.exp(jnp.sum(lp[2:3] * lp[3:4], axis=-1, keepdims=True)) + lam_init)
    o = o0 - lam * o1
    o = o * lax.rsqrt(jnp.mean(o * o, axis=-1, keepdims=True) + DIFF_NORM_EPS) * g_ref[...]
    o_ref[...] = (o * (1.0 - lam_init)).astype(o_ref.dtype)


def _diff_attention(rop, proj_v, lam_params, subln_g, lam_init, batch, seq, tq=512, kc=1024):
    N = rop.shape[0]
    tq = min(tq, seq)
    kc = min(kc, seq)
    nq = seq // tq
    qb, kb, vb = R_DIFF_Q // LANES, R_DIFF_K // LANES, V_DIFF // LANES
    return pl.pallas_call(
        functools.partial(_diff_attn_kernel, kc=kc, lam_init=lam_init),
        grid=(batch, DIFF_HEADS, nq),
        in_specs=[
            pl.BlockSpec((tq, LANES), lambda b, h, i: (b * nq + i, qb + h)),
            pl.BlockSpec((seq, LANES), lambda b, h, i: (b, kb + h)),
            pl.BlockSpec((seq, LANES), lambda b, h, i: (b, vb + h)),
            pl.BlockSpec((4, DIFF_QK_DIM), lambda b, h, i: (0, 0)),
            pl.BlockSpec((1, DIFF_V_DIM), lambda b, h, i: (0, 0)),
        ],
        out_specs=pl.BlockSpec((tq, LANES), lambda b, h, i: (b * nq + i, h)),
        out_shape=jax.ShapeDtypeStruct((N, DIFF_HEADS * DIFF_V_DIM), BF16),
        compiler_params=_cparams(("arbitrary", "arbitrary", "arbitrary")),
        name="diff_attention",
    )(rop, rop, proj_v, lam_params, subln_g.reshape(1, -1))


def _banded_kernel(*refs, half_window, blk, blocks_per_seq, grp, use_sink, want_lse):
    q_ref, kp_ref, kc_ref, kn_ref, vp_ref, vc_ref, vn_ref = refs[:7]
    rest = list(refs[7:])
    sink_ref = rest.pop(0) if use_sink else None
    o_ref = rest.pop(0)
    lse_ref = rest.pop(0) if want_lse else None

    il = lax.rem(pl.program_id(0), blocks_per_seq)
    qpos = il * blk + lax.broadcasted_iota(jnp.int32, (blk, 3 * blk), 0)
    kpos = (il - 1) * blk + lax.broadcasted_iota(jnp.int32, (blk, 3 * blk), 1)
    valid = ((jnp.abs(qpos - kpos) <= half_window) & (kpos >= 0) & (kpos < blocks_per_seq * blk))

    for h in range(DIL_HEADS):
        hk = h // grp
        cq = slice(h * LANES, (h + 1) * LANES)
        ck = slice(hk * LANES, (hk + 1) * LANES)
        k = jnp.concatenate([kp_ref[:, ck], kc_ref[:, ck], kn_ref[:, ck]], axis=0)
        v = jnp.concatenate([vp_ref[:, ck], vc_ref[:, ck], vn_ref[:, ck]], axis=0)
        s = jnp.where(valid, _nt_dot(q_ref[:, cq], k), -jnp.inf)
        m = jnp.max(s, axis=-1, keepdims=True)
        if use_sink:
            sk = sink_ref[0:1, h * LANES:h * LANES + 1]
            m = jnp.maximum(m, sk)
        p = jnp.exp(s - m)
        den = jnp.sum(p, axis=-1, keepdims=True)
        if use_sink:
            den = den + jnp.exp(sk - m)
        o = jnp.dot(p.astype(BF16), v, preferred_element_type=F32) / den
        o_ref[:, cq] = o.astype(o_ref.dtype)
        if want_lse:
            lse_ref[:, cq] = jnp.broadcast_to(m + jnp.log(den), (blk, LANES))


def _banded_attention(q_arr, k_arr, v_arr, *, seq, dilation, half_window, q_col, k_col, v_col,
                      n_col_groups, q_stride, k_stride, v_stride, kv_width, grp, sink=None,
                      want_lse=False):
    N = q_arr.shape[0]
    L = seq // dilation
    blk = min(128, L)
    blocks_per_seq = L // blk
    rows = N // dilation
    n_row_blocks = rows // blk
    qv = q_arr.reshape(rows, -1)
    kv = k_arr.reshape(rows, -1)
    vv = v_arr.reshape(rows, -1)
    last = n_row_blocks - 1

    def prev(i):
        return jnp.maximum(i - 1, 0)

    def nxt(i):
        return jnp.minimum(i + 1, last)

    qspec = pl.BlockSpec((blk, HW4), lambda i, c: (i, q_col + c * q_stride))
    kspecs = [pl.BlockSpec((blk, kv_width), lambda i, c, f=f: (f(i), k_col + c * k_stride))
              for f in (prev, lambda i: i, nxt)]
    vspecs = [pl.BlockSpec((blk, kv_width), lambda i, c, f=f: (f(i), v_col + c * v_stride))
              for f in (prev, lambda i: i, nxt)]
    in_specs = [qspec] + kspecs + vspecs
    args = [qv, kv, kv, kv, vv, vv, vv]
    if sink is not None:
        in_specs.append(pl.BlockSpec((1, HW4), lambda i, c: (0, c)))
        args.append(sink)
    out_cols = n_col_groups * HW4
    ospec = pl.BlockSpec((blk, HW4), lambda i, c: (i, c))
    out_specs = [ospec]
    out_shape = [jax.ShapeDtypeStruct((rows, out_cols), BF16)]
    if want_lse:
        out_specs.append(ospec)
        out_shape.append(jax.ShapeDtypeStruct((rows, out_cols), F32))
    outs = pl.pallas_call(
        functools.partial(_banded_kernel, half_window=half_window, blk=blk,
                          blocks_per_seq=blocks_per_seq, grp=grp, use_sink=sink is not None,
                          want_lse=want_lse),
        grid=(n_row_blocks, n_col_groups),
        in_specs=in_specs,
        out_specs=out_specs,
        out_shape=out_shape,
        compiler_params=_cparams(("arbitrary", "arbitrary")),
        name="banded_attention",
    )(*args)
    return [o.reshape(N, -1) for o in outs]


def _dil_combine_kernel(o0, o1, o2, l0, l1, l2, out_ref):
    a, b, c = l0[...], l1[...], l2[...]
    m = jnp.maximum(jnp.maximum(a, b), c)
    ea, eb, ec = jnp.exp(a - m), jnp.exp(b - m), jnp.exp(c - m)
    inv = 1.0 / (ea + eb + ec)
    out = (ea * inv) * o0[...].astype(F32) + (eb * inv) * o1[...].astype(F32) + (ec * inv) * o2[...].astype(F32)
    out_ref[...] = out.astype(out_ref.dtype)


def _dil_combine(outs, lses, tm=512):
    N = outs[0].shape[0]
    tm = min(tm, N)
    row = pl.BlockSpec((tm, HW4), lambda i: (i, 0))
    return pl.pallas_call(
        _dil_combine_kernel,
        grid=(N // tm,),
        in_specs=[row] * 6,
        out_specs=row,
        out_shape=jax.ShapeDtypeStruct((N, HW4), BF16),
        compiler_params=_cparams(("arbitrary",)),
        name="dil_combine",
    )(*outs, *lses)


def _merge_kernel(oa, ob, oc, od, wa, wb, wc, wd, ga, gb, gc, gd, o_ref):
    acc = None
    for o, w, g in ((oa, wa, ga), (ob, wb, gb), (oc, wc, gc), (od, wd, gd)):
        t = jax.nn.sigmoid(g[...].astype(F32)) * jnp.dot(o[...], w[...], preferred_element_type=F32)
        acc = t if acc is None else acc + t
    o_ref[...] = acc.astype(o_ref.dtype)


def _merge(branch_outs, w_branch, gates, tn=512, tm=1024):
    N = gates.shape[0]
    D = D_MODEL
    tm = min(tm, N)
    oa, ob, oc, od = branch_outs
    in_specs = [pl.BlockSpec((tm, o.shape[1]), lambda j, i: (i, 0)) for o in branch_outs]
    in_specs += [pl.BlockSpec((1024, tn), lambda j, i, r=r: (r, j)) for r in range(3)]
    in_specs += [pl.BlockSpec((512, tn), lambda j, i: (6, j))]
    nj = D // tn
    in_specs += [pl.BlockSpec((tm, tn), lambda j, i, r=r: (i, r * nj + j)) for r in range(4)]
    return pl.pallas_call(
        _merge_kernel,
        grid=(nj, N // tm),
        in_specs=in_specs,
        out_specs=pl.BlockSpec((tm, tn), lambda j, i: (i, j)),
        out_shape=jax.ShapeDtypeStruct((N, D), BF16),
        compiler_params=_cparams(("arbitrary", "arbitrary")),
        name="branch_merge",
    )(oa, ob, oc, od, w_branch, w_branch, w_branch, w_branch, gates, gates, gates, gates)


def _router_kernel(x_ref, g_ref, mod_ref, rw_ref, rb_ref, hb_ref, hf_ref, idx_ref, wt_ref, *,
                   shift_row, scale_row):
    x = x_ref[...]
    ms = jnp.mean(x * x, axis=-1, keepdims=True)
    y = x * lax.rsqrt(ms + NORM_EPS) * g_ref[...]
    h = y * (1.0 + mod_ref[0, scale_row:scale_row + 1, :]) + mod_ref[0, shift_row:shift_row + 1, :]
    hb_ref[...] = h.astype(hb_ref.dtype)
    hf_ref[...] = h

    tm = x.shape[0]
    logits = jnp.dot(h, rw_ref[...], preferred_element_type=F32, precision=lax.Precision.HIGHEST)
    lt = logits.T[:N_EXPERTS, :]
    scores = jax.nn.sigmoid(lt)
    choice = scores + rb_ref[...]

    per_group = N_EXPERTS // N_EXPERT_GROUPS
    sub = lax.broadcasted_iota(jnp.int32, (per_group, tm), 0)
    group_score = []
    for g in range(N_EXPERT_GROUPS):
        cg = choice[g * per_group:(g + 1) * per_group, :]
        m1 = jnp.max(cg, axis=0, keepdims=True)
        first = jnp.min(jnp.where(cg == m1, sub, per_group), axis=0, keepdims=True)
        m2 = jnp.max(jnp.where(sub == first, -jnp.inf, cg), axis=0, keepdims=True)
        group_score.append(m1 + m2)
    masked = []
    for g in range(N_EXPERT_GROUPS):
        ahead = jnp.zeros((1, tm), jnp.int32)
        for o in range(N_EXPERT_GROUPS):
            if o == g:
                continue
            better = (group_score[o] >= group_score[g]) if o < g else (group_score[o] > group_score[g])
            ahead = ahead + better.astype(jnp.int32)
        cg = choice[g * per_group:(g + 1) * per_group, :]
        masked.append(jnp.where(ahead < TOP_GROUPS, cg, -jnp.inf))
    cm = jnp.concatenate(masked, axis=0)

    eidx = lax.broadcasted_iota(jnp.int32, (N_EXPERTS, tm), 0)
    rank = jnp.zeros((N_EXPERTS, tm), jnp.int32)
    for e in range(N_EXPERTS):
        row = cm[e:e + 1, :]
        tie = jnp.where(eidx > e, 1, 0)
        rank = rank + jnp.where(row > cm, 1, jnp.where(row == cm, tie, 0))
    sel = rank < TOP_K
    wsel = jnp.where(sel, scores, 0.0)
    wsel = wsel / jnp.sum(wsel, axis=0, keepdims=True) * ROUTED_SCALE
    idx_rows, wt_rows = [], []
    for r in range(TOP_K):
        hit = rank == r
        idx_rows.append(jnp.sum(jnp.where(hit, eidx, 0), axis=0, keepdims=True))
        wt_rows.append(jnp.sum(jnp.where(hit, wsel, 0.0), axis=0, keepdims=True))
    idx_ref[...] = jnp.concatenate(idx_rows, axis=0)
    wt_ref[...] = jnp.concatenate(wt_rows, axis=0)


def _router(x, g, mod, shift_row, scale_row, router_w, router_bias, seq, tm=256):
    N, D = x.shape
    tm = min(tm, seq)
    per_b = seq // tm
    rw = jnp.zeros((D, LANES), F32).at[:, :N_EXPERTS].set(router_w)
    row = pl.BlockSpec((tm, D), lambda i: (i, 0))
    col = lambda r: pl.BlockSpec((r, tm), lambda i: (0, i))
    return pl.pallas_call(
        functools.partial(_router_kernel, shift_row=shift_row, scale_row=scale_row),
        grid=(N // tm,),
        in_specs=[
            row,
            pl.BlockSpec((1, D), lambda i: (0, 0)),
            pl.BlockSpec((1, 6, D), lambda i: (i // per_b, 0, 0)),
            pl.BlockSpec((D, LANES), lambda i: (0, 0)),
            pl.BlockSpec((N_EXPERTS, 1), lambda i: (0, 0)),
        ],
        out_specs=[row, row, col(TOP_K), col(TOP_K)],
        out_shape=[jax.ShapeDtypeStruct((N, D), BF16), jax.ShapeDtypeStruct((N, D), F32),
                   jax.ShapeDtypeStruct((TOP_K, N), jnp.int32), jax.ShapeDtypeStruct((TOP_K, N), F32)],
        compiler_params=_cparams(("arbitrary",)),
        name="ffn_norm_router",
    )(x, g.reshape(1, D), mod, rw, router_bias.reshape(N_EXPERTS, 1))


def _num_expert_blocks(n_tokens):
    bm = EXPERT_BLOCK_ROWS
    return -(-(n_tokens * TOP_K + N_EXPERTS * (bm - 1)) // bm)


def _dispatch_plan(top_idx):
    K, N = top_idx.shape
    bm = EXPERT_BLOCK_ROWS
    n_blocks = _num_expert_blocks(N)
    experts = jnp.arange(N_EXPERTS, dtype=jnp.int32)
    onehot = top_idx[:, None, :] == experts[None, :, None]
    mask = jnp.any(onehot, axis=0).astype(jnp.int32)
    counts = jnp.sum(mask, axis=1)
    pos = jnp.cumsum(mask, axis=1) - mask
    nblk = (counts + bm - 1) // bm
    blk_end = jnp.cumsum(nblk)
    blk_start = blk_end - nblk
    n_used = blk_end[-1]
    start = jnp.cumsum(counts) - counts
    slot_en = (blk_start * bm)[:, None] + pos
    slot_of = jnp.sum(jnp.where(onehot, slot_en[None], 0), axis=1)

    keys = top_idx * N + jnp.arange(N, dtype=jnp.int32)[None, :]
    tok_sorted = jnp.sort(keys.reshape(-1)) % N

    bidx = jnp.arange(n_blocks, dtype=jnp.int32)
    be = jnp.minimum(jnp.sum((bidx[:, None] >= blk_end[None, :]).astype(jnp.int32), axis=1),
                     N_EXPERTS - 1)
    be = jnp.where(bidx < n_used, be, be[jnp.maximum(n_used - 1, 0)])
    local = (bidx - blk_start[be]) * bm
    off = jnp.clip(start[be] + local, 0, K * N)
    tok_pad = jnp.concatenate([tok_sorted, jnp.zeros((bm,), jnp.int32)])
    slot_tok = jax.vmap(lambda o: lax.dynamic_slice(tok_pad, (o,), (bm,)))(off)
    valid = (local[:, None] + jnp.arange(bm, dtype=jnp.int32)[None, :] < counts[be][:, None])
    valid = valid & (bidx < n_used)[:, None]
    slot_tok = jnp.where(valid, slot_tok, 0).astype(jnp.int32)
    return be.astype(jnp.int32), slot_tok.reshape(-1), slot_of.astype(jnp.int32), n_used.astype(jnp.int32)


def _expert_kernel(be_ref, tok_ref, nu_ref, h_hbm, wg_ref, wu_ref, wd_ref, y_ref, xbuf, sem):
    bm = EXPERT_BLOCK_ROWS
    b = pl.program_id(0)
    n_used = nu_ref[0]

    def row_copy(blk, slot, r):
        tok = tok_ref[blk * bm + r]
        return pltpu.make_async_copy(h_hbm.at[pl.ds(tok, 1), :], xbuf.at[slot, pl.ds(r, 1), :],
                                     sem.at[slot])

    def gather(blk, slot):
        def body(r, carry):
            row_copy(blk, slot, r).start()
            return carry
        lax.fori_loop(0, bm, body, 0)

    @pl.when(b == 0)
    def _():
        gather(0, 0)

    @pl.when(b + 1 < n_used)
    def _():
        gather(b + 1, (b + 1) % 2)

    @pl.when(b < n_used)
    def _():
        slot = b % 2
        pltpu.make_async_copy(xbuf.at[slot], xbuf.at[slot], sem.at[slot]).wait()
        x = xbuf[slot].astype(BF16)
        g = jnp.dot(x, wg_ref[0, 0].astype(BF16), preferred_element_type=F32)
        u = jnp.dot(x, wu_ref[0, 0].astype(BF16), preferred_element_type=F32)
        a = (g * jax.nn.sigmoid(g) * u).astype(BF16)
        y_ref[...] = jnp.dot(a, wd_ref[0, 0].astype(BF16), preferred_element_type=F32)

    @pl.when(b >= n_used)
    def _():
        y_ref[...] = jnp.zeros(y_ref.shape, y_ref.dtype)


def _expert_ffn(h_f32, w_gate, w_up, w_down, layer, block_expert, slot_tok, n_used):
    N, D = h_f32.shape
    bm = EXPERT_BLOCK_ROWS
    n_blocks = block_expert.shape[0]
    grid_spec = pltpu.PrefetchScalarGridSpec(
        num_scalar_prefetch=3,
        grid=(n_blocks,),
        in_specs=[
            pl.BlockSpec(memory_space=pl.ANY),
            pl.BlockSpec((1, 1, D, EXPERT_FF), lambda b, be, tok, nu: (layer, be[b], 0, 0)),
            pl.BlockSpec((1, 1, D, EXPERT_FF), lambda b, be, tok, nu: (layer, be[b], 0, 0)),
            pl.BlockSpec((1, 1, EXPERT_FF, D), lambda b, be, tok, nu: (layer, be[b], 0, 0)),
        ],
        out_specs=pl.BlockSpec((bm, D), lambda b, be, tok, nu: (b, 0)),
        scratch_shapes=[pltpu.VMEM((2, bm, D), F32), pltpu.SemaphoreType.DMA((2,))],
    )
    return pl.pallas_call(
        _expert_kernel,
        grid_spec=grid_spec,
        out_shape=jax.ShapeDtypeStruct((n_blocks * bm, D), F32),
        compiler_params=_cparams(("arbitrary",)),
        name="expert_ffn",
    )(block_expert, slot_tok, n_used.reshape(1), h_f32, w_gate, w_up, w_down)


def _combine_kernel(slot_ref, y_hbm, x_ref, sh_ref, w_ref, mod_ref, o_ref, ybuf, sem, *,
                    gate_row, n_tokens):
    tc = COMBINE_TOKENS
    i = pl.program_id(0)
    n_steps = pl.num_programs(0)

    def gather(step, slot):
        def body(t, carry):
            for k in range(TOP_K):
                s = slot_ref[k * n_tokens + step * tc + t]
                pltpu.make_async_copy(y_hbm.at[pl.ds(s, 1), :], ybuf.at[slot, k, pl.ds(t, 1), :],
                                      sem.at[slot]).start()
            return carry
        lax.fori_loop(0, tc, body, 0)

    @pl.when(i == 0)
    def _():
        gather(0, 0)

    @pl.when(i + 1 < n_steps)
    def _():
        gather(i + 1, (i + 1) % 2)

    slot = i % 2
    pltpu.make_async_copy(ybuf.at[slot], ybuf.at[slot], sem.at[slot]).wait()
    acc = sh_ref[...].astype(F32)
    w = w_ref[...]
    for k in range(TOP_K):
        acc = acc + w[:, k:k + 1] * ybuf[slot, k]
    o_ref[...] = x_ref[...] + mod_ref[0, gate_row:gate_row + 1, :] * acc


def _combine(y, slot_of, x, shared, w_tok, mod, gate_row, seq):
    N, D = x.shape
    tc = COMBINE_TOKENS
    per_b = seq // tc
    grid_spec = pltpu.PrefetchScalarGridSpec(
        num_scalar_prefetch=1,
        grid=(N // tc,),
        in_specs=[
            pl.BlockSpec(memory_space=pl.ANY),
            pl.BlockSpec((tc, D), lambda i, s: (i, 0)),
            pl.BlockSpec((tc, D), lambda i, s: (i, 0)),
            pl.BlockSpec((tc, TOP_K), lambda i, s: (i, 0)),
            pl.BlockSpec((1, 6, D), lambda i, s: (i // per_b, 0, 0)),
        ],
        out_specs=pl.BlockSpec((tc, D), lambda i, s: (i, 0)),
        scratch_shapes=[pltpu.VMEM((2, TOP_K, tc, D), F32), pltpu.SemaphoreType.DMA((2,))],
    )
    return pl.pallas_call(
        functools.partial(_combine_kernel, gate_row=gate_row, n_tokens=N),
        grid_spec=grid_spec,
        out_shape=jax.ShapeDtypeStruct((N, D), F32),
        compiler_params=_cparams(("arbitrary",)),
        name="expert_combine",
    )(slot_of.reshape(-1), y, x, shared, w_tok, mod)


def _split_w_in(w):
    D = w.shape[0]
    z = lambda n: jnp.zeros((D, n), w.dtype)
    dil_qk = [w[:, OFF_D + g * D_GROUP_COLS:OFF_D + g * D_GROUP_COLS + 2 * HW4] for g in range(3)]
    dil_v = [w[:, OFF_D + g * D_GROUP_COLS + 2 * HW4:OFF_D + (g + 1) * D_GROUP_COLS] for g in range(3)]
    w_r = jnp.concatenate(dil_qk + [w[:, OFF_B:OFF_B + 2048], w[:, OFF_C:OFF_C + 1280],
                                    z(R_COLS - R_SWA_K - 256)], axis=1)
    w_v = jnp.concatenate(dil_v + [w[:, OFF_B + 2048:OFF_B + 3072], w[:, OFF_C + 1280:OFF_C + 1536],
                                   z(V_COLS - V_SWA - 256)], axis=1)
    w_a = jnp.concatenate([w[:, :A_COLS], z(A_PAD - A_COLS)], axis=1)
    w_g = w[:, OFF_G:]
    return w_r.astype(BF16), w_v.astype(BF16), w_a.astype(BF16), w_g.astype(BF16)


def _mla_weights(w_uq, w_ukv):
    qd = MLA_NOPE_DIM + MLA_ROPE_DIM
    wq = w_uq.reshape(MLA_Q_RANK, MLA_HEADS, qd)
    wq = jnp.concatenate([wq, jnp.zeros((MLA_Q_RANK, MLA_HEADS, MLA_QW - qd), wq.dtype)], axis=-1)
    wkv = w_ukv.reshape(MLA_KV_RANK, MLA_HEADS, MLA_NOPE_DIM + MLA_V_DIM)
    wkv = jnp.concatenate([wkv[:, :, :MLA_NOPE_DIM].reshape(MLA_KV_RANK, -1),
                           wkv[:, :, MLA_NOPE_DIM:].reshape(MLA_KV_RANK, -1)], axis=-1)
    return wq.reshape(MLA_Q_RANK, MLA_HEADS * MLA_QW).astype(BF16), wkv.astype(BF16)


def kernel(x, c, positions, w_ada, b_ada, mix_norm_g, ffn_norm_g, w_in, mla_q_norm_g, mla_w_uq,
           mla_kv_norm_g, mla_w_ukv, diff_lambda, diff_subln_g, swa_sink, w_branch, w_out,
           router_w, router_bias, expert_w_gate, expert_w_up, expert_w_down,
           shared_w_gate, shared_w_up, shared_w_down, final_norm_g):
    B, S, D = x.shape
    N = B * S
    xf = x.reshape(N, D)
    mod_all = _ada_mod(c, w_ada, b_ada)
    tabs = _rope_tables(positions)
    c64, s64 = tabs[2], tabs[3]

    for l in range(DEPTH):
        mod = mod_all[l, :B].reshape(B, 6, D)

        h = _norm_mod(xf, mix_norm_g[l], mod, 0, 1, S)
        w_r, w_v, w_a, w_g = _split_w_in(w_in[l])
        proj_r = _matmul(h, w_r, tn=512, name="proj_rotary")
        proj_v = _matmul(h, w_v, tn=1024, name="proj_value")
        proj_a = _matmul(h, w_a, tn=A_PAD, name="proj_latent")
        gates = _matmul(h, w_g, tn=1024, name="proj_gates")
        rop = _rope_section(proj_r, tabs)

        wq, wkv = _mla_weights(mla_w_uq[l], mla_w_ukv[l])
        qm, kn, kr, vm = _mla_prep(proj_a, mla_q_norm_g[l], mla_kv_norm_g[l], wq, wkv, c64, s64)
        out_a = _mla_attention(qm, kn, kr, vm, B, S)

        lam_init = 0.8 - 0.6 * math.exp(-0.3 * l)
        out_b = _diff_attention(rop, proj_v, diff_lambda[l], diff_subln_g[l], lam_init, B, S)

        sink = jnp.repeat(swa_sink[l].astype(F32), LANES).reshape(1, SWA_HEADS * LANES)
        (out_c,) = _banded_attention(
            rop, rop, proj_v, seq=S, dilation=1, half_window=SWA_HALF_WINDOW,
            q_col=R_SWA_Q // HW4, k_col=R_SWA_K // LANES, v_col=V_SWA // LANES,
            n_col_groups=SWA_KV_HEADS, q_stride=1, k_stride=1, v_stride=1, kv_width=LANES,
            grp=SWA_HEADS // SWA_KV_HEADS, sink=sink)

        d_outs, d_lses = [], []
        for g, (window, dil) in enumerate(DIL_PAIRS):
            o, lse = _banded_attention(
                rop, rop, proj_v, seq=S, dilation=dil, half_window=window // (2 * dil),
                q_col=2 * g, k_col=2 * g + 1, v_col=g,
                n_col_groups=dil, q_stride=R_COLS // HW4, k_stride=R_COLS // HW4,
                v_stride=V_COLS // HW4, kv_width=HW4, grp=1, want_lse=True)
            d_outs.append(o)
            d_lses.append(lse)
        out_d = _dil_combine(d_outs, d_lses)

        merged = _merge((out_a, out_b, out_c, out_d), w_branch[l].astype(BF16), gates)
        xf = _matmul_residual(merged, w_out[l].astype(BF16), xf, mod, 2, S)

        hb, hf, top_idx, top_w = _router(xf, ffn_norm_g[l], mod, 3, 4, router_w[l], router_bias[l], S)
        block_expert, slot_tok, slot_of, n_used = _dispatch_plan(top_idx)
        y = _expert_ffn(hf, expert_w_gate, expert_w_up, expert_w_down, l, block_expert, slot_tok, n_used)
        act = _matmul_glu(hb, shared_w_gate[l].astype(BF16), shared_w_up[l].astype(BF16))
        shared = _matmul(act, shared_w_down[l].astype(BF16), tn=1024, name="shared_down")
        xf = _combine(y, slot_of, xf, shared, top_w.T, mod, 5, S)

    return _final_norm(xf, final_norm_g).reshape(B, S, D)
```

```python
import functools
import math

import numpy as np
import jax
import jax.numpy as jnp
from jax import lax
from jax.experimental import pallas as pl
from jax.experimental.pallas import tpu as pltpu

F32 = jnp.float32
BF16 = jnp.bfloat16

D_MODEL = 4096
DEPTH = 2
HEAD_DIM = 128
ROPE_THETA = 10000.0
NORM_EPS = 1e-6

MLA_HEADS = 8
MLA_Q_RANK = 768
MLA_KV_RANK = 256
MLA_NOPE_DIM = 128
MLA_ROPE_DIM = 64
MLA_V_DIM = 128

DIFF_HEADS = 8
DIFF_QK_DIM = 64
DIFF_V_DIM = 128
DIFF_NORM_EPS = 1e-5

SWA_HEADS = 8
SWA_KV_HEADS = 2
SWA_HALF_WINDOW = 128

DIL_PAIRS = ((128, 1), (512, 4), (2048, 16))
DIL_HEADS = 4

N_EXPERTS = 64
N_EXPERT_GROUPS = 8
TOP_GROUPS = 4
TOP_K = 8
EXPERT_FF = 256
SHARED_FF = 1024
ROUTED_SCALE = 2.5

A_COLS = MLA_Q_RANK + MLA_KV_RANK + MLA_ROPE_DIM
B_COLS = 3 * DIFF_HEADS * DIFF_V_DIM
C_COLS = (SWA_HEADS + 2 * SWA_KV_HEADS) * HEAD_DIM
D_GROUP_COLS = 3 * DIL_HEADS * HEAD_DIM
D_COLS = len(DIL_PAIRS) * D_GROUP_COLS
OFF_B = A_COLS
OFF_C = OFF_B + B_COLS
OFF_D = OFF_C + C_COLS
OFF_G = OFF_D + D_COLS

LANES = 128
HW4 = DIL_HEADS * HEAD_DIM

R_DIL0_Q = 0
R_DIL0_K = HW4
R_DIFF_Q = 2 * HW4
R_DIFF_K = R_DIFF_Q + 1024
R_SWA_Q = R_DIFF_K + 1024
R_SWA_K = R_SWA_Q + 1024
R_USED = R_SWA_K + 256
R_COLS = 4608
X_GROUP = 3 * HW4
X_COLS = 2 * X_GROUP
V_DIL0 = 0
V_DIFF = HW4
V_SWA = V_DIFF + 1024
V_USED = V_SWA + 256
V_COLS = 2048
A_PAD = 1152

EXPERT_BLOCK_ROWS = 256
COMBINE_TOKENS = 32
PACK_HALF = D_MODEL // 2
PACK_SLABS = PACK_HALF // LANES
VMEM_LIMIT = 56 * 1024 * 1024


def _cparams(sem, vmem=VMEM_LIMIT):
    return pltpu.CompilerParams(dimension_semantics=sem, vmem_limit_bytes=vmem)


def _nt_dot(a, b):
    return lax.dot_general(a, b, (((1,), (1,)), ((), ())), preferred_element_type=F32)


def _ada_kernel(c_ref, w_ref, b_ref, o_ref):
    c = c_ref[...]
    s = (c * jax.nn.sigmoid(c)).astype(BF16)
    w = w_ref[0].astype(BF16)
    o_ref[0] = jnp.dot(s, w, preferred_element_type=F32) + b_ref[0]


def _ada_mod(c, w_ada, b_ada):
    B, D = c.shape
    cp = jnp.zeros((8, D), F32).at[:B].set(c)
    n6 = w_ada.shape[-1]
    tn = 512
    return pl.pallas_call(
        _ada_kernel,
        grid=(DEPTH, n6 // tn),
        in_specs=[
            pl.BlockSpec((8, D), lambda l, j: (0, 0)),
            pl.BlockSpec((1, D, tn), lambda l, j: (l, 0, j)),
            pl.BlockSpec((1, 1, tn), lambda l, j: (l, 0, j)),
        ],
        out_specs=pl.BlockSpec((1, 8, tn), lambda l, j: (l, 0, j)),
        out_shape=jax.ShapeDtypeStruct((DEPTH, 8, n6), F32),
        compiler_params=_cparams(("arbitrary", "arbitrary")),
        name="ada_mod",
    )(cp, w_ada, b_ada.reshape(DEPTH, 1, n6))


def _norm_mod_kernel(x_ref, g_ref, mod_ref, o_ref, *, shift_row, scale_row):
    x = x_ref[...]
    ms = jnp.mean(x * x, axis=-1, keepdims=True)
    y = x * lax.rsqrt(ms + NORM_EPS) * g_ref[...]
    h = y * (1.0 + mod_ref[0, scale_row:scale_row + 1, :]) + mod_ref[0, shift_row:shift_row + 1, :]
    o_ref[...] = h.astype(o_ref.dtype)


def _norm_mod(x, g, mod, shift_row, scale_row, seq, tm=256):
    N, D = x.shape
    tm = min(tm, seq)
    per_b = seq // tm
    return pl.pallas_call(
        functools.partial(_norm_mod_kernel, shift_row=shift_row, scale_row=scale_row),
        grid=(N // tm,),
        in_specs=[
            pl.BlockSpec((tm, D), lambda i: (i, 0)),
            pl.BlockSpec((1, D), lambda i: (0, 0)),
            pl.BlockSpec((1, 6, D), lambda i: (i // per_b, 0, 0)),
        ],
        out_specs=pl.BlockSpec((tm, D), lambda i: (i, 0)),
        out_shape=jax.ShapeDtypeStruct((N, D), BF16),
        compiler_params=_cparams(("arbitrary",)),
        name="norm_mod",
    )(x, g.reshape(1, D), mod)


def _final_norm_kernel(x_ref, g_ref, o_ref):
    x = x_ref[...]
    ms = jnp.mean(x * x, axis=-1, keepdims=True)
    o_ref[...] = x * lax.rsqrt(ms + NORM_EPS) * g_ref[...]


def _final_norm(x, g, tm=256):
    N, D = x.shape
    tm = min(tm, N)
    return pl.pallas_call(
        _final_norm_kernel,
        grid=(N // tm,),
        in_specs=[pl.BlockSpec((tm, D), lambda i: (i, 0)), pl.BlockSpec((1, D), lambda i: (0, 0))],
        out_specs=pl.BlockSpec((tm, D), lambda i: (i, 0)),
        out_shape=jax.ShapeDtypeStruct((N, D), F32),
        compiler_params=_cparams(("arbitrary",)),
        name="final_norm",
    )(x, g.reshape(1, D))


def _mm_kernel(a_ref, b_ref, o_ref):
    o_ref[...] = jnp.dot(a_ref[...], b_ref[...], preferred_element_type=F32).astype(o_ref.dtype)


def _matmul(a, b, tn, out_dtype=BF16, tm=1024, name="matmul"):
    M, K = a.shape
    _, Nc = b.shape
    tm = min(tm, M)
    return pl.pallas_call(
        _mm_kernel,
        grid=(Nc // tn, M // tm),
        in_specs=[pl.BlockSpec((tm, K), lambda j, i: (i, 0)), pl.BlockSpec((K, tn), lambda j, i: (0, j))],
        out_specs=pl.BlockSpec((tm, tn), lambda j, i: (i, j)),
        out_shape=jax.ShapeDtypeStruct((M, Nc), out_dtype),
        compiler_params=_cparams(("arbitrary", "arbitrary")),
        name=name,
    )(a, b)


def _mm_glu_kernel(a_ref, bg_ref, bu_ref, o_ref):
    a = a_ref[...]
    g = jnp.dot(a, bg_ref[...], preferred_element_type=F32)
    u = jnp.dot(a, bu_ref[...], preferred_element_type=F32)
    o_ref[...] = (g * jax.nn.sigmoid(g) * u).astype(o_ref.dtype)


def _matmul_glu(a, bg, bu, tn=512, tm=1024):
    M, K = a.shape
    _, Nc = bg.shape
    tm = min(tm, M)
    return pl.pallas_call(
        _mm_glu_kernel,
        grid=(Nc // tn, M // tm),
        in_specs=[
            pl.BlockSpec((tm, K), lambda j, i: (i, 0)),
            pl.BlockSpec((K, tn), lambda j, i: (0, j)),
            pl.BlockSpec((K, tn), lambda j, i: (0, j)),
        ],
        out_specs=pl.BlockSpec((tm, tn), lambda j, i: (i, j)),
        out_shape=jax.ShapeDtypeStruct((M, Nc), BF16),
        compiler_params=_cparams(("arbitrary", "arbitrary")),
        name="shared_glu",
    )(a, bg, bu)


def _mm_residual_kernel(a_ref, b_ref, x_ref, mod_ref, o_ref, *, gate_row):
    acc = jnp.dot(a_ref[...], b_ref[...], preferred_element_type=F32)
    o_ref[...] = x_ref[...] + mod_ref[0, gate_row:gate_row + 1, :] * acc


def _matmul_residual(a, b, x, mod, gate_row, seq, tn=512, tm=1024):
    M, K = a.shape
    _, Nc = b.shape
    tm = min(tm, seq)
    per_b = seq // tm
    return pl.pallas_call(
        functools.partial(_mm_residual_kernel, gate_row=gate_row),
        grid=(Nc // tn, M // tm),
        in_specs=[
            pl.BlockSpec((tm, K), lambda j, i: (i, 0)),
            pl.BlockSpec((K, tn), lambda j, i: (0, j)),
            pl.BlockSpec((tm, tn), lambda j, i: (i, j)),
            pl.BlockSpec((1, 6, tn), lambda j, i: (i // per_b, 0, j)),
        ],
        out_specs=pl.BlockSpec((tm, tn), lambda j, i: (i, j)),
        out_shape=jax.ShapeDtypeStruct((M, Nc), F32),
        compiler_params=_cparams(("arbitrary", "arbitrary")),
        name="out_proj_residual",
    )(a, b, x, mod)


def _rope_table_kernel(pos_ref, inv128_ref, inv64_ref, c128_ref, s128_ref, c64_ref, s64_ref):
    pos = pos_ref[...]
    lane = lax.broadcasted_iota(jnp.int32, pos.shape, 1)
    a = pos * inv128_ref[...]
    c128_ref[...] = jnp.cos(a)
    s128_ref[...] = jnp.where(lane < 64, -jnp.sin(a), jnp.sin(a))
    a = pos * inv64_ref[...]
    c64_ref[...] = jnp.cos(a)
    s64_ref[...] = jnp.where((lane & 63) < 32, -jnp.sin(a), jnp.sin(a))


def _rope_tables(positions, tm=512):
    N = positions.size
    tm = min(tm, N)
    pos = jnp.broadcast_to(positions.astype(F32).reshape(N, 1), (N, LANES))

    def inv(half):
        f = np.float32(ROPE_THETA) ** (-(np.arange(half, dtype=np.float32) / np.float32(half)))
        return jnp.asarray(np.tile(f.astype(np.float32), LANES // half).reshape(1, LANES))

    tab = jax.ShapeDtypeStruct((N, LANES), F32)
    row = pl.BlockSpec((tm, LANES), lambda i: (i, 0))
    one = pl.BlockSpec((1, LANES), lambda i: (0, 0))
    return pl.pallas_call(
        _rope_table_kernel,
        grid=(N // tm,),
        in_specs=[row, one, one],
        out_specs=[row, row, row, row],
        out_shape=[tab, tab, tab, tab],
        compiler_params=_cparams(("arbitrary",)),
        name="rope_tables",
    )(pos, inv(64), inv(32))


def _rot128(x):
    return pltpu.roll(x, 64, 1)


def _rot64(x):
    lane = lax.broadcasted_iota(jnp.int32, x.shape, 1)
    return jnp.where((lane & 63) < 32, pltpu.roll(x, 96, 1), pltpu.roll(x, 32, 1))


def _r_block_plan():
    plan = [(128, HEAD_DIM ** -0.5)] * 4 + [(128, 1.0)] * 4
    plan += [(64, DIFF_QK_DIM ** -0.5)] * 8 + [(64, 1.0)] * 8
    plan += [(128, HEAD_DIM ** -0.5)] * 8 + [(128, 1.0)] * 2
    plan += [None] * ((R_COLS - R_USED) // LANES)
    return plan


def _rope_kernel(p_ref, c128_ref, s128_ref, c64_ref, s64_ref, o_ref):
    c128, s128, c64, s64 = c128_ref[...], s128_ref[...], c64_ref[...], s64_ref[...]
    for blk, spec in enumerate(_r_block_plan()):
        cols = slice(blk * LANES, (blk + 1) * LANES)
        if spec is None:
            o_ref[:, cols] = jnp.zeros((o_ref.shape[0], LANES), o_ref.dtype)
            continue
        flavour, scale = spec
        x = p_ref[:, cols].astype(F32)
        if flavour == 128:
            y = x * c128 + _rot128(x) * s128
        else:
            y = x * c64 + _rot64(x) * s64
        if scale != 1.0:
            y = y * scale
        o_ref[:, cols] = y.astype(o_ref.dtype)


def _rope_section(proj_r, tabs, tm=256):
    N = proj_r.shape[0]
    tm = min(tm, N)
    row = pl.BlockSpec((tm, R_COLS), lambda i: (i, 0))
    tab = pl.BlockSpec((tm, LANES), lambda i: (i, 0))
    return pl.pallas_call(
        _rope_kernel,
        grid=(N // tm,),
        in_specs=[row, tab, tab, tab, tab],
        out_specs=row,
        out_shape=jax.ShapeDtypeStruct((N, R_COLS), BF16),
        compiler_params=_cparams(("arbitrary",)),
        name="rope_section",
    )(proj_r, *tabs)


def _dil_rope_kernel(x_ref, c128_ref, s128_ref, *refs, dils, tm):
    scr = refs[-1]
    outs = refs[:-1]
    c128, s128 = c128_ref[...], s128_ref[...]
    for g, d in enumerate(dils):
        for part in range(3):
            o_ref = outs[g * 3 + part]
            base = (g * 3 + part) * DIL_HEADS
            for c in range(DIL_HEADS):
                col = g * X_GROUP + part * HW4 + c * LANES
                x = x_ref[:, col:col + LANES].astype(F32)
                if part < 2:
                    x = x * c128 + _rot128(x) * s128
                if part == 0:
                    x = x * (HEAD_DIM ** -0.5)
                scr[base + c] = x
            for r in range(d):
                for c in range(DIL_HEADS):
                    o_ref[r, :, c * LANES:(c + 1) * LANES] = (
                        scr[base + c, pl.ds(r, tm // d, stride=d), :].astype(o_ref.dtype))


def _dil_rope_section(proj_x, c128, s128, batch, seq, dils, tm=256):
    N = proj_x.shape[0]
    tm = min(tm, seq)
    per_b = seq // tm
    tab = pl.BlockSpec((tm, LANES), lambda i: (i, 0))
    out_specs, out_shape = [], []
    for d in dils:
        for _ in range(3):
            out_specs.append(pl.BlockSpec((d, tm // d, HW4), lambda i: (i // per_b, i % per_b, 0)))
            out_shape.append(jax.ShapeDtypeStruct((batch * d, seq // d, HW4), BF16))
    outs = pl.pallas_call(
        functools.partial(_dil_rope_kernel, dils=dils, tm=tm),
        grid=(N // tm,),
        in_specs=[pl.BlockSpec((tm, X_COLS), lambda i: (i, 0)), tab, tab],
        out_specs=out_specs,
        out_shape=out_shape,
        scratch_shapes=[pltpu.VMEM((len(dils) * 3 * DIL_HEADS, tm, LANES), F32)],
        compiler_params=_cparams(("arbitrary",)),
        name="dil_rope_section",
    )(proj_x, c128, s128)
    return [outs[3 * g:3 * g + 3] for g in range(len(dils))]


MLA_QW = 256


def _mla_prep_kernel(a_ref, gq_ref, gkv_ref, wq_ref, wkv_ref, c64_ref, s64_ref,
                     q_ref, kn_ref, kr_ref, v_ref):
    a = a_ref[...].astype(F32)
    c64, s64 = c64_ref[...], s64_ref[...]
    scale = (MLA_NOPE_DIM + MLA_ROPE_DIM) ** -0.5

    cq = a[:, :MLA_Q_RANK]
    cq = cq * lax.rsqrt(jnp.mean(cq * cq, axis=-1, keepdims=True) + NORM_EPS) * gq_ref[...]
    q = jnp.dot(cq.astype(BF16), wq_ref[...], preferred_element_type=F32)
    for h in range(MLA_HEADS):
        qn = q[:, h * MLA_QW:h * MLA_QW + LANES]
        qr = q[:, h * MLA_QW + LANES:(h + 1) * MLA_QW]
        qr = qr * c64 + _rot64(qr) * s64
        q_ref[:, h * MLA_QW:h * MLA_QW + LANES] = (qn * scale).astype(q_ref.dtype)
        q_ref[:, h * MLA_QW + LANES:(h + 1) * MLA_QW] = (qr * scale).astype(q_ref.dtype)

    ckv = a[:, MLA_Q_RANK:MLA_Q_RANK + MLA_KV_RANK]
    ckv = ckv * lax.rsqrt(jnp.mean(ckv * ckv, axis=-1, keepdims=True) + NORM_EPS) * gkv_ref[...]
    kv = jnp.dot(ckv.astype(BF16), wkv_ref[...], preferred_element_type=F32)
    kn_ref[...] = kv[:, :MLA_HEADS * MLA_NOPE_DIM].astype(kn_ref.dtype)
    v_ref[...] = kv[:, MLA_HEADS * MLA_NOPE_DIM:].astype(v_ref.dtype)

    kr = a[:, MLA_Q_RANK + MLA_KV_RANK:]
    kr_ref[...] = (kr * c64 + _rot64(kr) * s64).astype(kr_ref.dtype)


def _mla_prep(proj_a, gq, gkv, wq, wkv, c64, s64, tm=256):
    N = proj_a.shape[0]
    tm = min(tm, N)
    row = lambda w: pl.BlockSpec((tm, w), lambda i: (i, 0))
    full = lambda r, c: pl.BlockSpec((r, c), lambda i: (0, 0))
    hq = MLA_HEADS * MLA_QW
    hk = MLA_HEADS * MLA_NOPE_DIM
    return pl.pallas_call(
        _mla_prep_kernel,
        grid=(N // tm,),
        in_specs=[row(A_PAD), full(1, MLA_Q_RANK), full(1, MLA_KV_RANK), full(MLA_Q_RANK, hq),
                  full(MLA_KV_RANK, 2 * hk), row(LANES), row(LANES)],
        out_specs=[row(hq), row(hk), row(LANES), row(hk)],
        out_shape=[jax.ShapeDtypeStruct((N, hq), BF16), jax.ShapeDtypeStruct((N, hk), BF16),
                   jax.ShapeDtypeStruct((N, LANES), BF16), jax.ShapeDtypeStruct((N, hk), BF16)],
        compiler_params=_cparams(("arbitrary",)),
        name="mla_prep",
    )(proj_a, gq.reshape(1, -1), gkv.reshape(1, -1), wq, wkv, c64, s64)


def _softmax_pv(q, k_ref, v_ref, kc):
    tq = q.shape[0]
    S = k_ref.shape[0]
    m = jnp.full((tq, 1), -jnp.inf, F32)
    l = jnp.zeros((tq, 1), F32)
    acc = jnp.zeros((tq, v_ref.shape[1]), F32)
    for c in range(S // kc):
        s = _nt_dot(q, k_ref[c * kc:(c + 1) * kc, :])
        m_new = jnp.maximum(m, jnp.max(s, axis=-1, keepdims=True))
        alpha = jnp.exp(m - m_new)
        p = jnp.exp(s - m_new)
        l = alpha * l + jnp.sum(p, axis=-1, keepdims=True)
        acc = alpha * acc + jnp.dot(p.astype(BF16), v_ref[c * kc:(c + 1) * kc, :],
                                    preferred_element_type=F32)
        m = m_new
    return acc / l


def _mla_attn_kernel(q_ref, kn_ref, kr_ref, v_ref, o_ref, kcat_ref, *, kc):
    @pl.when(pl.program_id(2) == 0)
    def _():
        kcat_ref[:, :LANES] = kn_ref[...]
        kcat_ref[:, LANES:] = kr_ref[...]

    o_ref[...] = _softmax_pv(q_ref[...], kcat_ref, v_ref, kc).astype(o_ref.dtype)


def _mla_attention(qm, kn, kr, vm, batch, seq, tq=512, kc=1024):
    N = qm.shape[0]
    tq = min(tq, seq)
    kc = min(kc, seq)
    nq = seq // tq
    return pl.pallas_call(
        functools.partial(_mla_attn_kernel, kc=kc),
        grid=(batch, MLA_HEADS, nq),
        in_specs=[
            pl.BlockSpec((tq, MLA_QW), lambda b, h, i: (b * nq + i, h)),
            pl.BlockSpec((seq, LANES), lambda b, h, i: (b, h)),
            pl.BlockSpec((seq, LANES), lambda b, h, i: (b, 0)),
            pl.BlockSpec((seq, LANES), lambda b, h, i: (b, h)),
        ],
        out_specs=pl.BlockSpec((tq, LANES), lambda b, h, i: (b * nq + i, h)),
        out_shape=jax.ShapeDtypeStruct((N, MLA_HEADS * MLA_V_DIM), BF16),
        scratch_shapes=[pltpu.VMEM((seq, MLA_QW), BF16)],
        compiler_params=_cparams(("arbitrary", "arbitrary", "arbitrary")),
        name="mla_attention",
    )(qm, kn, kr, vm)


def _diff_attn_kernel(q_ref, k_ref, v_ref, lam_ref, g_ref, o_ref, *, kc, lam_init):
    q = q_ref[...]
    lane = lax.broadcasted_iota(jnp.int32, q.shape, 1)
    zero = jnp.zeros_like(q)
    o0 = _softmax_pv(jnp.where(lane < DIFF_QK_DIM, q, zero), k_ref, v_ref, kc)
    o1 = _softmax_pv(jnp.where(lane >= DIFF_QK_DIM, q, zero), k_ref, v_ref, kc)
    lp = lam_ref[...]
    lam = (jnp.exp(jnp.sum(lp[0:1] * lp[1:2], axis=-1, keepdims=True))
           - jnp.exp(jnp.sum(lp[2:3] * lp[3:4], axis=-1, keepdims=True)) + lam_init)
    o = o0 - lam * o1
    o = o * lax.rsqrt(jnp.mean(o * o, axis=-1, keepdims=True) + DIFF_NORM_EPS) * g_ref[...]
    o_ref[...] = (o * (1.0 - lam_init)).astype(o_ref.dtype)


def _diff_attention(rop, proj_v, lam_params, subln_g, lam_init, batch, seq, tq=512, kc=1024):
    N = rop.shape[0]
    tq = min(tq, seq)
    kc = min(kc, seq)
    nq = seq // tq
    qb, kb, vb = R_DIFF_Q // LANES, R_DIFF_K // LANES, V_DIFF // LANES
    return pl.pallas_call(
        functools.partial(_diff_attn_kernel, kc=kc, lam_init=lam_init),
        grid=(batch, DIFF_HEADS, nq),
        in_specs=[
            pl.BlockSpec((tq, LANES), lambda b, h, i: (b * nq + i, qb + h)),
            pl.BlockSpec((seq, LANES), lambda b, h, i: (b, kb + h)),
            pl.BlockSpec((seq, LANES), lambda b, h, i: (b, vb + h)),
            pl.BlockSpec((4, DIFF_QK_DIM), lambda b, h, i: (0, 0)),
            pl.BlockSpec((1, DIFF_V_DIM), lambda b, h, i: (0, 0)),
        ],
        out_specs=pl.BlockSpec((tq, LANES), lambda b, h, i: (b * nq + i, h)),
        out_shape=jax.ShapeDtypeStruct((N, DIFF_HEADS * DIFF_V_DIM), BF16),
        compiler_params=_cparams(("arbitrary", "arbitrary", "arbitrary")),
        name="diff_attention",
    )(rop, rop, proj_v, lam_params, subln_g.reshape(1, -1))


def _banded_kernel(*refs, half_window, blk, blocks_per_seq, grp, use_sink, want_lse):
    q_ref, kp_ref, kc_ref, kn_ref, vp_ref, vc_ref, vn_ref = refs[:7]
    rest = list(refs[7:])
    sink_ref = rest.pop(0) if use_sink else None
    o_ref = rest.pop(0)
    lse_ref = rest.pop(0) if want_lse else None

    il = lax.rem(pl.program_id(0), blocks_per_seq)
    qpos = il * blk + lax.broadcasted_iota(jnp.int32, (blk, 3 * blk), 0)
    kpos = (il - 1) * blk + lax.broadcasted_iota(jnp.int32, (blk, 3 * blk), 1)
    valid = ((jnp.abs(qpos - kpos) <= half_window) & (kpos >= 0) & (kpos < blocks_per_seq * blk))

    for h in range(DIL_HEADS):
        hk = h // grp
        cq = slice(h * LANES, (h + 1) * LANES)
        ck = slice(hk * LANES, (hk + 1) * LANES)
        k = jnp.concatenate([kp_ref[:, ck], kc_ref[:, ck], kn_ref[:, ck]], axis=0)
        v = jnp.concatenate([vp_ref[:, ck], vc_ref[:, ck], vn_ref[:, ck]], axis=0)
        s = jnp.where(valid, _nt_dot(q_ref[:, cq], k), -jnp.inf)
        m = jnp.max(s, axis=-1, keepdims=True)
        if use_sink:
            sk = sink_ref[0:1, h * LANES:h * LANES + 1]
            m = jnp.maximum(m, sk)
        p = jnp.exp(s - m)
        den = jnp.sum(p, axis=-1, keepdims=True)
        if use_sink:
            den = den + jnp.exp(sk - m)
        o = jnp.dot(p.astype(BF16), v, preferred_element_type=F32) / den
        o_ref[:, cq] = o.astype(o_ref.dtype)
        if want_lse:
            lse_ref[:, cq] = jnp.broadcast_to(m + jnp.log(den), (blk, LANES))


def _banded_attention(q_arr, k_arr, v_arr, *, sub_len, half_window, q_col, k_col, v_col,
                      n_col_groups, kv_width, grp, sink=None, want_lse=False):
    rows = q_arr.shape[0]
    blk = min(128, sub_len)
    blocks_per_seq = sub_len // blk
    n_row_blocks = rows // blk
    last = n_row_blocks - 1

    def prev(i):
        return jnp.maximum(i - 1, 0)

    def nxt(i):
        return jnp.minimum(i + 1, last)

    qspec = pl.BlockSpec((blk, HW4), lambda i, c: (i, q_col + c))
    kspecs = [pl.BlockSpec((blk, kv_width), lambda i, c, f=f: (f(i), k_col + c))
              for f in (prev, lambda i: i, nxt)]
    vspecs = [pl.BlockSpec((blk, kv_width), lambda i, c, f=f: (f(i), v_col + c))
              for f in (prev, lambda i: i, nxt)]
    in_specs = [qspec] + kspecs + vspecs
    args = [q_arr, k_arr, k_arr, k_arr, v_arr, v_arr, v_arr]
    if sink is not None:
        in_specs.append(pl.BlockSpec((1, HW4), lambda i, c: (0, c)))
        args.append(sink)
    out_cols = n_col_groups * HW4
    ospec = pl.BlockSpec((blk, HW4), lambda i, c: (i, c))
    out_specs = [ospec]
    out_shape = [jax.ShapeDtypeStruct((rows, out_cols), BF16)]
    if want_lse:
        out_specs.append(ospec)
        out_shape.append(jax.ShapeDtypeStruct((rows, out_cols), F32))
    return pl.pallas_call(
        functools.partial(_banded_kernel, half_window=half_window, blk=blk,
                          blocks_per_seq=blocks_per_seq, grp=grp, use_sink=sink is not None,
                          want_lse=want_lse),
        grid=(n_row_blocks, n_col_groups),
        in_specs=in_specs,
        out_specs=out_specs,
        out_shape=out_shape,
        compiler_params=_cparams(("arbitrary", "arbitrary")),
        name="banded_attention",
    )(*args)


def _dil_combine_kernel(o0, l0, o1, l1, o2, l2, out_ref, scr, *, dils, tm):
    def natural(o_ref, l_ref, d, base):
        for r in range(d):
            for c in range(DIL_HEADS):
                cols = slice(c * LANES, (c + 1) * LANES)
                scr[base + c, pl.ds(r, tm // d, stride=d), :] = o_ref[r, :, cols].astype(F32)
                scr[base + DIL_HEADS + c, pl.ds(r, tm // d, stride=d), :] = l_ref[r, :, cols]

    natural(o1, l1, dils[0], 0)
    natural(o2, l2, dils[1], 2 * DIL_HEADS)
    for c in range(DIL_HEADS):
        cols = slice(c * LANES, (c + 1) * LANES)
        oa, la = o0[:, cols].astype(F32), l0[:, cols]
        ob, lb = scr[c], scr[DIL_HEADS + c]
        oc, lc = scr[2 * DIL_HEADS + c], scr[3 * DIL_HEADS + c]
        m = jnp.maximum(jnp.maximum(la, lb), lc)
        ea, eb, ec = jnp.exp(la - m), jnp.exp(lb - m), jnp.exp(lc - m)
        inv = 1.0 / (ea + eb + ec)
        out_ref[:, cols] = ((ea * inv) * oa + (eb * inv) * ob + (ec * inv) * oc).astype(out_ref.dtype)


def _dil_combine(o0, l0, o1, l1, o2, l2, batch, seq, dils, tm=256):
    N = o0.shape[0]
    tm = min(tm, seq)
    per_b = seq // tm
    row = pl.BlockSpec((tm, HW4), lambda i: (i, 0))
    grouped = [pl.BlockSpec((d, tm // d, HW4), lambda i: (i // per_b, i % per_b, 0)) for d in dils]
    shaped = lambda a, d: a.reshape(batch * d, seq // d, HW4)
    return pl.pallas_call(
        functools.partial(_dil_combine_kernel, dils=dils, tm=tm),
        grid=(N // tm,),
        in_specs=[row, row, grouped[0], grouped[0], grouped[1], grouped[1]],
        out_specs=row,
        out_shape=jax.ShapeDtypeStruct((N, HW4), BF16),
        scratch_shapes=[pltpu.VMEM((4 * DIL_HEADS, tm, LANES), F32)],
        compiler_params=_cparams(("arbitrary",)),
        name="dil_combine",
    )(o0, l0, shaped(o1, dils[0]), shaped(l1, dils[0]), shaped(o2, dils[1]), shaped(l2, dils[1]))


def _merge_kernel(oa, ob, oc, od, wa, wb, wc, wd, ga, gb, gc, gd, o_ref):
    acc = None
    for o, w, g in ((oa, wa, ga), (ob, wb, gb), (oc, wc, gc), (od, wd, gd)):
        t = jax.nn.sigmoid(g[...].astype(F32)) * jnp.dot(o[...], w[...], preferred_element_type=F32)
        acc = t if acc is None else acc + t
    o_ref[...] = acc.astype(o_ref.dtype)


def _merge(branch_outs, w_branch, gates, tn=512, tm=1024):
    N = gates.shape[0]
    D = D_MODEL
    tm = min(tm, N)
    oa, ob, oc, od = branch_outs
    in_specs = [pl.BlockSpec((tm, o.shape[1]), lambda j, i: (i, 0)) for o in branch_outs]
    in_specs += [pl.BlockSpec((1024, tn), lambda j, i, r=r: (r, j)) for r in range(3)]
    in_specs += [pl.BlockSpec((512, tn), lambda j, i: (6, j))]
    nj = D // tn
    in_specs += [pl.BlockSpec((tm, tn), lambda j, i, r=r: (i, r * nj + j)) for r in range(4)]
    return pl.pallas_call(
        _merge_kernel,
        grid=(nj, N // tm),
        in_specs=in_specs,
        out_specs=pl.BlockSpec((tm, tn), lambda j, i: (i, j)),
        out_shape=jax.ShapeDtypeStruct((N, D), BF16),
        compiler_params=_cparams(("arbitrary", "arbitrary")),
        name="branch_merge",
    )(oa, ob, oc, od, w_branch, w_branch, w_branch, w_branch, gates, gates, gates, gates)


def _pack_pairs(lo, hi):
    lo_b = lax.bitcast_convert_type(lo.astype(BF16).astype(F32), jnp.uint32)
    hi_b = lax.bitcast_convert_type(hi.astype(BF16).astype(F32), jnp.uint32)
    return (lo_b >> 16) | (hi_b & jnp.uint32(0xFFFF0000))


def _unpack_pairs(w):
    lo = lax.bitcast_convert_type(w << 16, F32)
    hi = lax.bitcast_convert_type(w & jnp.uint32(0xFFFF0000), F32)
    return lo, hi


def _store_packed(o_ref, x):
    for j in range(PACK_SLABS):
        o_ref[:, j, :] = _pack_pairs(x[:, j * LANES:(j + 1) * LANES],
                                     x[:, PACK_HALF + j * LANES:PACK_HALF + (j + 1) * LANES])


def _router_kernel(x_ref, g_ref, mod_ref, rw_ref, rb_ref, hb_ref, hp_ref, idx_ref, wt_ref, *,
                   shift_row, scale_row):
    x = x_ref[...]
    ms = jnp.mean(x * x, axis=-1, keepdims=True)
    y = x * lax.rsqrt(ms + NORM_EPS) * g_ref[...]
    h = y * (1.0 + mod_ref[0, scale_row:scale_row + 1, :]) + mod_ref[0, shift_row:shift_row + 1, :]
    hb_ref[...] = h.astype(hb_ref.dtype)
    _store_packed(hp_ref, h)

    tm = x.shape[0]
    logits = jnp.dot(h, rw_ref[...], preferred_element_type=F32, precision=lax.Precision.HIGHEST)
    lt = logits.T[:N_EXPERTS, :]
    scores = jax.nn.sigmoid(lt)
    choice = scores + rb_ref[...]

    per_group = N_EXPERTS // N_EXPERT_GROUPS
    sub = lax.broadcasted_iota(jnp.int32, (per_group, tm), 0)
    group_score = []
    for g in range(N_EXPERT_GROUPS):
        cg = choice[g * per_group:(g + 1) * per_group, :]
        m1 = jnp.max(cg, axis=0, keepdims=True)
        first = jnp.min(jnp.where(cg == m1, sub, per_group), axis=0, keepdims=True)
        m2 = jnp.max(jnp.where(sub == first, -jnp.inf, cg), axis=0, keepdims=True)
        group_score.append(m1 + m2)
    masked = []
    for g in range(N_EXPERT_GROUPS):
        ahead = jnp.zeros((1, tm), jnp.int32)
        for o in range(N_EXPERT_GROUPS):
            if o == g:
                continue
            better = (group_score[o] >= group_score[g]) if o < g else (group_score[o] > group_score[g])
            ahead = ahead + better.astype(jnp.int32)
        cg = choice[g * per_group:(g + 1) * per_group, :]
        masked.append(jnp.where(ahead < TOP_GROUPS, cg, -jnp.inf))
    cm = jnp.concatenate(masked, axis=0)

    eidx = lax.broadcasted_iota(jnp.int32, (N_EXPERTS, tm), 0)
    rank = jnp.zeros((N_EXPERTS, tm), jnp.int32)
    for e in range(N_EXPERTS):
        row = cm[e:e + 1, :]
        tie = jnp.where(eidx > e, 1, 0)
        rank = rank + jnp.where(row > cm, 1, jnp.where(row == cm, tie, 0))
    sel = rank < TOP_K
    wsel = jnp.where(sel, scores, 0.0)
    wsel = wsel / jnp.sum(wsel, axis=0, keepdims=True) * ROUTED_SCALE
    idx_rows, wt_rows = [], []
    for r in range(TOP_K):
        hit = rank == r
        idx_rows.append(jnp.sum(jnp.where(hit, eidx, 0), axis=0, keepdims=True))
        wt_rows.append(jnp.sum(jnp.where(hit, wsel, 0.0), axis=0, keepdims=True))
    idx_ref[...] = jnp.concatenate(idx_rows, axis=0)
    wt_ref[...] = jnp.concatenate(wt_rows, axis=0)


def _router(x, g, mod, shift_row, scale_row, router_w, router_bias, seq, tm=256):
    N, D = x.shape
    tm = min(tm, seq)
    per_b = seq // tm
    rw = jnp.zeros((D, LANES), F32).at[:, :N_EXPERTS].set(router_w)
    row = pl.BlockSpec((tm, D), lambda i: (i, 0))
    col = lambda r: pl.BlockSpec((r, tm), lambda i: (0, i))
    return pl.pallas_call(
        functools.partial(_router_kernel, shift_row=shift_row, scale_row=scale_row),
        grid=(N // tm,),
        in_specs=[
            row,
            pl.BlockSpec((1, D), lambda i: (0, 0)),
            pl.BlockSpec((1, 6, D), lambda i: (i // per_b, 0, 0)),
            pl.BlockSpec((D, LANES), lambda i: (0, 0)),
            pl.BlockSpec((N_EXPERTS, 1), lambda i: (0, 0)),
        ],
        out_specs=[row, pl.BlockSpec((tm, PACK_SLABS, LANES), lambda i: (i, 0, 0)), col(TOP_K), col(TOP_K)],
        out_shape=[jax.ShapeDtypeStruct((N, D), BF16), jax.ShapeDtypeStruct((N, PACK_SLABS, LANES), jnp.uint32),
                   jax.ShapeDtypeStruct((TOP_K, N), jnp.int32), jax.ShapeDtypeStruct((TOP_K, N), F32)],
        compiler_params=_cparams(("arbitrary",)),
        name="ffn_norm_router",
    )(x, g.reshape(1, D), mod, rw, router_bias.reshape(N_EXPERTS, 1))


def _num_expert_blocks(n_tokens):
    bm = EXPERT_BLOCK_ROWS
    return -(-(n_tokens * TOP_K + N_EXPERTS * (bm - 1)) // bm)


def _dispatch_plan(top_idx):
    K, N = top_idx.shape
    bm = EXPERT_BLOCK_ROWS
    n_blocks = _num_expert_blocks(N)
    experts = jnp.arange(N_EXPERTS, dtype=jnp.int32)
    onehot = top_idx[:, None, :] == experts[None, :, None]
    mask = jnp.any(onehot, axis=0).astype(jnp.int32)
    counts = jnp.sum(mask, axis=1)
    pos = jnp.cumsum(mask, axis=1) - mask
    nblk = (counts + bm - 1) // bm
    blk_end = jnp.cumsum(nblk)
    blk_start = blk_end - nblk
    n_used = blk_end[-1]
    start = jnp.cumsum(counts) - counts
    slot_en = (blk_start * bm)[:, None] + pos
    slot_of = jnp.sum(jnp.where(onehot, slot_en[None], 0), axis=1)

    keys = top_idx * N + jnp.arange(N, dtype=jnp.int32)[None, :]
    tok_sorted = jnp.sort(keys.reshape(-1)) % N

    bidx = jnp.arange(n_blocks + 1, dtype=jnp.int32)
    be = jnp.minimum(jnp.sum((bidx[:, None] >= blk_end[None, :]).astype(jnp.int32), axis=1),
                     N_EXPERTS - 1)
    be = jnp.where(bidx < n_used, be, be[jnp.maximum(n_used - 1, 0)])
    off = jnp.clip(start[be] + (bidx - blk_start[be]) * bm, 0, K * N)
    tok_pad = jnp.concatenate([tok_sorted, jnp.zeros((bm,), jnp.int32)])
    return (be.astype(jnp.int32), off.astype(jnp.int32), tok_pad.astype(jnp.int32),
            slot_of.astype(jnp.int32), n_used.astype(jnp.int32))


def _expert_kernel(be_ref, off_ref, tok_ref, nu_ref, h_hbm, wg_ref, wu_ref, wd_ref, y_ref,
                   xbuf0, xbuf1, xs, wg_s, wu_s, wd_s, sem):
    bm = EXPERT_BLOCK_ROWS
    b = pl.program_id(0)
    n_used = nu_ref[0]
    bufs = (xbuf0, xbuf1)

    def start_gather(blk, slot, unrolled):
        base = off_ref[blk]

        def one(r):
            pltpu.make_async_copy(h_hbm.at[tok_ref[base + r]], bufs[slot].at[r], sem.at[slot]).start()

        if unrolled:
            for r in range(bm):
                one(r)
        else:
            def body(r, carry):
                one(r)
                return carry
            lax.fori_loop(0, bm, body, 0)

    def wait_gather(slot):
        pltpu.make_async_copy(bufs[slot], bufs[slot], sem.at[slot]).wait()

    @pl.when(b == 0)
    def _():
        start_gather(0, 0, False)

    new_expert = (b == 0) | (be_ref[b] != be_ref[jnp.maximum(b - 1, 0)])

    @pl.when(new_expert & (b < n_used))
    def _():
        wg_s[...] = wg_ref[0, 0].astype(BF16)
        wu_s[...] = wu_ref[0, 0].astype(BF16)
        wd_s[...] = wd_ref[0, 0].astype(BF16)

    def work(slot):
        wait_gather(slot)
        for j in range(PACK_SLABS):
            lo, hi = _unpack_pairs(bufs[slot][:, j, :])
            xs[:, j * LANES:(j + 1) * LANES] = lo.astype(BF16)
            xs[:, PACK_HALF + j * LANES:PACK_HALF + (j + 1) * LANES] = hi.astype(BF16)
        start_gather(b + 1, 1 - slot, True)
        x = xs[...]
        g = jnp.dot(x, wg_s[...], preferred_element_type=F32)
        u = jnp.dot(x, wu_s[...], preferred_element_type=F32)
        a = (g * jax.nn.sigmoid(g) * u).astype(BF16)
        _store_packed(y_ref, jnp.dot(a, wd_s[...], preferred_element_type=F32))

    for slot in (0, 1):
        @pl.when((b < n_used) & (b % 2 == slot))
        def _(slot=slot):
            work(slot)

        @pl.when((b == n_used) & (b % 2 == slot))
        def _(slot=slot):
            wait_gather(slot)

    @pl.when(b >= n_used)
    def _():
        y_ref[...] = jnp.zeros(y_ref.shape, y_ref.dtype)


def _expert_ffn(h_packed, w_gate, w_up, w_down, layer, block_expert, block_off, tok_pad, n_used):
    bm = EXPERT_BLOCK_ROWS
    D = D_MODEL
    n_steps = block_expert.shape[0]
    wspec = lambda r, c: pl.BlockSpec((1, 1, r, c), lambda b, be, off, tok, nu: (layer, be[b], 0, 0))
    grid_spec = pltpu.PrefetchScalarGridSpec(
        num_scalar_prefetch=4,
        grid=(n_steps,),
        in_specs=[pl.BlockSpec(memory_space=pl.ANY), wspec(D, EXPERT_FF), wspec(D, EXPERT_FF),
                  wspec(EXPERT_FF, D)],
        out_specs=pl.BlockSpec((bm, PACK_SLABS, LANES), lambda b, be, off, tok, nu: (b, 0, 0)),
        scratch_shapes=[pltpu.VMEM((bm, PACK_SLABS, LANES), jnp.uint32),
                        pltpu.VMEM((bm, PACK_SLABS, LANES), jnp.uint32),
                        pltpu.VMEM((bm, D), BF16),
                        pltpu.VMEM((D, EXPERT_FF), BF16), pltpu.VMEM((D, EXPERT_FF), BF16),
                        pltpu.VMEM((EXPERT_FF, D), BF16),
                        pltpu.SemaphoreType.DMA((2,))],
    )
    return pl.pallas_call(
        _expert_kernel,
        grid_spec=grid_spec,
        out_shape=jax.ShapeDtypeStruct((n_steps * bm, PACK_SLABS, LANES), jnp.uint32),
        compiler_params=_cparams(("arbitrary",)),
        name="expert_ffn",
    )(block_expert, block_off, tok_pad, n_used.reshape(1), h_packed, w_gate, w_up, w_down)


def _combine_kernel(slot_ref, y_hbm, x_ref, sh_ref, w_ref, mod_ref, o_ref, ybuf0, ybuf1, sem, *,
                    gate_row, n_tokens):
    tc = COMBINE_TOKENS
    i = pl.program_id(0)
    n_steps = pl.num_programs(0)
    bufs = (ybuf0, ybuf1)

    def start_gather(step, slot, unrolled):
        base = step * tc

        def one(t, k):
            s = slot_ref[k * n_tokens + base + t]
            pltpu.make_async_copy(y_hbm.at[s], bufs[slot].at[k, t], sem.at[slot]).start()

        if unrolled:
            for t in range(tc):
                for k in range(TOP_K):
                    one(t, k)
        else:
            def body(t, carry):
                for k in range(TOP_K):
                    one(t, k)
                return carry
            lax.fori_loop(0, tc, body, 0)

    @pl.when(i == 0)
    def _():
        start_gather(0, 0, False)

    def work(slot, prefetch):
        pltpu.make_async_copy(bufs[slot], bufs[slot], sem.at[slot]).wait()
        if prefetch:
            start_gather(i + 1, 1 - slot, True)
        w = w_ref[...]
        wk = [w[:, k:k + 1] for k in range(TOP_K)]
        for j in range(PACK_SLABS):
            c_lo = slice(j * LANES, (j + 1) * LANES)
            c_hi = slice(PACK_HALF + j * LANES, PACK_HALF + (j + 1) * LANES)
            acc_lo = sh_ref[:, c_lo].astype(F32)
            acc_hi = sh_ref[:, c_hi].astype(F32)
            for k in range(TOP_K):
                lo, hi = _unpack_pairs(bufs[slot][k, :, j, :])
                acc_lo = acc_lo + wk[k] * lo
                acc_hi = acc_hi + wk[k] * hi
            o_ref[:, c_lo] = x_ref[:, c_lo] + mod_ref[0, gate_row:gate_row + 1, c_lo] * acc_lo
            o_ref[:, c_hi] = x_ref[:, c_hi] + mod_ref[0, gate_row:gate_row + 1, c_hi] * acc_hi

    for slot in (0, 1):
        for prefetch in (True, False):
            @pl.when((i % 2 == slot) & ((i + 1 < n_steps) == prefetch))
            def _(slot=slot, prefetch=prefetch):
                work(slot, prefetch)


def _combine(y_packed, slot_of, x, shared, w_tok, mod, gate_row, seq):
    N, D = x.shape
    tc = COMBINE_TOKENS
    per_b = seq // tc
    grid_spec = pltpu.PrefetchScalarGridSpec(
        num_scalar_prefetch=1,
        grid=(N // tc,),
        in_specs=[
            pl.BlockSpec(memory_space=pl.ANY),
            pl.BlockSpec((tc, D), lambda i, s: (i, 0)),
            pl.BlockSpec((tc, D), lambda i, s: (i, 0)),
            pl.BlockSpec((tc, TOP_K), lambda i, s: (i, 0)),
            pl.BlockSpec((1, 6, D), lambda i, s: (i // per_b, 0, 0)),
        ],
        out_specs=pl.BlockSpec((tc, D), lambda i, s: (i, 0)),
        scratch_shapes=[pltpu.VMEM((TOP_K, tc, PACK_SLABS, LANES), jnp.uint32),
                        pltpu.VMEM((TOP_K, tc, PACK_SLABS, LANES), jnp.uint32),
                        pltpu.SemaphoreType.DMA((2,))],
    )
    return pl.pallas_call(
        functools.partial(_combine_kernel, gate_row=gate_row, n_tokens=N),
        grid_spec=grid_spec,
        out_shape=jax.ShapeDtypeStruct((N, D), F32),
        compiler_params=_cparams(("arbitrary",)),
        name="expert_combine",
    )(slot_of.reshape(-1), y_packed, x, shared, w_tok, mod)


def _split_w_in(w):
    D = w.shape[0]
    z = lambda n: jnp.zeros((D, n), w.dtype)
    w_r = jnp.concatenate([w[:, OFF_D:OFF_D + 2 * HW4], w[:, OFF_B:OFF_B + 2048], w[:, OFF_C:OFF_C + 1280],
                           z(R_COLS - R_USED)], axis=1)
    w_x = w[:, OFF_D + D_GROUP_COLS:OFF_D + 3 * D_GROUP_COLS]
    w_v = jnp.concatenate([w[:, OFF_D + 2 * HW4:OFF_D + 3 * HW4], w[:, OFF_B + 2048:OFF_B + 3072],
                           w[:, OFF_C + 1280:OFF_C + 1536], z(V_COLS - V_USED)], axis=1)
    w_a = jnp.concatenate([w[:, :A_COLS], z(A_PAD - A_COLS)], axis=1)
    w_g = w[:, OFF_G:]
    return tuple(t.astype(BF16) for t in (w_r, w_x, w_v, w_a, w_g))


def _mla_weights(w_uq, w_ukv):
    qd = MLA_NOPE_DIM + MLA_ROPE_DIM
    wq = w_uq.reshape(MLA_Q_RANK, MLA_HEADS, qd)
    wq = jnp.concatenate([wq, jnp.zeros((MLA_Q_RANK, MLA_HEADS, MLA_QW - qd), wq.dtype)], axis=-1)
    wkv = w_ukv.reshape(MLA_KV_RANK, MLA_HEADS, MLA_NOPE_DIM + MLA_V_DIM)
    wkv = jnp.concatenate([wkv[:, :, :MLA_NOPE_DIM].reshape(MLA_KV_RANK, -1),
                           wkv[:, :, MLA_NOPE_DIM:].reshape(MLA_KV_RANK, -1)], axis=-1)
    return wq.reshape(MLA_Q_RANK, MLA_HEADS * MLA_QW).astype(BF16), wkv.astype(BF16)


def kernel(x, c, positions, w_ada, b_ada, mix_norm_g, ffn_norm_g, w_in, mla_q_norm_g, mla_w_uq,
           mla_kv_norm_g, mla_w_ukv, diff_lambda, diff_subln_g, swa_sink, w_branch, w_out,
           router_w, router_bias, expert_w_gate, expert_w_up, expert_w_down,
           shared_w_gate, shared_w_up, shared_w_down, final_norm_g):
    B, S, D = x.shape
    N = B * S
    xf = x.reshape(N, D)
    mod_all = _ada_mod(c, w_ada, b_ada)
    tabs = _rope_tables(positions)
    c64, s64 = tabs[2], tabs[3]

    for l in range(DEPTH):
        mod = mod_all[l, :B].reshape(B, 6, D)

        h = _norm_mod(xf, mix_norm_g[l], mod, 0, 1, S)
        w_r, w_x, w_v, w_a, w_g = _split_w_in(w_in[l])
        proj_r = _matmul(h, w_r, tn=512, name="proj_rotary")
        proj_x = _matmul(h, w_x, tn=1024, name="proj_dilated")
        proj_v = _matmul(h, w_v, tn=1024, name="proj_value")
        proj_a = _matmul(h, w_a, tn=A_PAD, name="proj_latent")
        gates = _matmul(h, w_g, tn=1024, name="proj_gates")
        rop = _rope_section(proj_r, tabs)
        dil_groups = _dil_rope_section(proj_x, tabs[0], tabs[1], B, S, tuple(d for _, d in DIL_PAIRS[1:]))

        wq, wkv = _mla_weights(mla_w_uq[l], mla_w_ukv[l])
        qm, kn, kr, vm = _mla_prep(proj_a, mla_q_norm_g[l], mla_kv_norm_g[l], wq, wkv, c64, s64)
        out_a = _mla_attention(qm, kn, kr, vm, B, S)

        lam_init = 0.8 - 0.6 * math.exp(-0.3 * l)
        out_b = _diff_attention(rop, proj_v, diff_lambda[l], diff_subln_g[l], lam_init, B, S)

        sink = jnp.repeat(swa_sink[l].astype(F32), LANES).reshape(1, SWA_HEADS * LANES)
        (out_c,) = _banded_attention(
            rop, rop, proj_v, sub_len=S, half_window=SWA_HALF_WINDOW,
            q_col=R_SWA_Q // HW4, k_col=R_SWA_K // LANES, v_col=V_SWA // LANES,
            n_col_groups=SWA_KV_HEADS, kv_width=LANES, grp=SWA_HEADS // SWA_KV_HEADS, sink=sink)

        window0, _ = DIL_PAIRS[0]
        d_res = list(_banded_attention(
            rop, rop, proj_v, sub_len=S, half_window=window0 // 2, q_col=R_DIL0_Q // HW4,
            k_col=R_DIL0_K // HW4, v_col=V_DIL0 // HW4, n_col_groups=1, kv_width=HW4, grp=1, want_lse=True))
        for (window, dil), (qg, kg, vg) in zip(DIL_PAIRS[1:], dil_groups):
            flat = lambda a: a.reshape(N, HW4)
            d_res += _banded_attention(
                flat(qg), flat(kg), flat(vg), sub_len=S // dil, half_window=window // (2 * dil),
                q_col=0, k_col=0, v_col=0, n_col_groups=1, kv_width=HW4, grp=1, want_lse=True)
        out_d = _dil_combine(*d_res, B, S, tuple(d for _, d in DIL_PAIRS[1:]))

        merged = _merge((out_a, out_b, out_c, out_d), w_branch[l].astype(BF16), gates)
        xf = _matmul_residual(merged, w_out[l].astype(BF16), xf, mod, 2, S)

        hb, hp, top_idx, top_w = _router(xf, ffn_norm_g[l], mod, 3, 4, router_w[l], router_bias[l], S)
        block_expert, block_off, tok_pad, slot_of, n_used = _dispatch_plan(top_idx)
        y = _expert_ffn(hp, expert_w_gate, expert_w_up, expert_w_down, l, block_expert, block_off,
                        tok_pad, n_used)
        act = _matmul_glu(hb, shared_w_gate[l].astype(BF16), shared_w_up[l].astype(BF16))
        shared = _matmul(act, shared_w_down[l].astype(BF16), tn=1024, name="shared_down")
        xf = _combine(y, slot_of, xf, shared, top_w.T, mod, 5, S)

    return _final_norm(xf, final_norm_g).reshape(B, S, D)
```

```python
import functools
import math

import numpy as np
import jax
import jax.numpy as jnp
from jax import lax
from jax.experimental import pallas as pl
from jax.experimental.pallas import tpu as pltpu

F32 = jnp.float32
BF16 = jnp.bfloat16

D_MODEL = 4096
DEPTH = 2
HEAD_DIM = 128
ROPE_THETA = 10000.0
NORM_EPS = 1e-6

MLA_HEADS = 8
MLA_Q_RANK = 768
MLA_KV_RANK = 256
MLA_NOPE_DIM = 128
MLA_ROPE_DIM = 64
MLA_V_DIM = 128

DIFF_HEADS = 8
DIFF_QK_DIM = 64
DIFF_V_DIM = 128
DIFF_NORM_EPS = 1e-5

SWA_HEADS = 8
SWA_KV_HEADS = 2
SWA_HALF_WINDOW = 128

DIL_PAIRS = ((128, 1), (512, 4), (2048, 16))
DIL_HEADS = 4

N_EXPERTS = 64
N_EXPERT_GROUPS = 8
TOP_GROUPS = 4
TOP_K = 8
EXPERT_FF = 256
SHARED_FF = 1024
ROUTED_SCALE = 2.5

A_COLS = MLA_Q_RANK + MLA_KV_RANK + MLA_ROPE_DIM
B_COLS = 3 * DIFF_HEADS * DIFF_V_DIM
C_COLS = (SWA_HEADS + 2 * SWA_KV_HEADS) * HEAD_DIM
D_GROUP_COLS = 3 * DIL_HEADS * HEAD_DIM
D_COLS = len(DIL_PAIRS) * D_GROUP_COLS
OFF_B = A_COLS
OFF_C = OFF_B + B_COLS
OFF_D = OFF_C + C_COLS
OFF_G = OFF_D + D_COLS

LANES = 128
HW4 = DIL_HEADS * HEAD_DIM

R_DIL0_Q = 0
R_DIL0_K = HW4
R_DIFF_Q = 2 * HW4
R_DIFF_K = R_DIFF_Q + 1024
R_SWA_Q = R_DIFF_K + 1024
R_SWA_K = R_SWA_Q + 1024
R_USED = R_SWA_K + 256
R_COLS = 4608
X_GROUP = 3 * HW4
X_COLS = 2 * X_GROUP
V_DIL0 = 0
V_DIFF = HW4
V_SWA = V_DIFF + 1024
V_USED = V_SWA + 256
V_COLS = 2048
A_PAD = 1152

EXPERT_BLOCK_ROWS = 256
COMBINE_TOKENS = 32
PACK_HALF = D_MODEL // 2
PACK_SLABS = PACK_HALF // LANES
VMEM_LIMIT = 56 * 1024 * 1024


def _cparams(sem, vmem=VMEM_LIMIT):
    return pltpu.CompilerParams(dimension_semantics=sem, vmem_limit_bytes=vmem)


def _nt_dot(a, b):
    return lax.dot_general(a, b, (((1,), (1,)), ((), ())), preferred_element_type=F32)


def _ada_kernel(ct_ref, w_ref, b_ref, o_ref, *, batch):
    ct = ct_ref[...]
    s = ct * jax.nn.sigmoid(ct)
    w = w_ref[0]
    rows = [jnp.sum(w * s[:, b:b + 1], axis=0, keepdims=True) for b in range(batch)]
    rows += [jnp.zeros_like(rows[0])] * (8 - batch)
    o_ref[0] = jnp.concatenate(rows, axis=0) + b_ref[0]


def _ada_mod(c, w_ada, b_ada):
    B, D = c.shape
    ct = jnp.zeros((D, 8), F32).at[:, :B].set(c.T)
    n6 = w_ada.shape[-1]
    tn = 512
    return pl.pallas_call(
        functools.partial(_ada_kernel, batch=B),
        grid=(DEPTH, n6 // tn),
        in_specs=[
            pl.BlockSpec((D, 8), lambda l, j: (0, 0)),
            pl.BlockSpec((1, D, tn), lambda l, j: (l, 0, j)),
            pl.BlockSpec((1, 1, tn), lambda l, j: (l, 0, j)),
        ],
        out_specs=pl.BlockSpec((1, 8, tn), lambda l, j: (l, 0, j)),
        out_shape=jax.ShapeDtypeStruct((DEPTH, 8, n6), F32),
        compiler_params=_cparams(("arbitrary", "arbitrary")),
        name="ada_mod",
    )(ct, w_ada, b_ada.reshape(DEPTH, 1, n6))


def _norm_mod_kernel(x_ref, g_ref, mod_ref, o_ref, *, shift_row, scale_row):
    x = x_ref[...]
    ms = jnp.mean(x * x, axis=-1, keepdims=True)
    y = x * lax.rsqrt(ms + NORM_EPS) * g_ref[...]
    h = y * (1.0 + mod_ref[0, scale_row:scale_row + 1, :]) + mod_ref[0, shift_row:shift_row + 1, :]
    o_ref[...] = h.astype(o_ref.dtype)


def _norm_mod(x, g, mod, shift_row, scale_row, seq, tm=256):
    N, D = x.shape
    tm = min(tm, seq)
    per_b = seq // tm
    return pl.pallas_call(
        functools.partial(_norm_mod_kernel, shift_row=shift_row, scale_row=scale_row),
        grid=(N // tm,),
        in_specs=[
            pl.BlockSpec((tm, D), lambda i: (i, 0)),
            pl.BlockSpec((1, D), lambda i: (0, 0)),
            pl.BlockSpec((1, 6, D), lambda i: (i // per_b, 0, 0)),
        ],
        out_specs=pl.BlockSpec((tm, D), lambda i: (i, 0)),
        out_shape=jax.ShapeDtypeStruct((N, D), BF16),
        compiler_params=_cparams(("arbitrary",)),
        name="norm_mod",
    )(x, g.reshape(1, D), mod)


def _final_norm_kernel(x_ref, g_ref, o_ref):
    x = x_ref[...]
    ms = jnp.mean(x * x, axis=-1, keepdims=True)
    o_ref[...] = x * lax.rsqrt(ms + NORM_EPS) * g_ref[...]


def _final_norm(x, g, tm=256):
    N, D = x.shape
    tm = min(tm, N)
    return pl.pallas_call(
        _final_norm_kernel,
        grid=(N // tm,),
        in_specs=[pl.BlockSpec((tm, D), lambda i: (i, 0)), pl.BlockSpec((1, D), lambda i: (0, 0))],
        out_specs=pl.BlockSpec((tm, D), lambda i: (i, 0)),
        out_shape=jax.ShapeDtypeStruct((N, D), F32),
        compiler_params=_cparams(("arbitrary",)),
        name="final_norm",
    )(x, g.reshape(1, D))


def _mm_kernel(a_ref, b_ref, o_ref):
    o_ref[...] = jnp.dot(a_ref[...], b_ref[...], preferred_element_type=F32).astype(o_ref.dtype)


def _matmul(a, b, tn, out_dtype=BF16, tm=1024, name="matmul"):
    M, K = a.shape
    _, Nc = b.shape
    tm = min(tm, M)
    return pl.pallas_call(
        _mm_kernel,
        grid=(Nc // tn, M // tm),
        in_specs=[pl.BlockSpec((tm, K), lambda j, i: (i, 0)), pl.BlockSpec((K, tn), lambda j, i: (0, j))],
        out_specs=pl.BlockSpec((tm, tn), lambda j, i: (i, j)),
        out_shape=jax.ShapeDtypeStruct((M, Nc), out_dtype),
        compiler_params=_cparams(("arbitrary", "arbitrary")),
        name=name,
    )(a, b)


def _mm_glu_kernel(a_ref, bg_ref, bu_ref, o_ref):
    a = a_ref[...]
    g = jnp.dot(a, bg_ref[...], preferred_element_type=F32)
    u = jnp.dot(a, bu_ref[...], preferred_element_type=F32)
    o_ref[...] = (g * jax.nn.sigmoid(g) * u).astype(o_ref.dtype)


def _matmul_glu(a, bg, bu, tn=512, tm=1024):
    M, K = a.shape
    _, Nc = bg.shape
    tm = min(tm, M)
    return pl.pallas_call(
        _mm_glu_kernel,
        grid=(Nc // tn, M // tm),
        in_specs=[
            pl.BlockSpec((tm, K), lambda j, i: (i, 0)),
            pl.BlockSpec((K, tn), lambda j, i: (0, j)),
            pl.BlockSpec((K, tn), lambda j, i: (0, j)),
        ],
        out_specs=pl.BlockSpec((tm, tn), lambda j, i: (i, j)),
        out_shape=jax.ShapeDtypeStruct((M, Nc), BF16),
        compiler_params=_cparams(("arbitrary", "arbitrary")),
        name="shared_glu",
    )(a, bg, bu)


def _mm_residual_kernel(a_ref, b_ref, x_ref, mod_ref, o_ref, *, gate_row):
    acc = jnp.dot(a_ref[...], b_ref[...], preferred_element_type=F32)
    o_ref[...] = x_ref[...] + mod_ref[0, gate_row:gate_row + 1, :] * acc


def _matmul_residual(a, b, x, mod, gate_row, seq, tn=512, tm=1024):
    M, K = a.shape
    _, Nc = b.shape
    tm = min(tm, seq)
    per_b = seq // tm
    return pl.pallas_call(
        functools.partial(_mm_residual_kernel, gate_row=gate_row),
        grid=(Nc // tn, M // tm),
        in_specs=[
            pl.BlockSpec((tm, K), lambda j, i: (i, 0)),
            pl.BlockSpec((K, tn), lambda j, i: (0, j)),
            pl.BlockSpec((tm, tn), lambda j, i: (i, j)),
            pl.BlockSpec((1, 6, tn), lambda j, i: (i // per_b, 0, j)),
        ],
        out_specs=pl.BlockSpec((tm, tn), lambda j, i: (i, j)),
        out_shape=jax.ShapeDtypeStruct((M, Nc), F32),
        compiler_params=_cparams(("arbitrary", "arbitrary")),
        name="out_proj_residual",
    )(a, b, x, mod)


def _rope_table_kernel(pos_ref, inv128_ref, inv64_ref, c128_ref, s128_ref, c64_ref, s64_ref):
    pos = pos_ref[...]
    lane = lax.broadcasted_iota(jnp.int32, pos.shape, 1)
    a = pos * inv128_ref[...]
    c128_ref[...] = jnp.cos(a)
    s128_ref[...] = jnp.where(lane < 64, -jnp.sin(a), jnp.sin(a))
    a = pos * inv64_ref[...]
    c64_ref[...] = jnp.cos(a)
    s64_ref[...] = jnp.where((lane & 63) < 32, -jnp.sin(a), jnp.sin(a))


def _rope_tables(positions, tm=512):
    N = positions.size
    tm = min(tm, N)
    pos = jnp.broadcast_to(positions.astype(F32).reshape(N, 1), (N, LANES))

    def inv(half):
        f = np.float32(ROPE_THETA) ** (-(np.arange(half, dtype=np.float32) / np.float32(half)))
        return jnp.asarray(np.tile(f.astype(np.float32), LANES // half).reshape(1, LANES))

    tab = jax.ShapeDtypeStruct((N, LANES), F32)
    row = pl.BlockSpec((tm, LANES), lambda i: (i, 0))
    one = pl.BlockSpec((1, LANES), lambda i: (0, 0))
    return pl.pallas_call(
        _rope_table_kernel,
        grid=(N // tm,),
        in_specs=[row, one, one],
        out_specs=[row, row, row, row],
        out_shape=[tab, tab, tab, tab],
        compiler_params=_cparams(("arbitrary",)),
        name="rope_tables",
    )(pos, inv(64), inv(32))


def _rot128(x):
    return pltpu.roll(x, 64, 1)


def _rot64(x):
    lane = lax.broadcasted_iota(jnp.int32, x.shape, 1)
    return jnp.where((lane & 63) < 32, pltpu.roll(x, 96, 1), pltpu.roll(x, 32, 1))


def _r_block_plan():
    plan = [(128, HEAD_DIM ** -0.5)] * 4 + [(128, 1.0)] * 4
    plan += [(64, DIFF_QK_DIM ** -0.5)] * 8 + [(64, 1.0)] * 8
    plan += [(128, HEAD_DIM ** -0.5)] * 8 + [(128, 1.0)] * 2
    plan += [None] * ((R_COLS - R_USED) // LANES)
    return plan


def _rope_kernel(p_ref, c128_ref, s128_ref, c64_ref, s64_ref, o_ref):
    c128, s128, c64, s64 = c128_ref[...], s128_ref[...], c64_ref[...], s64_ref[...]
    for blk, spec in enumerate(_r_block_plan()):
        cols = slice(blk * LANES, (blk + 1) * LANES)
        if spec is None:
            o_ref[:, cols] = jnp.zeros((o_ref.shape[0], LANES), o_ref.dtype)
            continue
        flavour, scale = spec
        x = p_ref[:, cols].astype(F32)
        if flavour == 128:
            y = x * c128 + _rot128(x) * s128
        else:
            y = x * c64 + _rot64(x) * s64
        if scale != 1.0:
            y = y * scale
        o_ref[:, cols] = y.astype(o_ref.dtype)


def _rope_section(proj_r, tabs, tm=256):
    N = proj_r.shape[0]
    tm = min(tm, N)
    row = pl.BlockSpec((tm, R_COLS), lambda i: (i, 0))
    tab = pl.BlockSpec((tm, LANES), lambda i: (i, 0))
    return pl.pallas_call(
        _rope_kernel,
        grid=(N // tm,),
        in_specs=[row, tab, tab, tab, tab],
        out_specs=row,
        out_shape=jax.ShapeDtypeStruct((N, R_COLS), BF16),
        compiler_params=_cparams(("arbitrary",)),
        name="rope_section",
    )(proj_r, *tabs)


def _dil_rope_kernel(x_ref, c128_ref, s128_ref, *refs, dils, tm):
    scr = refs[-1]
    outs = refs[:-1]
    c128, s128 = c128_ref[...], s128_ref[...]
    for g, d in enumerate(dils):
        for part in range(3):
            o_ref = outs[g * 3 + part]
            base = (g * 3 + part) * DIL_HEADS
            for c in range(DIL_HEADS):
                col = g * X_GROUP + part * HW4 + c * LANES
                x = x_ref[:, col:col + LANES].astype(F32)
                if part < 2:
                    x = x * c128 + _rot128(x) * s128
                if part == 0:
                    x = x * (HEAD_DIM ** -0.5)
                scr[base + c] = x
            for r in range(d):
                for c in range(DIL_HEADS):
                    o_ref[r, :, c * LANES:(c + 1) * LANES] = (
                        scr[base + c, pl.ds(r, tm // d, stride=d), :].astype(o_ref.dtype))


def _dil_rope_section(proj_x, c128, s128, batch, seq, dils, tm=256):
    N = proj_x.shape[0]
    tm = min(tm, seq)
    per_b = seq // tm
    tab = pl.BlockSpec((tm, LANES), lambda i: (i, 0))
    out_specs, out_shape = [], []
    for d in dils:
        for _ in range(3):
            out_specs.append(pl.BlockSpec((d, tm // d, HW4), lambda i: (i // per_b, i % per_b, 0)))
            out_shape.append(jax.ShapeDtypeStruct((batch * d, seq // d, HW4), BF16))
    outs = pl.pallas_call(
        functools.partial(_dil_rope_kernel, dils=dils, tm=tm),
        grid=(N // tm,),
        in_specs=[pl.BlockSpec((tm, X_COLS), lambda i: (i, 0)), tab, tab],
        out_specs=out_specs,
        out_shape=out_shape,
        scratch_shapes=[pltpu.VMEM((len(dils) * 3 * DIL_HEADS, tm, LANES), F32)],
        compiler_params=_cparams(("arbitrary",)),
        name="dil_rope_section",
    )(proj_x, c128, s128)
    return [outs[3 * g:3 * g + 3] for g in range(len(dils))]


MLA_QW = 256


def _mla_prep_kernel(a_ref, gq_ref, gkv_ref, wq_ref, wkv_ref, c64_ref, s64_ref,
                     q_ref, kn_ref, kr_ref, v_ref):
    a = a_ref[...].astype(F32)
    c64, s64 = c64_ref[...], s64_ref[...]
    scale = (MLA_NOPE_DIM + MLA_ROPE_DIM) ** -0.5

    cq = a[:, :MLA_Q_RANK]
    cq = cq * lax.rsqrt(jnp.mean(cq * cq, axis=-1, keepdims=True) + NORM_EPS) * gq_ref[...]
    q = jnp.dot(cq.astype(BF16), wq_ref[...], preferred_element_type=F32)
    for h in range(MLA_HEADS):
        qn = q[:, h * MLA_QW:h * MLA_QW + LANES]
        qr = q[:, h * MLA_QW + LANES:(h + 1) * MLA_QW]
        qr = qr * c64 + _rot64(qr) * s64
        q_ref[:, h * MLA_QW:h * MLA_QW + LANES] = (qn * scale).astype(q_ref.dtype)
        q_ref[:, h * MLA_QW + LANES:(h + 1) * MLA_QW] = (qr * scale).astype(q_ref.dtype)

    ckv = a[:, MLA_Q_RANK:MLA_Q_RANK + MLA_KV_RANK]
    ckv = ckv * lax.rsqrt(jnp.mean(ckv * ckv, axis=-1, keepdims=True) + NORM_EPS) * gkv_ref[...]
    kv = jnp.dot(ckv.astype(BF16), wkv_ref[...], preferred_element_type=F32)
    kn_ref[...] = kv[:, :MLA_HEADS * MLA_NOPE_DIM].astype(kn_ref.dtype)
    v_ref[...] = kv[:, MLA_HEADS * MLA_NOPE_DIM:].astype(v_ref.dtype)

    kr = a[:, MLA_Q_RANK + MLA_KV_RANK:]
    kr_ref[...] = (kr * c64 + _rot64(kr) * s64).astype(kr_ref.dtype)


def _mla_prep(proj_a, gq, gkv, wq, wkv, c64, s64, tm=256):
    N = proj_a.shape[0]
    tm = min(tm, N)
    row = lambda w: pl.BlockSpec((tm, w), lambda i: (i, 0))
    full = lambda r, c: pl.BlockSpec((r, c), lambda i: (0, 0))
    hq = MLA_HEADS * MLA_QW
    hk = MLA_HEADS * MLA_NOPE_DIM
    return pl.pallas_call(
        _mla_prep_kernel,
        grid=(N // tm,),
        in_specs=[row(A_PAD), full(1, MLA_Q_RANK), full(1, MLA_KV_RANK), full(MLA_Q_RANK, hq),
                  full(MLA_KV_RANK, 2 * hk), row(LANES), row(LANES)],
        out_specs=[row(hq), row(hk), row(LANES), row(hk)],
        out_shape=[jax.ShapeDtypeStruct((N, hq), BF16), jax.ShapeDtypeStruct((N, hk), BF16),
                   jax.ShapeDtypeStruct((N, LANES), BF16), jax.ShapeDtypeStruct((N, hk), BF16)],
        compiler_params=_cparams(("arbitrary",)),
        name="mla_prep",
    )(proj_a, gq.reshape(1, -1), gkv.reshape(1, -1), wq, wkv, c64, s64)


def _softmax_pv(q, k_ref, v_ref, kc):
    tq = q.shape[0]
    S = k_ref.shape[0]
    m = jnp.full((tq, 1), -jnp.inf, F32)
    l = jnp.zeros((tq, 1), F32)
    acc = jnp.zeros((tq, v_ref.shape[1]), F32)
    for c in range(S // kc):
        s = _nt_dot(q, k_ref[c * kc:(c + 1) * kc, :])
        m_new = jnp.maximum(m, jnp.max(s, axis=-1, keepdims=True))
        alpha = jnp.exp(m - m_new)
        p = jnp.exp(s - m_new)
        l = alpha * l + jnp.sum(p, axis=-1, keepdims=True)
        acc = alpha * acc + jnp.dot(p.astype(BF16), v_ref[c * kc:(c + 1) * kc, :],
                                    preferred_element_type=F32)
        m = m_new
    return acc / l


def _mla_attn_kernel(q_ref, kn_ref, kr_ref, v_ref, o_ref, kcat_ref, *, kc):
    @pl.when(pl.program_id(2) == 0)
    def _():
        kcat_ref[:, :LANES] = kn_ref[...]
        kcat_ref[:, LANES:] = kr_ref[...]

    o_ref[...] = _softmax_pv(q_ref[...], kcat_ref, v_ref, kc).astype(o_ref.dtype)


def _mla_attention(qm, kn, kr, vm, batch, seq, tq=512, kc=1024):
    N = qm.shape[0]
    tq = min(tq, seq)
    kc = min(kc, seq)
    nq = seq // tq
    return pl.pallas_call(
        functools.partial(_mla_attn_kernel, kc=kc),
        grid=(batch, MLA_HEADS, nq),
        in_specs=[
            pl.BlockSpec((tq, MLA_QW), lambda b, h, i: (b * nq + i, h)),
            pl.BlockSpec((seq, LANES), lambda b, h, i: (b, h)),
            pl.BlockSpec((seq, LANES), lambda b, h, i: (b, 0)),
            pl.BlockSpec((seq, LANES), lambda b, h, i: (b, h)),
        ],
        out_specs=pl.BlockSpec((tq, LANES), lambda b, h, i: (b * nq + i, h)),
        out_shape=jax.ShapeDtypeStruct((N, MLA_HEADS * MLA_V_DIM), BF16),
        scratch_shapes=[pltpu.VMEM((seq, MLA_QW), BF16)],
        compiler_params=_cparams(("arbitrary", "arbitrary", "arbitrary")),
        name="mla_attention",
    )(qm, kn, kr, vm)


def _diff_attn_kernel(q_ref, k_ref, v_ref, lam_ref, g_ref, o_ref, *, kc, lam_init):
    q = q_ref[...]
    lane = lax.broadcasted_iota(jnp.int32, q.shape, 1)
    zero = jnp.zeros_like(q)
    o0 = _softmax_pv(jnp.where(lane < DIFF_QK_DIM, q, zero), k_ref, v_ref, kc)
    o1 = _softmax_pv(jnp.where(lane >= DIFF_QK_DIM, q, zero), k_ref, v_ref, kc)
    lp = lam_ref[...]
    lam = (jnp.exp(jnp.sum(lp[0:1] * lp[1:2], axis=-1, keepdims=True))
           - jnp.exp(jnp.sum(lp[2:3] * lp[3:4], axis=-1, keepdims=True)) + lam_init)
    o = o0 - lam * o1
    o = o * lax.rsqrt(jnp.mean(o * o, axis=-1, keepdims=True) + DIFF_NORM_EPS) * g_ref[...]
    o_ref[...] = (o * (1.0 - lam_init)).astype(o_ref.dtype)


def _diff_attention(rop, proj_v, lam_params, subln_g, lam_init, batch, seq, tq=512, kc=1024):
    N = rop.shape[0]
    tq = min(tq, seq)
    kc = min(kc, seq)
    nq = seq // tq
    qb, kb, vb = R_DIFF_Q // LANES, R_DIFF_K // LANES, V_DIFF // LANES
    return pl.pallas_call(
        functools.partial(_diff_attn_kernel, kc=kc, lam_init=lam_init),
        grid=(batch, DIFF_HEADS, nq),
        in_specs=[
            pl.BlockSpec((tq, LANES), lambda b, h, i: (b * nq + i, qb + h)),
            pl.BlockSpec((seq, LANES), lambda b, h, i: (b, kb + h)),
            pl.BlockSpec((seq, LANES), lambda b, h, i: (b, vb + h)),
            pl.BlockSpec((4, DIFF_QK_DIM), lambda b, h, i: (0, 0)),
            pl.BlockSpec((1, DIFF_V_DIM), lambda b, h, i: (0, 0)),
        ],
        out_specs=pl.BlockSpec((tq, LANES), lambda b, h, i: (b * nq + i, h)),
        out_shape=jax.ShapeDtypeStruct((N, DIFF_HEADS * DIFF_V_DIM), BF16),
        compiler_params=_cparams(("arbitrary", "arbitrary", "arbitrary")),
        name="diff_attention",
    )(rop, rop, proj_v, lam_params, subln_g.reshape(1, -1))


def _banded_kernel(*refs, half_window, blk, blocks_per_seq, grp, use_sink, want_lse):
    q_ref, kp_ref, kc_ref, kn_ref, vp_ref, vc_ref, vn_ref = refs[:7]
    rest = list(refs[7:])
    sink_ref = rest.pop(0) if use_sink else None
    o_ref = rest.pop(0)
    lse_ref = rest.pop(0) if want_lse else None

    il = lax.rem(pl.program_id(0), blocks_per_seq)
    qpos = il * blk + lax.broadcasted_iota(jnp.int32, (blk, 3 * blk), 0)
    kpos = (il - 1) * blk + lax.broadcasted_iota(jnp.int32, (blk, 3 * blk), 1)
    valid = ((jnp.abs(qpos - kpos) <= half_window) & (kpos >= 0) & (kpos < blocks_per_seq * blk))

    for h in range(DIL_HEADS):
        hk = h // grp
        cq = slice(h * LANES, (h + 1) * LANES)
        ck = slice(hk * LANES, (hk + 1) * LANES)
        k = jnp.concatenate([kp_ref[:, ck], kc_ref[:, ck], kn_ref[:, ck]], axis=0)
        v = jnp.concatenate([vp_ref[:, ck], vc_ref[:, ck], vn_ref[:, ck]], axis=0)
        s = jnp.where(valid, _nt_dot(q_ref[:, cq], k), -jnp.inf)
        m = jnp.max(s, axis=-1, keepdims=True)
        if use_sink:
            sk = sink_ref[0:1, h * LANES:h * LANES + 1]
            m = jnp.maximum(m, sk)
        p = jnp.exp(s - m)
        den = jnp.sum(p, axis=-1, keepdims=True)
        if use_sink:
            den = den + jnp.exp(sk - m)
        o = jnp.dot(p.astype(BF16), v, preferred_element_type=F32) / den
        o_ref[:, cq] = o.astype(o_ref.dtype)
        if want_lse:
            lse_ref[:, cq] = jnp.broadcast_to(m + jnp.log(den), (blk, LANES))


def _banded_attention(q_arr, k_arr, v_arr, *, sub_len, half_window, q_col, k_col, v_col,
                      n_col_groups, kv_width, grp, sink=None, want_lse=False):
    rows = q_arr.shape[0]
    blk = min(128, sub_len)
    blocks_per_seq = sub_len // blk
    n_row_blocks = rows // blk
    last = n_row_blocks - 1

    def prev(i):
        return jnp.maximum(i - 1, 0)

    def nxt(i):
        return jnp.minimum(i + 1, last)

    qspec = pl.BlockSpec((blk, HW4), lambda i, c: (i, q_col + c))
    kspecs = [pl.BlockSpec((blk, kv_width), lambda i, c, f=f: (f(i), k_col + c))
              for f in (prev, lambda i: i, nxt)]
    vspecs = [pl.BlockSpec((blk, kv_width), lambda i, c, f=f: (f(i), v_col + c))
              for f in (prev, lambda i: i, nxt)]
    in_specs = [qspec] + kspecs + vspecs
    args = [q_arr, k_arr, k_arr, k_arr, v_arr, v_arr, v_arr]
    if sink is not None:
        in_specs.append(pl.BlockSpec((1, HW4), lambda i, c: (0, c)))
        args.append(sink)
    out_cols = n_col_groups * HW4
    ospec = pl.BlockSpec((blk, HW4), lambda i, c: (i, c))
    out_specs = [ospec]
    out_shape = [jax.ShapeDtypeStruct((rows, out_cols), BF16)]
    if want_lse:
        out_specs.append(ospec)
        out_shape.append(jax.ShapeDtypeStruct((rows, out_cols), F32))
    return pl.pallas_call(
        functools.partial(_banded_kernel, half_window=half_window, blk=blk,
                          blocks_per_seq=blocks_per_seq, grp=grp, use_sink=sink is not None,
                          want_lse=want_lse),
        grid=(n_row_blocks, n_col_groups),
        in_specs=in_specs,
        out_specs=out_specs,
        out_shape=out_shape,
        compiler_params=_cparams(("arbitrary", "arbitrary")),
        name="banded_attention",
    )(*args)


def _dil_combine_kernel(o0, l0, o1, l1, o2, l2, out_ref, scr, *, dils, tm):
    def natural(o_ref, l_ref, d, base):
        for r in range(d):
            for c in range(DIL_HEADS):
                cols = slice(c * LANES, (c + 1) * LANES)
                scr[base + c, pl.ds(r, tm // d, stride=d), :] = o_ref[r, :, cols].astype(F32)
                scr[base + DIL_HEADS + c, pl.ds(r, tm // d, stride=d), :] = l_ref[r, :, cols]

    natural(o1, l1, dils[0], 0)
    natural(o2, l2, dils[1], 2 * DIL_HEADS)
    for c in range(DIL_HEADS):
        cols = slice(c * LANES, (c + 1) * LANES)
        oa, la = o0[:, cols].astype(F32), l0[:, cols]
        ob, lb = scr[c], scr[DIL_HEADS + c]
        oc, lc = scr[2 * DIL_HEADS + c], scr[3 * DIL_HEADS + c]
        m = jnp.maximum(jnp.maximum(la, lb), lc)
        ea, eb, ec = jnp.exp(la - m), jnp.exp(lb - m), jnp.exp(lc - m)
        inv = 1.0 / (ea + eb + ec)
        out_ref[:, cols] = ((ea * inv) * oa + (eb * inv) * ob + (ec * inv) * oc).astype(out_ref.dtype)


def _dil_combine(o0, l0, o1, l1, o2, l2, batch, seq, dils, tm=256):
    N = o0.shape[0]
    tm = min(tm, seq)
    per_b = seq // tm
    row = pl.BlockSpec((tm, HW4), lambda i: (i, 0))
    grouped = [pl.BlockSpec((d, tm // d, HW4), lambda i: (i // per_b, i % per_b, 0)) for d in dils]
    shaped = lambda a, d: a.reshape(batch * d, seq // d, HW4)
    return pl.pallas_call(
        functools.partial(_dil_combine_kernel, dils=dils, tm=tm),
        grid=(N // tm,),
        in_specs=[row, row, grouped[0], grouped[0], grouped[1], grouped[1]],
        out_specs=row,
        out_shape=jax.ShapeDtypeStruct((N, HW4), BF16),
        scratch_shapes=[pltpu.VMEM((4 * DIL_HEADS, tm, LANES), F32)],
        compiler_params=_cparams(("arbitrary",)),
        name="dil_combine",
    )(o0, l0, shaped(o1, dils[0]), shaped(l1, dils[0]), shaped(o2, dils[1]), shaped(l2, dils[1]))


def _merge_kernel(oa, ob, oc, od, wa, wb, wc, wd, ga, gb, gc, gd, o_ref):
    acc = None
    for o, w, g in ((oa, wa, ga), (ob, wb, gb), (oc, wc, gc), (od, wd, gd)):
        t = jax.nn.sigmoid(g[...].astype(F32)) * jnp.dot(o[...], w[...], preferred_element_type=F32)
        acc = t if acc is None else acc + t
    o_ref[...] = acc.astype(o_ref.dtype)


def _merge(branch_outs, w_branch, gates, tn=512, tm=1024):
    N = gates.shape[0]
    D = D_MODEL
    tm = min(tm, N)
    oa, ob, oc, od = branch_outs
    in_specs = [pl.BlockSpec((tm, o.shape[1]), lambda j, i: (i, 0)) for o in branch_outs]
    in_specs += [pl.BlockSpec((1024, tn), lambda j, i, r=r: (r, j)) for r in range(3)]
    in_specs += [pl.BlockSpec((512, tn), lambda j, i: (6, j))]
    nj = D // tn
    in_specs += [pl.BlockSpec((tm, tn), lambda j, i, r=r: (i, r * nj + j)) for r in range(4)]
    return pl.pallas_call(
        _merge_kernel,
        grid=(nj, N // tm),
        in_specs=in_specs,
        out_specs=pl.BlockSpec((tm, tn), lambda j, i: (i, j)),
        out_shape=jax.ShapeDtypeStruct((N, D), BF16),
        compiler_params=_cparams(("arbitrary", "arbitrary")),
        name="branch_merge",
    )(oa, ob, oc, od, w_branch, w_branch, w_branch, w_branch, gates, gates, gates, gates)


VMEM_PITCH = 20


def _pack_pairs(lo, hi):
    lo_b = lax.bitcast_convert_type(lo.astype(BF16).astype(F32), jnp.uint32)
    hi_b = lax.bitcast_convert_type(hi.astype(BF16).astype(F32), jnp.uint32)
    return (lo_b >> 16) | (hi_b & jnp.uint32(0xFFFF0000))


def _unpack_pairs(w):
    lo = lax.bitcast_convert_type(w << 16, F32)
    hi = lax.bitcast_convert_type(w & jnp.uint32(0xFFFF0000), F32)
    return lo, hi


def _store_packed(o_ref, x):
    rows = x.shape[0]
    for j in range(PACK_SLABS):
        o_ref[pl.ds(j, rows, stride=PACK_SLABS), :] = _pack_pairs(
            x[:, j * LANES:(j + 1) * LANES], x[:, PACK_HALF + j * LANES:PACK_HALF + (j + 1) * LANES])


def _load_slab(buf_ref, first_row, rows, j):
    return buf_ref[pl.ds(first_row + j, rows, stride=VMEM_PITCH), :]


def _router_kernel(x_ref, g_ref, mod_ref, rw_ref, rb_ref, hb_ref, hp_ref, idx_ref, wt_ref, *,
                   shift_row, scale_row):
    x = x_ref[...]
    ms = jnp.mean(x * x, axis=-1, keepdims=True)
    y = x * lax.rsqrt(ms + NORM_EPS) * g_ref[...]
    h = y * (1.0 + mod_ref[0, scale_row:scale_row + 1, :]) + mod_ref[0, shift_row:shift_row + 1, :]
    hb_ref[...] = h.astype(hb_ref.dtype)
    _store_packed(hp_ref, h)

    tm = x.shape[0]
    logits = jnp.dot(h, rw_ref[...], preferred_element_type=F32, precision=lax.Precision.HIGHEST)
    lt = logits.T[:N_EXPERTS, :]
    scores = jax.nn.sigmoid(lt)
    choice = scores + rb_ref[...]

    per_group = N_EXPERTS // N_EXPERT_GROUPS
    sub = lax.broadcasted_iota(jnp.int32, (per_group, tm), 0)
    group_score = []
    for g in range(N_EXPERT_GROUPS):
        cg = choice[g * per_group:(g + 1) * per_group, :]
        m1 = jnp.max(cg, axis=0, keepdims=True)
        first = jnp.min(jnp.where(cg == m1, sub, per_group), axis=0, keepdims=True)
        m2 = jnp.max(jnp.where(sub == first, -jnp.inf, cg), axis=0, keepdims=True)
        group_score.append(m1 + m2)
    masked = []
    for g in range(N_EXPERT_GROUPS):
        ahead = jnp.zeros((1, tm), jnp.int32)
        for o in range(N_EXPERT_GROUPS):
            if o == g:
                continue
            better = (group_score[o] >= group_score[g]) if o < g else (group_score[o] > group_score[g])
            ahead = ahead + better.astype(jnp.int32)
        cg = choice[g * per_group:(g + 1) * per_group, :]
        masked.append(jnp.where(ahead < TOP_GROUPS, cg, -jnp.inf))
    cm = jnp.concatenate(masked, axis=0)

    eidx = lax.broadcasted_iota(jnp.int32, (N_EXPERTS, tm), 0)
    rank = jnp.zeros((N_EXPERTS, tm), jnp.int32)
    for e in range(N_EXPERTS):
        row = cm[e:e + 1, :]
        tie = jnp.where(eidx > e, 1, 0)
        rank = rank + jnp.where(row > cm, 1, jnp.where(row == cm, tie, 0))
    sel = rank < TOP_K
    wsel = jnp.where(sel, scores, 0.0)
    wsel = wsel / jnp.sum(wsel, axis=0, keepdims=True) * ROUTED_SCALE
    idx_rows, wt_rows = [], []
    for r in range(TOP_K):
        hit = rank == r
        idx_rows.append(jnp.sum(jnp.where(hit, eidx, 0), axis=0, keepdims=True))
        wt_rows.append(jnp.sum(jnp.where(hit, wsel, 0.0), axis=0, keepdims=True))
    idx_ref[...] = jnp.concatenate(idx_rows, axis=0)
    wt_ref[...] = jnp.concatenate(wt_rows, axis=0)


def _router(x, g, mod, shift_row, scale_row, router_w, router_bias, seq, tm=256):
    N, D = x.shape
    tm = min(tm, seq)
    per_b = seq // tm
    rw = jnp.zeros((D, LANES), F32).at[:, :N_EXPERTS].set(router_w)
    row = pl.BlockSpec((tm, D), lambda i: (i, 0))
    col = lambda r: pl.BlockSpec((r, tm), lambda i: (0, i))
    return pl.pallas_call(
        functools.partial(_router_kernel, shift_row=shift_row, scale_row=scale_row),
        grid=(N // tm,),
        in_specs=[
            row,
            pl.BlockSpec((1, D), lambda i: (0, 0)),
            pl.BlockSpec((1, 6, D), lambda i: (i // per_b, 0, 0)),
            pl.BlockSpec((D, LANES), lambda i: (0, 0)),
            pl.BlockSpec((N_EXPERTS, 1), lambda i: (0, 0)),
        ],
        out_specs=[row, pl.BlockSpec((tm * PACK_SLABS, LANES), lambda i: (i, 0)), col(TOP_K), col(TOP_K)],
        out_shape=[jax.ShapeDtypeStruct((N, D), BF16), jax.ShapeDtypeStruct((N * PACK_SLABS, LANES), jnp.uint32),
                   jax.ShapeDtypeStruct((TOP_K, N), jnp.int32), jax.ShapeDtypeStruct((TOP_K, N), F32)],
        compiler_params=_cparams(("arbitrary",)),
        name="ffn_norm_router",
    )(x, g.reshape(1, D), mod, rw, router_bias.reshape(N_EXPERTS, 1))


def _num_expert_blocks(n_tokens):
    bm = EXPERT_BLOCK_ROWS
    return -(-(n_tokens * TOP_K + N_EXPERTS * (bm - 1)) // bm)


def _dispatch_plan(top_idx):
    K, N = top_idx.shape
    bm = EXPERT_BLOCK_ROWS
    n_blocks = _num_expert_blocks(N)
    experts = jnp.arange(N_EXPERTS, dtype=jnp.int32)
    onehot = top_idx[:, None, :] == experts[None, :, None]
    mask = jnp.any(onehot, axis=0).astype(jnp.int32)
    counts = jnp.sum(mask, axis=1)
    pos = jnp.cumsum(mask, axis=1) - mask
    nblk = (counts + bm - 1) // bm
    blk_end = jnp.cumsum(nblk)
    blk_start = blk_end - nblk
    n_used = blk_end[-1]
    start = jnp.cumsum(counts) - counts
    slot_en = (blk_start * bm)[:, None] + pos
    slot_of = jnp.sum(jnp.where(onehot, slot_en[None], 0), axis=1)

    keys = top_idx * N + jnp.arange(N, dtype=jnp.int32)[None, :]
    tok_sorted = jnp.sort(keys.reshape(-1)) % N

    bidx = jnp.arange(n_blocks + 1, dtype=jnp.int32)
    be = jnp.minimum(jnp.sum((bidx[:, None] >= blk_end[None, :]).astype(jnp.int32), axis=1),
                     N_EXPERTS - 1)
    be = jnp.where(bidx < n_used, be, be[jnp.maximum(n_used - 1, 0)])
    off = jnp.clip(start[be] + (bidx - blk_start[be]) * bm, 0, K * N)
    tok_pad = jnp.concatenate([tok_sorted, jnp.zeros((bm,), jnp.int32)])
    return (be.astype(jnp.int32), off.astype(jnp.int32), tok_pad.astype(jnp.int32),
            slot_of.astype(jnp.int32), n_used.astype(jnp.int32))


def _expert_kernel(be_ref, off_ref, tok_ref, nu_ref, h_hbm, wg_ref, wu_ref, wd_ref, y_ref,
                   xbuf0, xbuf1, xs, wg_s, wu_s, wd_s, sem):
    bm = EXPERT_BLOCK_ROWS
    b = pl.program_id(0)
    n_used = nu_ref[0]
    bufs = (xbuf0, xbuf1)

    def start_gather(blk, slot, unrolled):
        base = off_ref[blk]

        def one(r):
            src = pl.multiple_of(tok_ref[base + r] * PACK_SLABS, PACK_SLABS)
            pltpu.make_async_copy(h_hbm.at[pl.ds(src, PACK_SLABS), :],
                                  bufs[slot].at[pl.ds(r * VMEM_PITCH, PACK_SLABS), :], sem.at[slot]).start()

        if unrolled:
            for r in range(bm):
                one(r)
        else:
            def body(r, carry):
                one(r)
                return carry
            lax.fori_loop(0, bm, body, 0)

    def wait_gather(slot):
        pltpu.make_async_copy(h_hbm.at[pl.ds(0, bm * PACK_SLABS), :],
                              bufs[slot].at[pl.ds(0, bm * PACK_SLABS), :], sem.at[slot]).wait()

    @pl.when(b == 0)
    def _():
        start_gather(0, 0, False)

    new_expert = (b == 0) | (be_ref[b] != be_ref[jnp.maximum(b - 1, 0)])

    @pl.when(new_expert & (b < n_used))
    def _():
        wg_s[...] = wg_ref[0, 0].astype(BF16)
        wu_s[...] = wu_ref[0, 0].astype(BF16)
        wd_s[...] = wd_ref[0, 0].astype(BF16)

    def work(slot):
        wait_gather(slot)
        for j in range(PACK_SLABS):
            lo, hi = _unpack_pairs(_load_slab(bufs[slot], 0, bm, j))
            xs[:, j * LANES:(j + 1) * LANES] = lo.astype(BF16)
            xs[:, PACK_HALF + j * LANES:PACK_HALF + (j + 1) * LANES] = hi.astype(BF16)
        start_gather(b + 1, 1 - slot, True)
        x = xs[...]
        g = jnp.dot(x, wg_s[...], preferred_element_type=F32)
        u = jnp.dot(x, wu_s[...], preferred_element_type=F32)
        a = (g * jax.nn.sigmoid(g) * u).astype(BF16)
        _store_packed(y_ref, jnp.dot(a, wd_s[...], preferred_element_type=F32))

    for slot in (0, 1):
        @pl.when((b < n_used) & (b % 2 == slot))
        def _(slot=slot):
            work(slot)

        @pl.when((b == n_used) & (b % 2 == slot))
        def _(slot=slot):
            wait_gather(slot)

    @pl.when(b >= n_used)
    def _():
        y_ref[...] = jnp.zeros(y_ref.shape, y_ref.dtype)


def _expert_ffn(h_packed, w_gate, w_up, w_down, layer, block_expert, block_off, tok_pad, n_used):
    bm = EXPERT_BLOCK_ROWS
    D = D_MODEL
    n_steps = block_expert.shape[0]
    wspec = lambda r, c: pl.BlockSpec((1, 1, r, c), lambda b, be, off, tok, nu: (layer, be[b], 0, 0))
    grid_spec = pltpu.PrefetchScalarGridSpec(
        num_scalar_prefetch=4,
        grid=(n_steps,),
        in_specs=[pl.BlockSpec(memory_space=pl.ANY), wspec(D, EXPERT_FF), wspec(D, EXPERT_FF),
                  wspec(EXPERT_FF, D)],
        out_specs=pl.BlockSpec((bm * PACK_SLABS, LANES), lambda b, be, off, tok, nu: (b, 0)),
        scratch_shapes=[pltpu.VMEM((bm * VMEM_PITCH, LANES), jnp.uint32),
                        pltpu.VMEM((bm * VMEM_PITCH, LANES), jnp.uint32),
                        pltpu.VMEM((bm, D), BF16),
                        pltpu.VMEM((D, EXPERT_FF), BF16), pltpu.VMEM((D, EXPERT_FF), BF16),
                        pltpu.VMEM((EXPERT_FF, D), BF16),
                        pltpu.SemaphoreType.DMA((2,))],
    )
    return pl.pallas_call(
        _expert_kernel,
        grid_spec=grid_spec,
        out_shape=jax.ShapeDtypeStruct((n_steps * bm * PACK_SLABS, LANES), jnp.uint32),
        compiler_params=_cparams(("arbitrary",)),
        name="expert_ffn",
    )(block_expert, block_off, tok_pad, n_used.reshape(1), h_packed, w_gate, w_up, w_down)


def _combine_kernel(slot_ref, y_hbm, x_ref, sh_ref, w_ref, mod_ref, o_ref, ybuf0, ybuf1, sem, *,
                    gate_row, n_tokens):
    tc = COMBINE_TOKENS
    i = pl.program_id(0)
    n_steps = pl.num_programs(0)
    bufs = (ybuf0, ybuf1)

    def start_gather(step, slot, unrolled):
        base = step * tc

        def one(t, k):
            src = pl.multiple_of(slot_ref[k * n_tokens + base + t] * PACK_SLABS, PACK_SLABS)
            pltpu.make_async_copy(y_hbm.at[pl.ds(src, PACK_SLABS), :],
                                  bufs[slot].at[pl.ds((k * tc + t) * VMEM_PITCH, PACK_SLABS), :],
                                  sem.at[slot]).start()

        if unrolled:
            for t in range(tc):
                for k in range(TOP_K):
                    one(t, k)
        else:
            def body(t, carry):
                for k in range(TOP_K):
                    one(t, k)
                return carry
            lax.fori_loop(0, tc, body, 0)

    @pl.when(i == 0)
    def _():
        start_gather(0, 0, False)

    def work(slot, prefetch):
        n_rows = TOP_K * tc * PACK_SLABS
        pltpu.make_async_copy(y_hbm.at[pl.ds(0, n_rows), :], bufs[slot].at[pl.ds(0, n_rows), :],
                              sem.at[slot]).wait()
        if prefetch:
            start_gather(i + 1, 1 - slot, True)
        w = w_ref[...]
        wk = [w[:, k:k + 1] for k in range(TOP_K)]
        for j in range(PACK_SLABS):
            c_lo = slice(j * LANES, (j + 1) * LANES)
            c_hi = slice(PACK_HALF + j * LANES, PACK_HALF + (j + 1) * LANES)
            acc_lo = sh_ref[:, c_lo].astype(F32)
            acc_hi = sh_ref[:, c_hi].astype(F32)
            for k in range(TOP_K):
                lo, hi = _unpack_pairs(_load_slab(bufs[slot], k * tc * VMEM_PITCH, tc, j))
                acc_lo = acc_lo + wk[k] * lo
                acc_hi = acc_hi + wk[k] * hi
            o_ref[:, c_lo] = x_ref[:, c_lo] + mod_ref[0, gate_row:gate_row + 1, c_lo] * acc_lo
            o_ref[:, c_hi] = x_ref[:, c_hi] + mod_ref[0, gate_row:gate_row + 1, c_hi] * acc_hi

    for slot in (0, 1):
        for prefetch in (True, False):
            @pl.when((i % 2 == slot) & ((i + 1 < n_steps) == prefetch))
            def _(slot=slot, prefetch=prefetch):
                work(slot, prefetch)


def _combine(y_packed, slot_of, x, shared, w_tok, mod, gate_row, seq):
    N, D = x.shape
    tc = COMBINE_TOKENS
    per_b = seq // tc
    grid_spec = pltpu.PrefetchScalarGridSpec(
        num_scalar_prefetch=1,
        grid=(N // tc,),
        in_specs=[
            pl.BlockSpec(memory_space=pl.ANY),
            pl.BlockSpec((tc, D), lambda i, s: (i, 0)),
            pl.BlockSpec((tc, D), lambda i, s: (i, 0)),
            pl.BlockSpec((tc, TOP_K), lambda i, s: (i, 0)),
            pl.BlockSpec((1, 6, D), lambda i, s: (i // per_b, 0, 0)),
        ],
        out_specs=pl.BlockSpec((tc, D), lambda i, s: (i, 0)),
        scratch_shapes=[pltpu.VMEM((TOP_K * tc * VMEM_PITCH, LANES), jnp.uint32),
                        pltpu.VMEM((TOP_K * tc * VMEM_PITCH, LANES), jnp.uint32),
                        pltpu.SemaphoreType.DMA((2,))],
    )
    return pl.pallas_call(
        functools.partial(_combine_kernel, gate_row=gate_row, n_tokens=N),
        grid_spec=grid_spec,
        out_shape=jax.ShapeDtypeStruct((N, D), F32),
        compiler_params=_cparams(("arbitrary",)),
        name="expert_combine",
    )(slot_of.reshape(-1), y_packed, x, shared, w_tok, mod)


def _split_w_in(w):
    D = w.shape[0]
    z = lambda n: jnp.zeros((D, n), w.dtype)
    w_r = jnp.concatenate([w[:, OFF_D:OFF_D + 2 * HW4], w[:, OFF_B:OFF_B + 2048], w[:, OFF_C:OFF_C + 1280],
                           z(R_COLS - R_USED)], axis=1)
    w_x = w[:, OFF_D + D_GROUP_COLS:OFF_D + 3 * D_GROUP_COLS]
    w_v = jnp.concatenate([w[:, OFF_D + 2 * HW4:OFF_D + 3 * HW4], w[:, OFF_B + 2048:OFF_B + 3072],
                           w[:, OFF_C + 1280:OFF_C + 1536], z(V_COLS - V_USED)], axis=1)
    w_a = jnp.concatenate([w[:, :A_COLS], z(A_PAD - A_COLS)], axis=1)
    w_g = w[:, OFF_G:]
    return tuple(t.astype(BF16) for t in (w_r, w_x, w_v, w_a, w_g))


def _mla_weights(w_uq, w_ukv):
    qd = MLA_NOPE_DIM + MLA_ROPE_DIM
    wq = w_uq.reshape(MLA_Q_RANK, MLA_HEADS, qd)
    wq = jnp.concatenate([wq, jnp.zeros((MLA_Q_RANK, MLA_HEADS, MLA_QW - qd), wq.dtype)], axis=-1)
    wkv = w_ukv.reshape(MLA_KV_RANK, MLA_HEADS, MLA_NOPE_DIM + MLA_V_DIM)
    wkv = jnp.concatenate([wkv[:, :, :MLA_NOPE_DIM].reshape(MLA_KV_RANK, -1),
                           wkv[:, :, MLA_NOPE_DIM:].reshape(MLA_KV_RANK, -1)], axis=-1)
    return wq.reshape(MLA_Q_RANK, MLA_HEADS * MLA_QW).astype(BF16), wkv.astype(BF16)


def kernel(x, c, positions, w_ada, b_ada, mix_norm_g, ffn_norm_g, w_in, mla_q_norm_g, mla_w_uq,
           mla_kv_norm_g, mla_w_ukv, diff_lambda, diff_subln_g, swa_sink, w_branch, w_out,
           router_w, router_bias, expert_w_gate, expert_w_up, expert_w_down,
           shared_w_gate, shared_w_up, shared_w_down, final_norm_g):
    B, S, D = x.shape
    N = B * S
    xf = x.reshape(N, D)
    mod_all = _ada_mod(c, w_ada, b_ada)
    tabs = _rope_tables(positions)
    c64, s64 = tabs[2], tabs[3]

    for l in range(DEPTH):
        mod = mod_all[l, :B].reshape(B, 6, D)

        h = _norm_mod(xf, mix_norm_g[l], mod, 0, 1, S)
        w_r, w_x, w_v, w_a, w_g = _split_w_in(w_in[l])
        proj_r = _matmul(h, w_r, tn=512, name="proj_rotary")
        proj_x = _matmul(h, w_x, tn=1024, name="proj_dilated")
        proj_v = _matmul(h, w_v, tn=1024, name="proj_value")
        proj_a = _matmul(h, w_a, tn=A_PAD, name="proj_latent")
        gates = _matmul(h, w_g, tn=1024, name="proj_gates")
        rop = _rope_section(proj_r, tabs)
        dil_groups = _dil_rope_section(proj_x, tabs[0], tabs[1], B, S, tuple(d for _, d in DIL_PAIRS[1:]))

        wq, wkv = _mla_weights(mla_w_uq[l], mla_w_ukv[l])
        qm, kn, kr, vm = _mla_prep(proj_a, mla_q_norm_g[l], mla_kv_norm_g[l], wq, wkv, c64, s64)
        out_a = _mla_attention(qm, kn, kr, vm, B, S)

        lam_init = 0.8 - 0.6 * math.exp(-0.3 * l)
        out_b = _diff_attention(rop, proj_v, diff_lambda[l], diff_subln_g[l], lam_init, B, S)

        sink = jnp.repeat(swa_sink[l].astype(F32), LANES).reshape(1, SWA_HEADS * LANES)
        (out_c,) = _banded_attention(
            rop, rop, proj_v, sub_len=S, half_window=SWA_HALF_WINDOW,
            q_col=R_SWA_Q // HW4, k_col=R_SWA_K // LANES, v_col=V_SWA // LANES,
            n_col_groups=SWA_KV_HEADS, kv_width=LANES, grp=SWA_HEADS // SWA_KV_HEADS, sink=sink)

        window0, _ = DIL_PAIRS[0]
        d_res = list(_banded_attention(
            rop, rop, proj_v, sub_len=S, half_window=window0 // 2, q_col=R_DIL0_Q // HW4,
            k_col=R_DIL0_K // HW4, v_col=V_DIL0 // HW4, n_col_groups=1, kv_width=HW4, grp=1, want_lse=True))
        for (window, dil), (qg, kg, vg) in zip(DIL_PAIRS[1:], dil_groups):
            flat = lambda a: a.reshape(N, HW4)
            d_res += _banded_attention(
                flat(qg), flat(kg), flat(vg), sub_len=S // dil, half_window=window // (2 * dil),
                q_col=0, k_col=0, v_col=0, n_col_groups=1, kv_width=HW4, grp=1, want_lse=True)
        out_d = _dil_combine(*d_res, B, S, tuple(d for _, d in DIL_PAIRS[1:]))

        merged = _merge((out_a, out_b, out_c, out_d), w_branch[l].astype(BF16), gates)
        xf = _matmul_residual(merged, w_out[l].astype(BF16), xf, mod, 2, S)

        hb, hp, top_idx, top_w = _router(xf, ffn_norm_g[l], mod, 3, 4, router_w[l], router_bias[l], S)
        block_expert, block_off, tok_pad, slot_of, n_used = _dispatch_plan(top_idx)
        y = _expert_ffn(hp, expert_w_gate, expert_w_up, expert_w_down, l, block_expert, block_off,
                        tok_pad, n_used)
        act = _matmul_glu(hb, shared_w_gate[l].astype(BF16), shared_w_up[l].astype(BF16))
        shared = _matmul(act, shared_w_down[l].astype(BF16), tn=1024, name="shared_down")
        xf = _combine(y, slot_of, xf, shared, top_w.T, mod, 5, S)

    return _final_norm(xf, final_norm_g).reshape(B, S, D)
```

```python
import functools
import math

import numpy as np
import jax
import jax.numpy as jnp
from jax import lax
from jax.experimental import pallas as pl
from jax.experimental.pallas import tpu as pltpu

F32 = jnp.float32
BF16 = jnp.bfloat16

D_MODEL = 4096
DEPTH = 2
HEAD_DIM = 128
ROPE_THETA = 10000.0
NORM_EPS = 1e-6

MLA_HEADS = 8
MLA_Q_RANK = 768
MLA_KV_RANK = 256
MLA_NOPE_DIM = 128
MLA_ROPE_DIM = 64
MLA_V_DIM = 128

DIFF_HEADS = 8
DIFF_QK_DIM = 64
DIFF_V_DIM = 128
DIFF_NORM_EPS = 1e-5

SWA_HEADS = 8
SWA_KV_HEADS = 2
SWA_HALF_WINDOW = 128

DIL_PAIRS = ((128, 1), (512, 4), (2048, 16))
DIL_HEADS = 4

N_EXPERTS = 64
N_EXPERT_GROUPS = 8
TOP_GROUPS = 4
TOP_K = 8
EXPERT_FF = 256
SHARED_FF = 1024
ROUTED_SCALE = 2.5

A_COLS = MLA_Q_RANK + MLA_KV_RANK + MLA_ROPE_DIM
B_COLS = 3 * DIFF_HEADS * DIFF_V_DIM
C_COLS = (SWA_HEADS + 2 * SWA_KV_HEADS) * HEAD_DIM
D_GROUP_COLS = 3 * DIL_HEADS * HEAD_DIM
D_COLS = len(DIL_PAIRS) * D_GROUP_COLS
OFF_B = A_COLS
OFF_C = OFF_B + B_COLS
OFF_D = OFF_C + C_COLS
OFF_G = OFF_D + D_COLS

LANES = 128
HW4 = DIL_HEADS * HEAD_DIM

R_DIL0_Q = 0
R_DIL0_K = HW4
R_DIFF_Q = 2 * HW4
R_DIFF_K = R_DIFF_Q + 1024
R_SWA_Q = R_DIFF_K + 1024
R_SWA_K = R_SWA_Q + 1024
R_USED = R_SWA_K + 256
R_COLS = 4608
X_GROUP = 3 * HW4
X_COLS = 2 * X_GROUP
V_DIL0 = 0
V_DIFF = HW4
V_SWA = V_DIFF + 1024
V_USED = V_SWA + 256
V_COLS = 2048
A_PAD = 1152

EXPERT_BLOCK_ROWS = 256
COMBINE_TOKENS = 32
PACK_HALF = D_MODEL // 2
PACK_SLABS = PACK_HALF // LANES
VMEM_LIMIT = 56 * 1024 * 1024


def _cparams(sem, vmem=VMEM_LIMIT):
    return pltpu.CompilerParams(dimension_semantics=sem, vmem_limit_bytes=vmem)


def _nt_dot(a, b):
    return lax.dot_general(a, b, (((1,), (1,)), ((), ())), preferred_element_type=F32)


ADA_ROWS = 128


def _ada_kernel(ct_ref, w_ref, b_ref, o_ref, *, batch):
    k = pl.program_id(1)
    ct = ct_ref[...]
    s = ct * jax.nn.sigmoid(ct)
    w = w_ref[0]
    rows = [jnp.sum(w * s[:, b:b + 1], axis=0, keepdims=True) for b in range(batch)]
    rows += [jnp.zeros_like(rows[0])] * (8 - batch)
    part = jnp.concatenate(rows, axis=0)

    @pl.when(k == 0)
    def _():
        o_ref[0] = part + b_ref[0]

    @pl.when(k > 0)
    def _():
        o_ref[0] += part


def _ada_mod(c, w_ada, b_ada):
    B, D = c.shape
    ct = jnp.zeros((D, 8), F32).at[:, :B].set(c.T)
    n6 = w_ada.shape[-1]
    return pl.pallas_call(
        functools.partial(_ada_kernel, batch=B),
        grid=(DEPTH, D // ADA_ROWS),
        in_specs=[
            pl.BlockSpec((ADA_ROWS, 8), lambda l, k: (k, 0)),
            pl.BlockSpec((1, ADA_ROWS, n6), lambda l, k: (l, k, 0)),
            pl.BlockSpec((1, 1, n6), lambda l, k: (l, 0, 0)),
        ],
        out_specs=pl.BlockSpec((1, 8, n6), lambda l, k: (l, 0, 0)),
        out_shape=jax.ShapeDtypeStruct((DEPTH, 8, n6), F32),
        compiler_params=_cparams(("arbitrary", "arbitrary")),
        name="ada_mod",
    )(ct, w_ada, b_ada.reshape(DEPTH, 1, n6))


def _norm_mod_kernel(x_ref, g_ref, mod_ref, o_ref, *, shift_row, scale_row):
    x = x_ref[...]
    ms = jnp.mean(x * x, axis=-1, keepdims=True)
    y = x * lax.rsqrt(ms + NORM_EPS) * g_ref[...]
    h = y * (1.0 + mod_ref[0, scale_row:scale_row + 1, :]) + mod_ref[0, shift_row:shift_row + 1, :]
    o_ref[...] = h.astype(o_ref.dtype)


def _norm_mod(x, g, mod, shift_row, scale_row, seq, tm=256):
    N, D = x.shape
    tm = min(tm, seq)
    per_b = seq // tm
    return pl.pallas_call(
        functools.partial(_norm_mod_kernel, shift_row=shift_row, scale_row=scale_row),
        grid=(N // tm,),
        in_specs=[
            pl.BlockSpec((tm, D), lambda i: (i, 0)),
            pl.BlockSpec((1, D), lambda i: (0, 0)),
            pl.BlockSpec((1, 6, D), lambda i: (i // per_b, 0, 0)),
        ],
        out_specs=pl.BlockSpec((tm, D), lambda i: (i, 0)),
        out_shape=jax.ShapeDtypeStruct((N, D), BF16),
        compiler_params=_cparams(("arbitrary",)),
        name="norm_mod",
    )(x, g.reshape(1, D), mod)


def _final_norm_kernel(x_ref, g_ref, o_ref):
    x = x_ref[...]
    ms = jnp.mean(x * x, axis=-1, keepdims=True)
    o_ref[...] = x * lax.rsqrt(ms + NORM_EPS) * g_ref[...]


def _final_norm(x, g, tm=256):
    N, D = x.shape
    tm = min(tm, N)
    return pl.pallas_call(
        _final_norm_kernel,
        grid=(N // tm,),
        in_specs=[pl.BlockSpec((tm, D), lambda i: (i, 0)), pl.BlockSpec((1, D), lambda i: (0, 0))],
        out_specs=pl.BlockSpec((tm, D), lambda i: (i, 0)),
        out_shape=jax.ShapeDtypeStruct((N, D), F32),
        compiler_params=_cparams(("arbitrary",)),
        name="final_norm",
    )(x, g.reshape(1, D))


def _weight_scratch(*weights_and_tiles):
    return [pltpu.VMEM(tile, BF16) for w, tile in weights_and_tiles if w.dtype != BF16]


def _weight_spec(w, layer, rows, tn, row_block=0):
    if w.ndim == 2:
        return pl.BlockSpec((rows, tn), lambda j, i: (row_block, j))
    return pl.BlockSpec((None, rows, tn), lambda j, i: (layer, row_block, j))


def _resident_bf16(b_ref, scratch):
    if b_ref.dtype == BF16:
        return b_ref[...]
    s_ref = scratch.pop(0)

    @pl.when(pl.program_id(1) == 0)
    def _():
        s_ref[...] = b_ref[...].astype(BF16)

    return s_ref[...]


def _mm_kernel(a_ref, b_ref, o_ref, *scratch):
    b = _resident_bf16(b_ref, list(scratch))
    o_ref[...] = jnp.dot(a_ref[...], b, preferred_element_type=F32).astype(o_ref.dtype)


def _matmul(a, b, tn, out_dtype=BF16, tm=1024, name="matmul", layer=0):
    M, K = a.shape
    Nc = b.shape[-1]
    tm = min(tm, M)
    return pl.pallas_call(
        _mm_kernel,
        grid=(Nc // tn, M // tm),
        in_specs=[pl.BlockSpec((tm, K), lambda j, i: (i, 0)), _weight_spec(b, layer, K, tn)],
        out_specs=pl.BlockSpec((tm, tn), lambda j, i: (i, j)),
        out_shape=jax.ShapeDtypeStruct((M, Nc), out_dtype),
        scratch_shapes=_weight_scratch((b, (K, tn))),
        compiler_params=_cparams(("arbitrary", "arbitrary")),
        name=name,
    )(a, b)


def _mm_glu_kernel(a_ref, bg_ref, bu_ref, o_ref, *scratch):
    scratch = list(scratch)
    bg = _resident_bf16(bg_ref, scratch)
    bu = _resident_bf16(bu_ref, scratch)
    a = a_ref[...]
    g = jnp.dot(a, bg, preferred_element_type=F32)
    u = jnp.dot(a, bu, preferred_element_type=F32)
    o_ref[...] = (g * jax.nn.sigmoid(g) * u).astype(o_ref.dtype)


def _matmul_glu(a, bg, bu, tn=256, tm=1024, layer=0):
    M, K = a.shape
    Nc = bg.shape[-1]
    tm = min(tm, M)
    return pl.pallas_call(
        _mm_glu_kernel,
        grid=(Nc // tn, M // tm),
        in_specs=[
            pl.BlockSpec((tm, K), lambda j, i: (i, 0)),
            _weight_spec(bg, layer, K, tn),
            _weight_spec(bu, layer, K, tn),
        ],
        out_specs=pl.BlockSpec((tm, tn), lambda j, i: (i, j)),
        out_shape=jax.ShapeDtypeStruct((M, Nc), BF16),
        scratch_shapes=_weight_scratch((bg, (K, tn)), (bu, (K, tn))),
        compiler_params=_cparams(("arbitrary", "arbitrary")),
        name="shared_glu",
    )(a, bg, bu)


def _mm_residual_kernel(a_ref, b_ref, x_ref, mod_ref, o_ref, *scratch, gate_row):
    b = _resident_bf16(b_ref, list(scratch))
    acc = jnp.dot(a_ref[...], b, preferred_element_type=F32)
    o_ref[...] = x_ref[...] + mod_ref[0, gate_row:gate_row + 1, :] * acc


def _matmul_residual(a, b, x, mod, gate_row, seq, tn=512, tm=1024, layer=0):
    M, K = a.shape
    Nc = b.shape[-1]
    tm = min(tm, seq)
    per_b = seq // tm
    return pl.pallas_call(
        functools.partial(_mm_residual_kernel, gate_row=gate_row),
        grid=(Nc // tn, M // tm),
        in_specs=[
            pl.BlockSpec((tm, K), lambda j, i: (i, 0)),
            _weight_spec(b, layer, K, tn),
            pl.BlockSpec((tm, tn), lambda j, i: (i, j)),
            pl.BlockSpec((1, 6, tn), lambda j, i: (i // per_b, 0, j)),
        ],
        out_specs=pl.BlockSpec((tm, tn), lambda j, i: (i, j)),
        out_shape=jax.ShapeDtypeStruct((M, Nc), F32),
        scratch_shapes=_weight_scratch((b, (K, tn))),
        compiler_params=_cparams(("arbitrary", "arbitrary")),
        name="out_proj_residual",
    )(a, b, x, mod)


def _rope_table_kernel(pos_ref, inv128_ref, inv64_ref, c128_ref, s128_ref, c64_ref, s64_ref):
    pos = pos_ref[...]
    lane = lax.broadcasted_iota(jnp.int32, pos.shape, 1)
    a = pos * inv128_ref[...]
    c128_ref[...] = jnp.cos(a)
    s128_ref[...] = jnp.where(lane < 64, -jnp.sin(a), jnp.sin(a))
    a = pos * inv64_ref[...]
    c64_ref[...] = jnp.cos(a)
    s64_ref[...] = jnp.where((lane & 63) < 32, -jnp.sin(a), jnp.sin(a))


def _rope_tables(positions, tm=512):
    N = positions.size
    tm = min(tm, N)
    pos = jnp.broadcast_to(positions.astype(F32).reshape(N, 1), (N, LANES))

    def inv(half):
        f = np.float32(ROPE_THETA) ** (-(np.arange(half, dtype=np.float32) / np.float32(half)))
        return jnp.asarray(np.tile(f.astype(np.float32), LANES // half).reshape(1, LANES))

    tab = jax.ShapeDtypeStruct((N, LANES), F32)
    row = pl.BlockSpec((tm, LANES), lambda i: (i, 0))
    one = pl.BlockSpec((1, LANES), lambda i: (0, 0))
    return pl.pallas_call(
        _rope_table_kernel,
        grid=(N // tm,),
        in_specs=[row, one, one],
        out_specs=[row, row, row, row],
        out_shape=[tab, tab, tab, tab],
        compiler_params=_cparams(("arbitrary",)),
        name="rope_tables",
    )(pos, inv(64), inv(32))


def _rot128(x):
    return pltpu.roll(x, 64, 1)


def _rot64(x):
    lane = lax.broadcasted_iota(jnp.int32, x.shape, 1)
    return jnp.where((lane & 63) < 32, pltpu.roll(x, 96, 1), pltpu.roll(x, 32, 1))


def _r_block_plan():
    plan = [(128, HEAD_DIM ** -0.5)] * 4 + [(128, 1.0)] * 4
    plan += [(64, DIFF_QK_DIM ** -0.5)] * 8 + [(64, 1.0)] * 8
    plan += [(128, HEAD_DIM ** -0.5)] * 8 + [(128, 1.0)] * 2
    plan += [None] * ((R_COLS - R_USED) // LANES)
    return plan


def _rope_kernel(p_ref, c128_ref, s128_ref, c64_ref, s64_ref, o_ref):
    c128, s128, c64, s64 = c128_ref[...], s128_ref[...], c64_ref[...], s64_ref[...]
    for blk, spec in enumerate(_r_block_plan()):
        cols = slice(blk * LANES, (blk + 1) * LANES)
        if spec is None:
            o_ref[:, cols] = jnp.zeros((o_ref.shape[0], LANES), o_ref.dtype)
            continue
        flavour, scale = spec
        x = p_ref[:, cols].astype(F32)
        if flavour == 128:
            y = x * c128 + _rot128(x) * s128
        else:
            y = x * c64 + _rot64(x) * s64
        if scale != 1.0:
            y = y * scale
        o_ref[:, cols] = y.astype(o_ref.dtype)


def _rope_section(proj_r, tabs, tm=256):
    N = proj_r.shape[0]
    tm = min(tm, N)
    row = pl.BlockSpec((tm, R_COLS), lambda i: (i, 0))
    tab = pl.BlockSpec((tm, LANES), lambda i: (i, 0))
    return pl.pallas_call(
        _rope_kernel,
        grid=(N // tm,),
        in_specs=[row, tab, tab, tab, tab],
        out_specs=row,
        out_shape=jax.ShapeDtypeStruct((N, R_COLS), BF16),
        compiler_params=_cparams(("arbitrary",)),
        name="rope_section",
    )(proj_r, *tabs)


def _dil_rope_kernel(x_ref, c128_ref, s128_ref, *refs, dils, tm):
    scr = refs[-1]
    outs = refs[:-1]
    c128, s128 = c128_ref[...], s128_ref[...]
    for g, d in enumerate(dils):
        for part in range(3):
            o_ref = outs[g * 3 + part]
            base = (g * 3 + part) * DIL_HEADS
            for c in range(DIL_HEADS):
                col = g * X_GROUP + part * HW4 + c * LANES
                x = x_ref[:, col:col + LANES].astype(F32)
                if part < 2:
                    x = x * c128 + _rot128(x) * s128
                if part == 0:
                    x = x * (HEAD_DIM ** -0.5)
                scr[base + c] = x
            for r in range(d):
                for c in range(DIL_HEADS):
                    o_ref[r, :, c * LANES:(c + 1) * LANES] = (
                        scr[base + c, pl.ds(r, tm // d, stride=d), :].astype(o_ref.dtype))


def _dil_rope_section(proj_x, c128, s128, batch, seq, dils, tm=256):
    N = proj_x.shape[0]
    tm = min(tm, seq)
    per_b = seq // tm
    tab = pl.BlockSpec((tm, LANES), lambda i: (i, 0))
    out_specs, out_shape = [], []
    for d in dils:
        for _ in range(3):
            out_specs.append(pl.BlockSpec((d, tm // d, HW4), lambda i: (i // per_b, i % per_b, 0)))
            out_shape.append(jax.ShapeDtypeStruct((batch * d, seq // d, HW4), BF16))
    outs = pl.pallas_call(
        functools.partial(_dil_rope_kernel, dils=dils, tm=tm),
        grid=(N // tm,),
        in_specs=[pl.BlockSpec((tm, X_COLS), lambda i: (i, 0)), tab, tab],
        out_specs=out_specs,
        out_shape=out_shape,
        scratch_shapes=[pltpu.VMEM((len(dils) * 3 * DIL_HEADS, tm, LANES), F32)],
        compiler_params=_cparams(("arbitrary",)),
        name="dil_rope_section",
    )(proj_x, c128, s128)
    return [outs[3 * g:3 * g + 3] for g in range(len(dils))]


MLA_QW = 256


def _mla_prep_kernel(a_ref, gq_ref, gkv_ref, wq_ref, wkv_ref, c64_ref, s64_ref,
                     q_ref, kn_ref, kr_ref, v_ref):
    a = a_ref[...].astype(F32)
    c64, s64 = c64_ref[...], s64_ref[...]
    scale = (MLA_NOPE_DIM + MLA_ROPE_DIM) ** -0.5

    cq = a[:, :MLA_Q_RANK]
    cq = cq * lax.rsqrt(jnp.mean(cq * cq, axis=-1, keepdims=True) + NORM_EPS) * gq_ref[...]
    q = jnp.dot(cq.astype(BF16), wq_ref[...], preferred_element_type=F32)
    for h in range(MLA_HEADS):
        qn = q[:, h * MLA_QW:h * MLA_QW + LANES]
        qr = q[:, h * MLA_QW + LANES:(h + 1) * MLA_QW]
        qr = qr * c64 + _rot64(qr) * s64
        q_ref[:, h * MLA_QW:h * MLA_QW + LANES] = (qn * scale).astype(q_ref.dtype)
        q_ref[:, h * MLA_QW + LANES:(h + 1) * MLA_QW] = (qr * scale).astype(q_ref.dtype)

    ckv = a[:, MLA_Q_RANK:MLA_Q_RANK + MLA_KV_RANK]
    ckv = ckv * lax.rsqrt(jnp.mean(ckv * ckv, axis=-1, keepdims=True) + NORM_EPS) * gkv_ref[...]
    kv = jnp.dot(ckv.astype(BF16), wkv_ref[...], preferred_element_type=F32)
    kn_ref[...] = kv[:, :MLA_HEADS * MLA_NOPE_DIM].astype(kn_ref.dtype)
    v_ref[...] = kv[:, MLA_HEADS * MLA_NOPE_DIM:].astype(v_ref.dtype)

    kr = a[:, MLA_Q_RANK + MLA_KV_RANK:]
    kr_ref[...] = (kr * c64 + _rot64(kr) * s64).astype(kr_ref.dtype)


def _mla_prep(proj_a, gq, gkv, wq, wkv, c64, s64, tm=256):
    N = proj_a.shape[0]
    tm = min(tm, N)
    row = lambda w: pl.BlockSpec((tm, w), lambda i: (i, 0))
    full = lambda r, c: pl.BlockSpec((r, c), lambda i: (0, 0))
    hq = MLA_HEADS * MLA_QW
    hk = MLA_HEADS * MLA_NOPE_DIM
    return pl.pallas_call(
        _mla_prep_kernel,
        grid=(N // tm,),
        in_specs=[row(A_PAD), full(1, MLA_Q_RANK), full(1, MLA_KV_RANK), full(MLA_Q_RANK, hq),
                  full(MLA_KV_RANK, 2 * hk), row(LANES), row(LANES)],
        out_specs=[row(hq), row(hk), row(LANES), row(hk)],
        out_shape=[jax.ShapeDtypeStruct((N, hq), BF16), jax.ShapeDtypeStruct((N, hk), BF16),
                   jax.ShapeDtypeStruct((N, LANES), BF16), jax.ShapeDtypeStruct((N, hk), BF16)],
        compiler_params=_cparams(("arbitrary",)),
        name="mla_prep",
    )(proj_a, gq.reshape(1, -1), gkv.reshape(1, -1), wq, wkv, c64, s64)


def _softmax_pv(q, k_ref, v_ref, kc):
    tq = q.shape[0]
    S = k_ref.shape[0]
    m = jnp.full((tq, 1), -jnp.inf, F32)
    l = jnp.zeros((tq, 1), F32)
    acc = jnp.zeros((tq, v_ref.shape[1]), F32)
    for c in range(S // kc):
        s = _nt_dot(q, k_ref[c * kc:(c + 1) * kc, :])
        m_new = jnp.maximum(m, jnp.max(s, axis=-1, keepdims=True))
        alpha = jnp.exp(m - m_new)
        p = jnp.exp(s - m_new)
        l = alpha * l + jnp.sum(p, axis=-1, keepdims=True)
        acc = alpha * acc + jnp.dot(p.astype(BF16), v_ref[c * kc:(c + 1) * kc, :],
                                    preferred_element_type=F32)
        m = m_new
    return acc / l


def _mla_attn_kernel(q_ref, kn_ref, kr_ref, v_ref, o_ref, kcat_ref, *, kc):
    @pl.when(pl.program_id(2) == 0)
    def _():
        kcat_ref[:, :LANES] = kn_ref[...]
        kcat_ref[:, LANES:] = kr_ref[...]

    o_ref[...] = _softmax_pv(q_ref[...], kcat_ref, v_ref, kc).astype(o_ref.dtype)


def _mla_attention(qm, kn, kr, vm, batch, seq, tq=1024, kc=1024):
    N = qm.shape[0]
    tq = min(tq, seq)
    kc = min(kc, seq)
    nq = seq // tq
    return pl.pallas_call(
        functools.partial(_mla_attn_kernel, kc=kc),
        grid=(batch, MLA_HEADS, nq),
        in_specs=[
            pl.BlockSpec((tq, MLA_QW), lambda b, h, i: (b * nq + i, h)),
            pl.BlockSpec((seq, LANES), lambda b, h, i: (b, h)),
            pl.BlockSpec((seq, LANES), lambda b, h, i: (b, 0)),
            pl.BlockSpec((seq, LANES), lambda b, h, i: (b, h)),
        ],
        out_specs=pl.BlockSpec((tq, LANES), lambda b, h, i: (b * nq + i, h)),
        out_shape=jax.ShapeDtypeStruct((N, MLA_HEADS * MLA_V_DIM), BF16),
        scratch_shapes=[pltpu.VMEM((seq, MLA_QW), BF16)],
        compiler_params=_cparams(("arbitrary", "arbitrary", "arbitrary")),
        name="mla_attention",
    )(qm, kn, kr, vm)


def _diff_attn_kernel(q_ref, k_ref, v_ref, lam_ref, g_ref, o_ref, *, kc, lam_init):
    q = q_ref[...]
    lane = lax.broadcasted_iota(jnp.int32, q.shape, 1)
    zero = jnp.zeros_like(q)
    o0 = _softmax_pv(jnp.where(lane < DIFF_QK_DIM, q, zero), k_ref, v_ref, kc)
    o1 = _softmax_pv(jnp.where(lane >= DIFF_QK_DIM, q, zero), k_ref, v_ref, kc)
    lp = lam_ref[...]
    lam = (jnp.exp(jnp.sum(lp[0:1] * lp[1:2], axis=-1, keepdims=True))
           - jnp.exp(jnp.sum(lp[2:3] * lp[3:4], axis=-1, keepdims=True)) + lam_init)
    o = o0 - lam * o1
    o = o * lax.rsqrt(jnp.mean(o * o, axis=-1, keepdims=True) + DIFF_NORM_EPS) * g_ref[...]
    o_ref[...] = (o * (1.0 - lam_init)).astype(o_ref.dtype)


def _diff_attention(rop, proj_v, lam_params, subln_g, lam_init, batch, seq, tq=1024, kc=1024):
    N = rop.shape[0]
    tq = min(tq, seq)
    kc = min(kc, seq)
    nq = seq // tq
    qb, kb, vb = R_DIFF_Q // LANES, R_DIFF_K // LANES, V_DIFF // LANES
    return pl.pallas_call(
        functools.partial(_diff_attn_kernel, kc=kc, lam_init=lam_init),
        grid=(batch, DIFF_HEADS, nq),
        in_specs=[
            pl.BlockSpec((tq, LANES), lambda b, h, i: (b * nq + i, qb + h)),
            pl.BlockSpec((seq, LANES), lambda b, h, i: (b, kb + h)),
            pl.BlockSpec((seq, LANES), lambda b, h, i: (b, vb + h)),
            pl.BlockSpec((4, DIFF_QK_DIM), lambda b, h, i: (0, 0)),
            pl.BlockSpec((1, DIFF_V_DIM), lambda b, h, i: (0, 0)),
        ],
        out_specs=pl.BlockSpec((tq, LANES), lambda b, h, i: (b * nq + i, h)),
        out_shape=jax.ShapeDtypeStruct((N, DIFF_HEADS * DIFF_V_DIM), BF16),
        compiler_params=_cparams(("arbitrary", "arbitrary", "arbitrary")),
        name="diff_attention",
    )(rop, rop, proj_v, lam_params, subln_g.reshape(1, -1))


def _banded_kernel(*refs, half_window, blk, blocks_per_seq, grp, use_sink, want_lse):
    q_ref, kp_ref, kc_ref, kn_ref, vp_ref, vc_ref, vn_ref = refs[:7]
    rest = list(refs[7:])
    sink_ref = rest.pop(0) if use_sink else None
    o_ref = rest.pop(0)
    lse_ref = rest.pop(0) if want_lse else None

    il = lax.rem(pl.program_id(0), blocks_per_seq)
    qpos = il * blk + lax.broadcasted_iota(jnp.int32, (blk, 3 * blk), 0)
    kpos = (il - 1) * blk + lax.broadcasted_iota(jnp.int32, (blk, 3 * blk), 1)
    valid = ((jnp.abs(qpos - kpos) <= half_window) & (kpos >= 0) & (kpos < blocks_per_seq * blk))

    heads = range(DIL_HEADS)
    cq = [slice(h * LANES, (h + 1) * LANES) for h in heads]
    ck = [slice((h // grp) * LANES, (h // grp + 1) * LANES) for h in heads]
    scores = []
    for h in heads:
        k = jnp.concatenate([kp_ref[:, ck[h]], kc_ref[:, ck[h]], kn_ref[:, ck[h]]], axis=0)
        scores.append(jnp.where(valid, _nt_dot(q_ref[:, cq[h]], k), -jnp.inf))
    probs, dens, maxes = [], [], []
    for h in heads:
        m = jnp.max(scores[h], axis=-1, keepdims=True)
        if use_sink:
            sk = jnp.max(jnp.broadcast_to(sink_ref[0:1, cq[h]], (blk, LANES)), axis=-1, keepdims=True)
            m = jnp.maximum(m, sk)
        p = jnp.exp(scores[h] - m)
        den = jnp.sum(p, axis=-1, keepdims=True)
        if use_sink:
            den = den + jnp.exp(sk - m)
        probs.append(p.astype(BF16))
        dens.append(den)
        maxes.append(m)
    for h in heads:
        v = jnp.concatenate([vp_ref[:, ck[h]], vc_ref[:, ck[h]], vn_ref[:, ck[h]]], axis=0)
        o = jnp.dot(probs[h], v, preferred_element_type=F32) / dens[h]
        o_ref[:, cq[h]] = o.astype(o_ref.dtype)
        if want_lse:
            lse_ref[:, cq[h]] = jnp.broadcast_to(maxes[h] + jnp.log(dens[h]), (blk, LANES))


def _banded_attention(q_arr, k_arr, v_arr, *, sub_len, half_window, q_col, k_col, v_col,
                      n_col_groups, kv_width, grp, sink=None, want_lse=False):
    rows = q_arr.shape[0]
    blk = min(128, sub_len)
    blocks_per_seq = sub_len // blk
    n_row_blocks = rows // blk
    last = n_row_blocks - 1

    def prev(i):
        return jnp.maximum(i - 1, 0)

    def nxt(i):
        return jnp.minimum(i + 1, last)

    qspec = pl.BlockSpec((blk, HW4), lambda i, c: (i, q_col + c))
    kspecs = [pl.BlockSpec((blk, kv_width), lambda i, c, f=f: (f(i), k_col + c))
              for f in (prev, lambda i: i, nxt)]
    vspecs = [pl.BlockSpec((blk, kv_width), lambda i, c, f=f: (f(i), v_col + c))
              for f in (prev, lambda i: i, nxt)]
    in_specs = [qspec] + kspecs + vspecs
    args = [q_arr, k_arr, k_arr, k_arr, v_arr, v_arr, v_arr]
    if sink is not None:
        in_specs.append(pl.BlockSpec((1, HW4), lambda i, c: (0, c)))
        args.append(sink)
    out_cols = n_col_groups * HW4
    ospec = pl.BlockSpec((blk, HW4), lambda i, c: (i, c))
    out_specs = [ospec]
    out_shape = [jax.ShapeDtypeStruct((rows, out_cols), BF16)]
    if want_lse:
        out_specs.append(ospec)
        out_shape.append(jax.ShapeDtypeStruct((rows, out_cols), F32))
    return pl.pallas_call(
        functools.partial(_banded_kernel, half_window=half_window, blk=blk,
                          blocks_per_seq=blocks_per_seq, grp=grp, use_sink=sink is not None,
                          want_lse=want_lse),
        grid=(n_row_blocks, n_col_groups),
        in_specs=in_specs,
        out_specs=out_specs,
        out_shape=out_shape,
        compiler_params=_cparams(("arbitrary", "arbitrary")),
        name="banded_attention",
    )(*args)


def _dil_combine_kernel(o0, l0, o1, l1, o2, l2, out_ref, scr, *, dils, tm):
    def natural(o_ref, l_ref, d, base):
        for r in range(d):
            for c in range(DIL_HEADS):
                cols = slice(c * LANES, (c + 1) * LANES)
                scr[base + c, pl.ds(r, tm // d, stride=d), :] = o_ref[r, :, cols].astype(F32)
                scr[base + DIL_HEADS + c, pl.ds(r, tm // d, stride=d), :] = l_ref[r, :, cols]

    natural(o1, l1, dils[0], 0)
    natural(o2, l2, dils[1], 2 * DIL_HEADS)
    for c in range(DIL_HEADS):
        cols = slice(c * LANES, (c + 1) * LANES)
        oa, la = o0[:, cols].astype(F32), l0[:, cols]
        ob, lb = scr[c], scr[DIL_HEADS + c]
        oc, lc = scr[2 * DIL_HEADS + c], scr[3 * DIL_HEADS + c]
        m = jnp.maximum(jnp.maximum(la, lb), lc)
        ea, eb, ec = jnp.exp(la - m), jnp.exp(lb - m), jnp.exp(lc - m)
        inv = 1.0 / (ea + eb + ec)
        out_ref[:, cols] = ((ea * inv) * oa + (eb * inv) * ob + (ec * inv) * oc).astype(out_ref.dtype)


def _dil_combine(o0, l0, o1, l1, o2, l2, batch, seq, dils, tm=256):
    N = o0.shape[0]
    tm = min(tm, seq)
    per_b = seq // tm
    row = pl.BlockSpec((tm, HW4), lambda i: (i, 0))
    grouped = [pl.BlockSpec((d, tm // d, HW4), lambda i: (i // per_b, i % per_b, 0)) for d in dils]
    shaped = lambda a, d: a.reshape(batch * d, seq // d, HW4)
    return pl.pallas_call(
        functools.partial(_dil_combine_kernel, dils=dils, tm=tm),
        grid=(N // tm,),
        in_specs=[row, row, grouped[0], grouped[0], grouped[1], grouped[1]],
        out_specs=row,
        out_shape=jax.ShapeDtypeStruct((N, HW4), BF16),
        scratch_shapes=[pltpu.VMEM((4 * DIL_HEADS, tm, LANES), F32)],
        compiler_params=_cparams(("arbitrary",)),
        name="dil_combine",
    )(o0, l0, shaped(o1, dils[0]), shaped(l1, dils[0]), shaped(o2, dils[1]), shaped(l2, dils[1]))


def _merge_kernel(oa, ob, oc, od, wa, wb, wc, wd, ga, gb, gc, gd, o_ref, *scratch):
    scratch = list(scratch)
    acc = None
    for o, w, g in ((oa, wa, ga), (ob, wb, gb), (oc, wc, gc), (od, wd, gd)):
        wt = _resident_bf16(w, scratch)
        t = jax.nn.sigmoid(g[...].astype(F32)) * jnp.dot(o[...], wt, preferred_element_type=F32)
        acc = t if acc is None else acc + t
    o_ref[...] = acc.astype(o_ref.dtype)


def _merge(branch_outs, w_branch, gates, layer, tn=512, tm=1024):
    N = gates.shape[0]
    D = D_MODEL
    tm = min(tm, N)
    oa, ob, oc, od = branch_outs
    in_specs = [pl.BlockSpec((tm, o.shape[1]), lambda j, i: (i, 0)) for o in branch_outs]
    rows = [o.shape[1] for o in branch_outs]
    row_blocks = [0, 1, 2, 3072 // rows[3]]
    in_specs += [_weight_spec(w_branch, layer, rows[r], tn, row_blocks[r]) for r in range(4)]
    nj = D // tn
    in_specs += [pl.BlockSpec((tm, tn), lambda j, i, r=r: (i, r * nj + j)) for r in range(4)]
    return pl.pallas_call(
        _merge_kernel,
        grid=(nj, N // tm),
        in_specs=in_specs,
        out_specs=pl.BlockSpec((tm, tn), lambda j, i: (i, j)),
        out_shape=jax.ShapeDtypeStruct((N, D), BF16),
        scratch_shapes=_weight_scratch(*[(w_branch, (rows[r], tn)) for r in range(4)]),
        compiler_params=_cparams(("arbitrary", "arbitrary")),
        name="branch_merge",
    )(oa, ob, oc, od, w_branch, w_branch, w_branch, w_branch, gates, gates, gates, gates)


VMEM_PITCH = 20


def _pack_pairs(lo, hi):
    lo_b = lax.bitcast_convert_type(lo.astype(BF16).astype(F32), jnp.uint32)
    hi_b = lax.bitcast_convert_type(hi.astype(BF16).astype(F32), jnp.uint32)
    return (lo_b >> 16) | (hi_b & jnp.uint32(0xFFFF0000))


def _unpack_pairs(w):
    lo = lax.bitcast_convert_type(w << 16, F32)
    hi = lax.bitcast_convert_type(w & jnp.uint32(0xFFFF0000), F32)
    return lo, hi


def _store_packed(o_ref, x):
    rows = x.shape[0]
    for j in range(PACK_SLABS):
        o_ref[pl.ds(j, rows, stride=PACK_SLABS), :] = _pack_pairs(
            x[:, j * LANES:(j + 1) * LANES], x[:, PACK_HALF + j * LANES:PACK_HALF + (j + 1) * LANES])


def _load_slab(buf_ref, first_row, rows, j):
    return buf_ref[pl.ds(first_row + j, rows, stride=VMEM_PITCH), :]


def _router_kernel(x_ref, g_ref, mod_ref, rw_ref, rb_ref, hb_ref, hp_ref, idx_ref, wt_ref, *,
                   shift_row, scale_row):
    x = x_ref[...]
    ms = jnp.mean(x * x, axis=-1, keepdims=True)
    y = x * lax.rsqrt(ms + NORM_EPS) * g_ref[...]
    h = y * (1.0 + mod_ref[0, scale_row:scale_row + 1, :]) + mod_ref[0, shift_row:shift_row + 1, :]
    hb_ref[...] = h.astype(hb_ref.dtype)
    _store_packed(hp_ref, h)

    tm = x.shape[0]
    logits = jnp.dot(h, rw_ref[...], preferred_element_type=F32, precision=lax.Precision.HIGHEST)
    lt = logits.T[:N_EXPERTS, :]
    scores = jax.nn.sigmoid(lt)
    choice = scores + rb_ref[...]

    per_group = N_EXPERTS // N_EXPERT_GROUPS
    sub = lax.broadcasted_iota(jnp.int32, (per_group, tm), 0)
    group_score = []
    for g in range(N_EXPERT_GROUPS):
        cg = choice[g * per_group:(g + 1) * per_group, :]
        m1 = jnp.max(cg, axis=0, keepdims=True)
        first = jnp.min(jnp.where(cg == m1, sub, per_group), axis=0, keepdims=True)
        m2 = jnp.max(jnp.where(sub == first, -jnp.inf, cg), axis=0, keepdims=True)
        group_score.append(m1 + m2)
    masked = []
    for g in range(N_EXPERT_GROUPS):
        ahead = jnp.zeros((1, tm), jnp.int32)
        for o in range(N_EXPERT_GROUPS):
            if o == g:
                continue
            better = (group_score[o] >= group_score[g]) if o < g else (group_score[o] > group_score[g])
            ahead = ahead + better.astype(jnp.int32)
        cg = choice[g * per_group:(g + 1) * per_group, :]
        masked.append(jnp.where(ahead < TOP_GROUPS, cg, -jnp.inf))
    cm = jnp.concatenate(masked, axis=0)

    eidx = lax.broadcasted_iota(jnp.int32, (N_EXPERTS, tm), 0)
    rank = jnp.zeros((N_EXPERTS, tm), jnp.int32)
    for e in range(N_EXPERTS):
        row = cm[e:e + 1, :]
        tie = jnp.where(eidx > e, 1, 0)
        rank = rank + jnp.where(row > cm, 1, jnp.where(row == cm, tie, 0))
    sel = rank < TOP_K
    wsel = jnp.where(sel, scores, 0.0)
    wsel = wsel / jnp.sum(wsel, axis=0, keepdims=True) * ROUTED_SCALE
    idx_rows, wt_rows = [], []
    for r in range(TOP_K):
        hit = rank == r
        idx_rows.append(jnp.sum(jnp.where(hit, eidx, 0), axis=0, keepdims=True))
        wt_rows.append(jnp.sum(jnp.where(hit, wsel, 0.0), axis=0, keepdims=True))
    idx_ref[...] = jnp.concatenate(idx_rows, axis=0)
    wt_ref[...] = jnp.concatenate(wt_rows, axis=0)


def _router(x, g, mod, shift_row, scale_row, router_w, router_bias, seq, tm=256):
    N, D = x.shape
    tm = min(tm, seq)
    per_b = seq // tm
    rw = jnp.zeros((D, LANES), F32).at[:, :N_EXPERTS].set(router_w)
    row = pl.BlockSpec((tm, D), lambda i: (i, 0))
    col = lambda r: pl.BlockSpec((r, tm), lambda i: (0, i))
    return pl.pallas_call(
        functools.partial(_router_kernel, shift_row=shift_row, scale_row=scale_row),
        grid=(N // tm,),
        in_specs=[
            row,
            pl.BlockSpec((1, D), lambda i: (0, 0)),
            pl.BlockSpec((1, 6, D), lambda i: (i // per_b, 0, 0)),
            pl.BlockSpec((D, LANES), lambda i: (0, 0)),
            pl.BlockSpec((N_EXPERTS, 1), lambda i: (0, 0)),
        ],
        out_specs=[row, pl.BlockSpec((tm * PACK_SLABS, LANES), lambda i: (i, 0)), col(TOP_K), col(TOP_K)],
        out_shape=[jax.ShapeDtypeStruct((N, D), BF16), jax.ShapeDtypeStruct((N * PACK_SLABS, LANES), jnp.uint32),
                   jax.ShapeDtypeStruct((TOP_K, N), jnp.int32), jax.ShapeDtypeStruct((TOP_K, N), F32)],
        compiler_params=_cparams(("arbitrary",)),
        name="ffn_norm_router",
    )(x, g.reshape(1, D), mod, rw, router_bias.reshape(N_EXPERTS, 1))


def _num_expert_blocks(n_tokens):
    bm = EXPERT_BLOCK_ROWS
    return -(-(n_tokens * TOP_K + N_EXPERTS * (bm - 1)) // bm)


def _dispatch_plan(top_idx):
    K, N = top_idx.shape
    bm = EXPERT_BLOCK_ROWS
    n_blocks = _num_expert_blocks(N)
    experts = jnp.arange(N_EXPERTS, dtype=jnp.int32)
    onehot = top_idx[:, None, :] == experts[None, :, None]
    mask = jnp.any(onehot, axis=0).astype(jnp.int32)
    counts = jnp.sum(mask, axis=1)
    pos = jnp.cumsum(mask, axis=1) - mask
    nblk = (counts + bm - 1) // bm
    blk_end = jnp.cumsum(nblk)
    blk_start = blk_end - nblk
    n_used = blk_end[-1]
    start = jnp.cumsum(counts) - counts
    slot_en = (blk_start * bm)[:, None] + pos
    slot_of = jnp.sum(jnp.where(onehot, slot_en[None], 0), axis=1)

    keys = top_idx * N + jnp.arange(N, dtype=jnp.int32)[None, :]
    tok_sorted = jnp.sort(keys.reshape(-1)) % N

    bidx = jnp.arange(n_blocks + 1, dtype=jnp.int32)
    be = jnp.minimum(jnp.sum((bidx[:, None] >= blk_end[None, :]).astype(jnp.int32), axis=1),
                     N_EXPERTS - 1)
    be = jnp.where(bidx < n_used, be, be[jnp.maximum(n_used - 1, 0)])
    off = jnp.clip(start[be] + (bidx - blk_start[be]) * bm, 0, K * N)
    tok_pad = jnp.concatenate([tok_sorted, jnp.zeros((bm,), jnp.int32)])
    return (be.astype(jnp.int32), off.astype(jnp.int32), tok_pad.astype(jnp.int32),
            slot_of.astype(jnp.int32), n_used.astype(jnp.int32))


def _expert_kernel(be_ref, off_ref, tok_ref, nu_ref, h_hbm, wg_ref, wu_ref, wd_ref, y_ref,
                   xbuf0, xbuf1, xs, wg_s, wu_s, wd_s, sem):
    bm = EXPERT_BLOCK_ROWS
    b = pl.program_id(0)
    n_used = nu_ref[0]
    bufs = (xbuf0, xbuf1)

    def start_gather(blk, slot, unrolled):
        base = off_ref[blk]

        def one(r, priority):
            src = pl.multiple_of(tok_ref[base + r] * PACK_SLABS, PACK_SLABS)
            pltpu.make_async_copy(h_hbm.at[pl.ds(src, PACK_SLABS), :],
                                  bufs[slot].at[pl.ds(r * VMEM_PITCH, PACK_SLABS), :],
                                  sem.at[slot]).start(priority=priority)

        if unrolled:
            for r in range(bm):
                one(r, r % 2)
        else:
            def body(i, carry):
                one(2 * i, 0)
                one(2 * i + 1, 1)
                return carry
            lax.fori_loop(0, bm // 2, body, 0)

    def wait_gather(slot):
        pltpu.make_async_copy(h_hbm.at[pl.ds(0, bm * PACK_SLABS), :],
                              bufs[slot].at[pl.ds(0, bm * PACK_SLABS), :], sem.at[slot]).wait()

    @pl.when(b == 0)
    def _():
        start_gather(0, 0, False)

    new_expert = (b == 0) | (be_ref[b] != be_ref[jnp.maximum(b - 1, 0)])

    @pl.when(new_expert & (b < n_used))
    def _():
        wg_s[...] = wg_ref[0, 0].astype(BF16)
        wu_s[...] = wu_ref[0, 0].astype(BF16)
        wd_s[...] = wd_ref[0, 0].astype(BF16)

    def work(slot):
        wait_gather(slot)
        for j in range(PACK_SLABS):
            lo, hi = _unpack_pairs(_load_slab(bufs[slot], 0, bm, j))
            xs[:, j * LANES:(j + 1) * LANES] = lo.astype(BF16)
            xs[:, PACK_HALF + j * LANES:PACK_HALF + (j + 1) * LANES] = hi.astype(BF16)
        start_gather(b + 1, 1 - slot, True)
        x = xs[...]
        g = jnp.dot(x, wg_s[...], preferred_element_type=F32)
        u = jnp.dot(x, wu_s[...], preferred_element_type=F32)
        a = (g * jax.nn.sigmoid(g) * u).astype(BF16)
        _store_packed(y_ref, jnp.dot(a, wd_s[...], preferred_element_type=F32))

    for slot in (0, 1):
        @pl.when((b < n_used) & (b % 2 == slot))
        def _(slot=slot):
            work(slot)

        @pl.when((b == n_used) & (b % 2 == slot))
        def _(slot=slot):
            wait_gather(slot)

    @pl.when(b >= n_used)
    def _():
        y_ref[...] = jnp.zeros(y_ref.shape, y_ref.dtype)


def _expert_ffn(h_packed, w_gate, w_up, w_down, layer, block_expert, block_off, tok_pad, n_used):
    bm = EXPERT_BLOCK_ROWS
    D = D_MODEL
    n_steps = block_expert.shape[0]
    wspec = lambda r, c: pl.BlockSpec((1, 1, r, c), lambda b, be, off, tok, nu: (layer, be[b], 0, 0))
    grid_spec = pltpu.PrefetchScalarGridSpec(
        num_scalar_prefetch=4,
        grid=(n_steps,),
        in_specs=[pl.BlockSpec(memory_space=pl.ANY), wspec(D, EXPERT_FF), wspec(D, EXPERT_FF),
                  wspec(EXPERT_FF, D)],
        out_specs=pl.BlockSpec((bm * PACK_SLABS, LANES), lambda b, be, off, tok, nu: (b, 0)),
        scratch_shapes=[pltpu.VMEM((bm * VMEM_PITCH, LANES), jnp.uint32),
                        pltpu.VMEM((bm * VMEM_PITCH, LANES), jnp.uint32),
                        pltpu.VMEM((bm, D), BF16),
                        pltpu.VMEM((D, EXPERT_FF), BF16), pltpu.VMEM((D, EXPERT_FF), BF16),
                        pltpu.VMEM((EXPERT_FF, D), BF16),
                        pltpu.SemaphoreType.DMA((2,))],
    )
    return pl.pallas_call(
        _expert_kernel,
        grid_spec=grid_spec,
        out_shape=jax.ShapeDtypeStruct((n_steps * bm * PACK_SLABS, LANES), jnp.uint32),
        compiler_params=_cparams(("arbitrary",)),
        name="expert_ffn",
    )(block_expert, block_off, tok_pad, n_used.reshape(1), h_packed, w_gate, w_up, w_down)


def _combine_kernel(slot_ref, y_hbm, x_ref, sh_ref, w_ref, mod_ref, o_ref, ybuf0, ybuf1, sem, *,
                    gate_row, n_tokens):
    tc = COMBINE_TOKENS
    i = pl.program_id(0)
    n_steps = pl.num_programs(0)
    bufs = (ybuf0, ybuf1)

    def start_gather(step, slot, unrolled):
        base = step * tc

        def one(t, k):
            src = pl.multiple_of(slot_ref[k * n_tokens + base + t] * PACK_SLABS, PACK_SLABS)
            pltpu.make_async_copy(y_hbm.at[pl.ds(src, PACK_SLABS), :],
                                  bufs[slot].at[pl.ds((k * tc + t) * VMEM_PITCH, PACK_SLABS), :],
                                  sem.at[slot]).start(priority=k % 2)

        if unrolled:
            for t in range(tc):
                for k in range(TOP_K):
                    one(t, k)
        else:
            def body(t, carry):
                for k in range(TOP_K):
                    one(t, k)
                return carry
            lax.fori_loop(0, tc, body, 0)

    @pl.when(i == 0)
    def _():
        start_gather(0, 0, False)

    def work(slot, prefetch):
        n_rows = TOP_K * tc * PACK_SLABS
        pltpu.make_async_copy(y_hbm.at[pl.ds(0, n_rows), :], bufs[slot].at[pl.ds(0, n_rows), :],
                              sem.at[slot]).wait()
        if prefetch:
            start_gather(i + 1, 1 - slot, True)
        w = w_ref[...]
        wk = [w[:, k:k + 1] for k in range(TOP_K)]
        for j in range(PACK_SLABS):
            c_lo = slice(j * LANES, (j + 1) * LANES)
            c_hi = slice(PACK_HALF + j * LANES, PACK_HALF + (j + 1) * LANES)
            acc_lo = sh_ref[:, c_lo].astype(F32)
            acc_hi = sh_ref[:, c_hi].astype(F32)
            for k in range(TOP_K):
                lo, hi = _unpack_pairs(_load_slab(bufs[slot], k * tc * VMEM_PITCH, tc, j))
                acc_lo = acc_lo + wk[k] * lo
                acc_hi = acc_hi + wk[k] * hi
            o_ref[:, c_lo] = x_ref[:, c_lo] + mod_ref[0, gate_row:gate_row + 1, c_lo] * acc_lo
            o_ref[:, c_hi] = x_ref[:, c_hi] + mod_ref[0, gate_row:gate_row + 1, c_hi] * acc_hi

    for slot in (0, 1):
        for prefetch in (True, False):
            @pl.when((i % 2 == slot) & ((i + 1 < n_steps) == prefetch))
            def _(slot=slot, prefetch=prefetch):
                work(slot, prefetch)


def _combine(y_packed, slot_of, x, shared, w_tok, mod, gate_row, seq):
    N, D = x.shape
    tc = COMBINE_TOKENS
    per_b = seq // tc
    grid_spec = pltpu.PrefetchScalarGridSpec(
        num_scalar_prefetch=1,
        grid=(N // tc,),
        in_specs=[
            pl.BlockSpec(memory_space=pl.ANY),
            pl.BlockSpec((tc, D), lambda i, s: (i, 0)),
            pl.BlockSpec((tc, D), lambda i, s: (i, 0)),
            pl.BlockSpec((tc, TOP_K), lambda i, s: (i, 0)),
            pl.BlockSpec((1, 6, D), lambda i, s: (i // per_b, 0, 0)),
        ],
        out_specs=pl.BlockSpec((tc, D), lambda i, s: (i, 0)),
        scratch_shapes=[pltpu.VMEM((TOP_K * tc * VMEM_PITCH, LANES), jnp.uint32),
                        pltpu.VMEM((TOP_K * tc * VMEM_PITCH, LANES), jnp.uint32),
                        pltpu.SemaphoreType.DMA((2,))],
    )
    return pl.pallas_call(
        functools.partial(_combine_kernel, gate_row=gate_row, n_tokens=N),
        grid_spec=grid_spec,
        out_shape=jax.ShapeDtypeStruct((N, D), F32),
        compiler_params=_cparams(("arbitrary",)),
        name="expert_combine",
    )(slot_of.reshape(-1), y_packed, x, shared, w_tok, mod)


def _split_w_in(w):
    D = w.shape[0]
    z = lambda n: jnp.zeros((D, n), w.dtype)
    w_r = jnp.concatenate([w[:, OFF_D:OFF_D + 2 * HW4], w[:, OFF_B:OFF_B + 2048], w[:, OFF_C:OFF_C + 1280],
                           z(R_COLS - R_USED)], axis=1)
    w_x = w[:, OFF_D + D_GROUP_COLS:OFF_D + 3 * D_GROUP_COLS]
    w_v = jnp.concatenate([w[:, OFF_D + 2 * HW4:OFF_D + 3 * HW4], w[:, OFF_B + 2048:OFF_B + 3072],
                           w[:, OFF_C + 1280:OFF_C + 1536], z(V_COLS - V_USED)], axis=1)
    w_a = jnp.concatenate([w[:, :A_COLS], z(A_PAD - A_COLS)], axis=1)
    w_g = w[:, OFF_G:]
    return tuple(t.astype(BF16) for t in (w_r, w_x, w_v, w_a, w_g))


def _mla_weights(w_uq, w_ukv):
    qd = MLA_NOPE_DIM + MLA_ROPE_DIM
    wq = w_uq.reshape(MLA_Q_RANK, MLA_HEADS, qd)
    wq = jnp.concatenate([wq, jnp.zeros((MLA_Q_RANK, MLA_HEADS, MLA_QW - qd), wq.dtype)], axis=-1)
    wkv = w_ukv.reshape(MLA_KV_RANK, MLA_HEADS, MLA_NOPE_DIM + MLA_V_DIM)
    wkv = jnp.concatenate([wkv[:, :, :MLA_NOPE_DIM].reshape(MLA_KV_RANK, -1),
                           wkv[:, :, MLA_NOPE_DIM:].reshape(MLA_KV_RANK, -1)], axis=-1)
    return wq.reshape(MLA_Q_RANK, MLA_HEADS * MLA_QW).astype(BF16), wkv.astype(BF16)


def kernel(x, c, positions, w_ada, b_ada, mix_norm_g, ffn_norm_g, w_in, mla_q_norm_g, mla_w_uq,
           mla_kv_norm_g, mla_w_ukv, diff_lambda, diff_subln_g, swa_sink, w_branch, w_out,
           router_w, router_bias, expert_w_gate, expert_w_up, expert_w_down,
           shared_w_gate, shared_w_up, shared_w_down, final_norm_g):
    B, S, D = x.shape
    N = B * S
    xf = x.reshape(N, D)
    mod_all = _ada_mod(c, w_ada, b_ada)
    tabs = _rope_tables(positions)
    c64, s64 = tabs[2], tabs[3]

    for l in range(DEPTH):
        mod = mod_all[l, :B].reshape(B, 6, D)

        h = _norm_mod(xf, mix_norm_g[l], mod, 0, 1, S)
        w_r, w_x, w_v, w_a, w_g = _split_w_in(w_in[l])
        proj_r = _matmul(h, w_r, tn=512, name="proj_rotary")
        proj_x = _matmul(h, w_x, tn=1024, name="proj_dilated")
        proj_v = _matmul(h, w_v, tn=1024, name="proj_value")
        proj_a = _matmul(h, w_a, tn=A_PAD, name="proj_latent")
        gates = _matmul(h, w_g, tn=1024, name="proj_gates")
        rop = _rope_section(proj_r, tabs)
        dil_groups = _dil_rope_section(proj_x, tabs[0], tabs[1], B, S, tuple(d for _, d in DIL_PAIRS[1:]))

        wq, wkv = _mla_weights(mla_w_uq[l], mla_w_ukv[l])
        qm, kn, kr, vm = _mla_prep(proj_a, mla_q_norm_g[l], mla_kv_norm_g[l], wq, wkv, c64, s64)
        out_a = _mla_attention(qm, kn, kr, vm, B, S)

        lam_init = 0.8 - 0.6 * math.exp(-0.3 * l)
        out_b = _diff_attention(rop, proj_v, diff_lambda[l], diff_subln_g[l], lam_init, B, S)

        sink = jnp.repeat(swa_sink[l].astype(F32), LANES).reshape(1, SWA_HEADS * LANES)
        (out_c,) = _banded_attention(
            rop, rop, proj_v, sub_len=S, half_window=SWA_HALF_WINDOW,
            q_col=R_SWA_Q // HW4, k_col=R_SWA_K // LANES, v_col=V_SWA // LANES,
            n_col_groups=SWA_KV_HEADS, kv_width=LANES, grp=SWA_HEADS // SWA_KV_HEADS, sink=sink)

        window0, _ = DIL_PAIRS[0]
        d_res = list(_banded_attention(
            rop, rop, proj_v, sub_len=S, half_window=window0 // 2, q_col=R_DIL0_Q // HW4,
            k_col=R_DIL0_K // HW4, v_col=V_DIL0 // HW4, n_col_groups=1, kv_width=HW4, grp=1, want_lse=True))
        for (window, dil), (qg, kg, vg) in zip(DIL_PAIRS[1:], dil_groups):
            flat = lambda a: a.reshape(N, HW4)
            d_res += _banded_attention(
                flat(qg), flat(kg), flat(vg), sub_len=S // dil, half_window=window // (2 * dil),
                q_col=0, k_col=0, v_col=0, n_col_groups=1, kv_width=HW4, grp=1, want_lse=True)
        out_d = _dil_combine(*d_res, B, S, tuple(d for _, d in DIL_PAIRS[1:]))

        merged = _merge((out_a, out_b, out_c, out_d), w_branch, gates, l)
        xf = _matmul_residual(merged, w_out, xf, mod, 2, S, layer=l)

        hb, hp, top_idx, top_w = _router(xf, ffn_norm_g[l], mod, 3, 4, router_w[l], router_bias[l], S)
        block_expert, block_off, tok_pad, slot_of, n_used = _dispatch_plan(top_idx)
        y = _expert_ffn(hp, expert_w_gate, expert_w_up, expert_w_down, l, block_expert, block_off,
                        tok_pad, n_used)
        act = _matmul_glu(hb, shared_w_gate, shared_w_up, layer=l)
        shared = _matmul(act, shared_w_down, tn=1024, name="shared_down", layer=l)
        xf = _combine(y, slot_of, xf, shared, top_w.T, mod, 5, S)

    return _final_norm(xf, final_norm_g).reshape(B, S, D)
```

```python
import functools
import math

import numpy as np
import jax
import jax.numpy as jnp
from jax import lax
from jax.experimental import pallas as pl
from jax.experimental.pallas import tpu as pltpu

F32 = jnp.float32
BF16 = jnp.bfloat16

D_MODEL = 4096
DEPTH = 2
HEAD_DIM = 128
ROPE_THETA = 10000.0
NORM_EPS = 1e-6

MLA_HEADS = 8
MLA_Q_RANK = 768
MLA_KV_RANK = 256
MLA_NOPE_DIM = 128
MLA_ROPE_DIM = 64
MLA_V_DIM = 128

DIFF_HEADS = 8
DIFF_QK_DIM = 64
DIFF_V_DIM = 128
DIFF_NORM_EPS = 1e-5

SWA_HEADS = 8
SWA_KV_HEADS = 2
SWA_HALF_WINDOW = 128

DIL_PAIRS = ((128, 1), (512, 4), (2048, 16))
DIL_HEADS = 4

N_EXPERTS = 64
N_EXPERT_GROUPS = 8
TOP_GROUPS = 4
TOP_K = 8
EXPERT_FF = 256
SHARED_FF = 1024
ROUTED_SCALE = 2.5

A_COLS = MLA_Q_RANK + MLA_KV_RANK + MLA_ROPE_DIM
B_COLS = 3 * DIFF_HEADS * DIFF_V_DIM
C_COLS = (SWA_HEADS + 2 * SWA_KV_HEADS) * HEAD_DIM
D_GROUP_COLS = 3 * DIL_HEADS * HEAD_DIM
D_COLS = len(DIL_PAIRS) * D_GROUP_COLS
OFF_B = A_COLS
OFF_C = OFF_B + B_COLS
OFF_D = OFF_C + C_COLS
OFF_G = OFF_D + D_COLS

LANES = 128
HW4 = DIL_HEADS * HEAD_DIM

R_DIL0_Q = 0
R_DIL0_K = HW4
R_DIFF_Q = 2 * HW4
R_DIFF_K = R_DIFF_Q + 1024
R_SWA_Q = R_DIFF_K + 1024
R_IN_COLS = R_SWA_Q + 1024
R_SWA_K = R_IN_COLS
R_COLS = R_SWA_K + 256
X_GROUP = 3 * HW4
X_COLS = 2 * X_GROUP
V_DIL0 = 0
V_DIFF = HW4
V_SWA_K = V_DIFF + 1024
V_SWA = V_SWA_K + 256
V_COLS = 2048
A_PAD = 1152
PROJ_TILE = 512

R_TILE_COLS = ([OFF_D, OFF_D + 512] + [OFF_B + 512 * t for t in range(4)] + [OFF_C, OFF_C + 512])
X_TILE_COLS = [OFF_D + D_GROUP_COLS + 512 * t for t in range(6)]
V_TILE_COLS = [OFF_D + 2 * HW4, OFF_B + 2048, OFF_B + 2560, OFF_C + 1024]

EXPERT_BLOCK_ROWS = 256
COMBINE_TOKENS = 32
PACK_HALF = D_MODEL // 2
PACK_SLABS = PACK_HALF // LANES
VMEM_LIMIT = 56 * 1024 * 1024


def _cparams(sem, vmem=VMEM_LIMIT):
    return pltpu.CompilerParams(dimension_semantics=sem, vmem_limit_bytes=vmem)


def _nt_dot(a, b):
    return lax.dot_general(a, b, (((1,), (1,)), ((), ())), preferred_element_type=F32)


ADA_ROWS = 128
ADA_STREAMS = 8


def _ada_kernel(ct_ref, *refs, batch):
    w_refs, (b_ref, o_ref) = refs[:ADA_STREAMS], refs[ADA_STREAMS:]
    k = pl.program_id(1)
    ct = ct_ref[...]
    s = ct * jax.nn.sigmoid(ct)
    sub = ADA_ROWS // ADA_STREAMS
    rows = []
    for b in range(batch):
        acc = None
        for i, w_ref in enumerate(w_refs):
            t = jnp.sum(w_ref[0] * s[i * sub:(i + 1) * sub, b:b + 1], axis=0, keepdims=True)
            acc = t if acc is None else acc + t
        rows.append(acc)
    rows += [jnp.zeros_like(rows[0])] * (8 - batch)
    part = jnp.concatenate(rows, axis=0)

    @pl.when(k == 0)
    def _():
        o_ref[0] = part + b_ref[0]

    @pl.when(k > 0)
    def _():
        o_ref[0] += part


def _ada_mod(c, w_ada, b_ada):
    B, D = c.shape
    ct = jnp.zeros((D, 8), F32).at[:, :B].set(c.T)
    n6 = w_ada.shape[-1]
    sub = ADA_ROWS // ADA_STREAMS
    w_specs = [pl.BlockSpec((1, sub, n6), lambda l, k, i=i: (l, k * ADA_STREAMS + i, 0))
               for i in range(ADA_STREAMS)]
    return pl.pallas_call(
        functools.partial(_ada_kernel, batch=B),
        grid=(DEPTH, D // ADA_ROWS),
        in_specs=[pl.BlockSpec((ADA_ROWS, 8), lambda l, k: (k, 0))] + w_specs
        + [pl.BlockSpec((1, 1, n6), lambda l, k: (l, 0, 0))],
        out_specs=pl.BlockSpec((1, 8, n6), lambda l, k: (l, 0, 0)),
        out_shape=jax.ShapeDtypeStruct((DEPTH, 8, n6), F32),
        compiler_params=_cparams(("arbitrary", "arbitrary")),
        name="ada_mod",
    )(ct, *([w_ada] * ADA_STREAMS), b_ada.reshape(DEPTH, 1, n6))


def _norm_mod_kernel(x_ref, g_ref, mod_ref, o_ref, *, shift_row, scale_row):
    x = x_ref[...]
    ms = jnp.mean(x * x, axis=-1, keepdims=True)
    y = x * lax.rsqrt(ms + NORM_EPS) * g_ref[...]
    h = y * (1.0 + mod_ref[0, scale_row:scale_row + 1, :]) + mod_ref[0, shift_row:shift_row + 1, :]
    o_ref[...] = h.astype(o_ref.dtype)


def _norm_mod(x, g, mod, shift_row, scale_row, seq, tm=256):
    N, D = x.shape
    tm = min(tm, seq)
    per_b = seq // tm
    return pl.pallas_call(
        functools.partial(_norm_mod_kernel, shift_row=shift_row, scale_row=scale_row),
        grid=(N // tm,),
        in_specs=[
            pl.BlockSpec((tm, D), lambda i: (i, 0)),
            pl.BlockSpec((1, D), lambda i: (0, 0)),
            pl.BlockSpec((1, 6, D), lambda i: (i // per_b, 0, 0)),
        ],
        out_specs=pl.BlockSpec((tm, D), lambda i: (i, 0)),
        out_shape=jax.ShapeDtypeStruct((N, D), BF16),
        compiler_params=_cparams(("arbitrary",)),
        name="norm_mod",
    )(x, g.reshape(1, D), mod)


def _final_norm_kernel(x_ref, g_ref, o_ref):
    x = x_ref[...]
    ms = jnp.mean(x * x, axis=-1, keepdims=True)
    o_ref[...] = x * lax.rsqrt(ms + NORM_EPS) * g_ref[...]


def _final_norm(x, g, tm=256):
    N, D = x.shape
    tm = min(tm, N)
    return pl.pallas_call(
        _final_norm_kernel,
        grid=(N // tm,),
        in_specs=[pl.BlockSpec((tm, D), lambda i: (i, 0)), pl.BlockSpec((1, D), lambda i: (0, 0))],
        out_specs=pl.BlockSpec((tm, D), lambda i: (i, 0)),
        out_shape=jax.ShapeDtypeStruct((N, D), F32),
        compiler_params=_cparams(("arbitrary",)),
        name="final_norm",
    )(x, g.reshape(1, D))


def _weight_scratch(*weights_and_tiles):
    return [pltpu.VMEM(tile, BF16) for w, tile in weights_and_tiles if w.dtype != BF16]


def _weight_spec(w, layer, rows, tn, row_block=0):
    if w.ndim == 2:
        return pl.BlockSpec((rows, tn), lambda j, i: (row_block, j))
    return pl.BlockSpec((None, rows, tn), lambda j, i: (layer, row_block, j))


def _resident_bf16(b_ref, scratch):
    if b_ref.dtype == BF16:
        return b_ref[...]
    s_ref = scratch.pop(0)

    @pl.when(pl.program_id(1) == 0)
    def _():
        s_ref[...] = b_ref[...].astype(BF16)

    return s_ref[...]


def _mm_kernel(a_ref, b_ref, o_ref, *scratch):
    b = _resident_bf16(b_ref, list(scratch))
    o_ref[...] = jnp.dot(a_ref[...], b, preferred_element_type=F32).astype(o_ref.dtype)


def _matmul(a, b, tn, out_dtype=BF16, tm=1024, name="matmul", layer=0, n_cols=None):
    M, K = a.shape
    Nc = b.shape[-1] if n_cols is None else n_cols
    tm = min(tm, M)
    return pl.pallas_call(
        _mm_kernel,
        grid=(Nc // tn, M // tm),
        in_specs=[pl.BlockSpec((tm, K), lambda j, i: (i, 0)), _weight_spec(b, layer, K, tn)],
        out_specs=pl.BlockSpec((tm, tn), lambda j, i: (i, j)),
        out_shape=jax.ShapeDtypeStruct((M, Nc), out_dtype),
        scratch_shapes=_weight_scratch((b, (K, tn))),
        compiler_params=_cparams(("arbitrary", "arbitrary")),
        name=name,
    )(a, b)


W_IN_MISALIGN = OFF_B % LANES


def _mm_window_kernel(starts_ref, a_ref, b_ref, o_ref, scr, *, tn):
    @pl.when(pl.program_id(1) == 0)
    def _():
        scr[...] = b_ref[:, W_IN_MISALIGN:W_IN_MISALIGN + tn].astype(BF16)

    o_ref[...] = jnp.dot(a_ref[...], scr[...], preferred_element_type=F32).astype(o_ref.dtype)


def _matmul_w_in(a, w_in, layer, tile_cols, name, tm=1024):
    M, K = a.shape
    tn = PROJ_TILE
    tm = min(tm, M)
    assert all(c % LANES == W_IN_MISALIGN for c in tile_cols)
    starts = jnp.asarray([c - W_IN_MISALIGN for c in tile_cols], jnp.int32)
    grid_spec = pltpu.PrefetchScalarGridSpec(
        num_scalar_prefetch=1,
        grid=(len(tile_cols), M // tm),
        in_specs=[
            pl.BlockSpec((tm, K), lambda j, i, s: (i, 0)),
            pl.BlockSpec((None, pl.Element(K), pl.Element(tn + LANES)),
                         lambda j, i, s: (layer, 0, pl.multiple_of(s[j], LANES))),
        ],
        out_specs=pl.BlockSpec((tm, tn), lambda j, i, s: (i, j)),
        scratch_shapes=[pltpu.VMEM((K, tn), BF16)],
    )
    return pl.pallas_call(
        functools.partial(_mm_window_kernel, tn=tn),
        grid_spec=grid_spec,
        out_shape=jax.ShapeDtypeStruct((M, len(tile_cols) * tn), BF16),
        compiler_params=_cparams(("arbitrary", "arbitrary")),
        name=name,
    )(starts, a, w_in)


def _mm_glu_kernel(a_ref, bg_ref, bu_ref, o_ref, *scratch):
    scratch = list(scratch)
    bg = _resident_bf16(bg_ref, scratch)
    bu = _resident_bf16(bu_ref, scratch)
    a = a_ref[...]
    g = jnp.dot(a, bg, preferred_element_type=F32)
    u = jnp.dot(a, bu, preferred_element_type=F32)
    o_ref[...] = (g * jax.nn.sigmoid(g) * u).astype(o_ref.dtype)


def _matmul_glu(a, bg, bu, tn=256, tm=1024, layer=0):
    M, K = a.shape
    Nc = bg.shape[-1]
    tm = min(tm, M)
    return pl.pallas_call(
        _mm_glu_kernel,
        grid=(Nc // tn, M // tm),
        in_specs=[
            pl.BlockSpec((tm, K), lambda j, i: (i, 0)),
            _weight_spec(bg, layer, K, tn),
            _weight_spec(bu, layer, K, tn),
        ],
        out_specs=pl.BlockSpec((tm, tn), lambda j, i: (i, j)),
        out_shape=jax.ShapeDtypeStruct((M, Nc), BF16),
        scratch_shapes=_weight_scratch((bg, (K, tn)), (bu, (K, tn))),
        compiler_params=_cparams(("arbitrary", "arbitrary")),
        name="shared_glu",
    )(a, bg, bu)


def _mm_residual_kernel(a_ref, b_ref, x_ref, mod_ref, o_ref, *scratch, gate_row):
    b = _resident_bf16(b_ref, list(scratch))
    acc = jnp.dot(a_ref[...], b, preferred_element_type=F32)
    o_ref[...] = x_ref[...] + mod_ref[0, gate_row:gate_row + 1, :] * acc


def _matmul_residual(a, b, x, mod, gate_row, seq, tn=512, tm=1024, layer=0):
    M, K = a.shape
    Nc = b.shape[-1]
    tm = min(tm, seq)
    per_b = seq // tm
    return pl.pallas_call(
        functools.partial(_mm_residual_kernel, gate_row=gate_row),
        grid=(Nc // tn, M // tm),
        in_specs=[
            pl.BlockSpec((tm, K), lambda j, i: (i, 0)),
            _weight_spec(b, layer, K, tn),
            pl.BlockSpec((tm, tn), lambda j, i: (i, j)),
            pl.BlockSpec((1, 6, tn), lambda j, i: (i // per_b, 0, j)),
        ],
        out_specs=pl.BlockSpec((tm, tn), lambda j, i: (i, j)),
        out_shape=jax.ShapeDtypeStruct((M, Nc), F32),
        scratch_shapes=_weight_scratch((b, (K, tn))),
        compiler_params=_cparams(("arbitrary", "arbitrary")),
        name="out_proj_residual",
    )(a, b, x, mod)


def _rope_table_kernel(pos_ref, inv128_ref, inv64_ref, c128_ref, s128_ref, c64_ref, s64_ref):
    pos = pos_ref[...]
    lane = lax.broadcasted_iota(jnp.int32, pos.shape, 1)
    a = pos * inv128_ref[...]
    c128_ref[...] = jnp.cos(a)
    s128_ref[...] = jnp.where(lane < 64, -jnp.sin(a), jnp.sin(a))
    a = pos * inv64_ref[...]
    c64_ref[...] = jnp.cos(a)
    s64_ref[...] = jnp.where((lane & 63) < 32, -jnp.sin(a), jnp.sin(a))


def _rope_tables(positions, tm=512):
    N = positions.size
    tm = min(tm, N)
    pos = jnp.broadcast_to(positions.astype(F32).reshape(N, 1), (N, LANES))

    def inv(half):
        f = np.float32(ROPE_THETA) ** (-(np.arange(half, dtype=np.float32) / np.float32(half)))
        return jnp.asarray(np.tile(f.astype(np.float32), LANES // half).reshape(1, LANES))

    tab = jax.ShapeDtypeStruct((N, LANES), F32)
    row = pl.BlockSpec((tm, LANES), lambda i: (i, 0))
    one = pl.BlockSpec((1, LANES), lambda i: (0, 0))
    return pl.pallas_call(
        _rope_table_kernel,
        grid=(N // tm,),
        in_specs=[row, one, one],
        out_specs=[row, row, row, row],
        out_shape=[tab, tab, tab, tab],
        compiler_params=_cparams(("arbitrary",)),
        name="rope_tables",
    )(pos, inv(64), inv(32))


def _rot128(x):
    return pltpu.roll(x, 64, 1)


def _rot64(x):
    lane = lax.broadcasted_iota(jnp.int32, x.shape, 1)
    return jnp.where((lane & 63) < 32, pltpu.roll(x, 96, 1), pltpu.roll(x, 32, 1))


def _r_block_plan():
    plan = [(128, HEAD_DIM ** -0.5)] * 4 + [(128, 1.0)] * 4
    plan += [(64, DIFF_QK_DIM ** -0.5)] * 8 + [(64, 1.0)] * 8
    plan += [(128, HEAD_DIM ** -0.5)] * 8 + [(128, 1.0)] * 2
    return plan


def _rope_kernel(p_ref, swak_ref, c128_ref, s128_ref, c64_ref, s64_ref, o_ref):
    c128, s128, c64, s64 = c128_ref[...], s128_ref[...], c64_ref[...], s64_ref[...]
    n_in = R_IN_COLS // LANES
    for blk, (flavour, scale) in enumerate(_r_block_plan()):
        if blk < n_in:
            x = p_ref[:, blk * LANES:(blk + 1) * LANES].astype(F32)
        else:
            x = swak_ref[:, (blk - n_in) * LANES:(blk - n_in + 1) * LANES].astype(F32)
        if flavour == 128:
            y = x * c128 + _rot128(x) * s128
        else:
            y = x * c64 + _rot64(x) * s64
        if scale != 1.0:
            y = y * scale
        o_ref[:, blk * LANES:(blk + 1) * LANES] = y.astype(o_ref.dtype)


def _rope_section(proj_r, proj_v, tabs, tm=256):
    N = proj_r.shape[0]
    tm = min(tm, N)
    tab = pl.BlockSpec((tm, LANES), lambda i: (i, 0))
    return pl.pallas_call(
        _rope_kernel,
        grid=(N // tm,),
        in_specs=[pl.BlockSpec((tm, R_IN_COLS), lambda i: (i, 0)),
                  pl.BlockSpec((tm, 256), lambda i: (i, V_SWA_K // 256)), tab, tab, tab, tab],
        out_specs=pl.BlockSpec((tm, R_COLS), lambda i: (i, 0)),
        out_shape=jax.ShapeDtypeStruct((N, R_COLS), BF16),
        compiler_params=_cparams(("arbitrary",)),
        name="rope_section",
    )(proj_r, proj_v, *tabs)


def _dil_rope_kernel(x_ref, c128_ref, s128_ref, *refs, dils, tm):
    scr = refs[-1]
    outs = refs[:-1]
    c128, s128 = c128_ref[...], s128_ref[...]
    for g, d in enumerate(dils):
        for part in range(3):
            o_ref = outs[g * 3 + part]
            base = (g * 3 + part) * DIL_HEADS
            for c in range(DIL_HEADS):
                col = g * X_GROUP + part * HW4 + c * LANES
                x = x_ref[:, col:col + LANES].astype(F32)
                if part < 2:
                    x = x * c128 + _rot128(x) * s128
                if part == 0:
                    x = x * (HEAD_DIM ** -0.5)
                scr[base + c] = x
            for r in range(d):
                for c in range(DIL_HEADS):
                    o_ref[r, :, c * LANES:(c + 1) * LANES] = (
                        scr[base + c, pl.ds(r, tm // d, stride=d), :].astype(o_ref.dtype))


def _dil_rope_section(proj_x, c128, s128, batch, seq, dils, tm=256):
    N = proj_x.shape[0]
    tm = min(tm, seq)
    per_b = seq // tm
    tab = pl.BlockSpec((tm, LANES), lambda i: (i, 0))
    out_specs, out_shape = [], []
    for d in dils:
        for _ in range(3):
            out_specs.append(pl.BlockSpec((d, tm // d, HW4), lambda i: (i // per_b, i % per_b, 0)))
            out_shape.append(jax.ShapeDtypeStruct((batch * d, seq // d, HW4), BF16))
    outs = pl.pallas_call(
        functools.partial(_dil_rope_kernel, dils=dils, tm=tm),
        grid=(N // tm,),
        in_specs=[pl.BlockSpec((tm, X_COLS), lambda i: (i, 0)), tab, tab],
        out_specs=out_specs,
        out_shape=out_shape,
        scratch_shapes=[pltpu.VMEM((len(dils) * 3 * DIL_HEADS, tm, LANES), F32)],
        compiler_params=_cparams(("arbitrary",)),
        name="dil_rope_section",
    )(proj_x, c128, s128)
    return [outs[3 * g:3 * g + 3] for g in range(len(dils))]


MLA_QW = 256


def _mla_prep_kernel(a_ref, gq_ref, gkv_ref, wq_ref, wkv_ref, c64_ref, s64_ref,
                     q_ref, kn_ref, kr_ref, v_ref):
    a = a_ref[...].astype(F32)
    c64, s64 = c64_ref[...], s64_ref[...]
    scale = (MLA_NOPE_DIM + MLA_ROPE_DIM) ** -0.5

    cq = a[:, :MLA_Q_RANK]
    cq = cq * lax.rsqrt(jnp.mean(cq * cq, axis=-1, keepdims=True) + NORM_EPS) * gq_ref[...]
    q = jnp.dot(cq.astype(BF16), wq_ref[...], preferred_element_type=F32)
    for h in range(MLA_HEADS):
        qn = q[:, h * MLA_QW:h * MLA_QW + LANES]
        qr = q[:, h * MLA_QW + LANES:(h + 1) * MLA_QW]
        qr = qr * c64 + _rot64(qr) * s64
        q_ref[:, h * MLA_QW:h * MLA_QW + LANES] = (qn * scale).astype(q_ref.dtype)
        q_ref[:, h * MLA_QW + LANES:(h + 1) * MLA_QW] = (qr * scale).astype(q_ref.dtype)

    ckv = a[:, MLA_Q_RANK:MLA_Q_RANK + MLA_KV_RANK]
    ckv = ckv * lax.rsqrt(jnp.mean(ckv * ckv, axis=-1, keepdims=True) + NORM_EPS) * gkv_ref[...]
    kv = jnp.dot(ckv.astype(BF16), wkv_ref[...], preferred_element_type=F32)
    kn_ref[...] = kv[:, :MLA_HEADS * MLA_NOPE_DIM].astype(kn_ref.dtype)
    v_ref[...] = kv[:, MLA_HEADS * MLA_NOPE_DIM:].astype(v_ref.dtype)

    kr = a[:, MLA_Q_RANK + MLA_KV_RANK:]
    lane = lax.broadcasted_iota(jnp.int32, kr.shape, 1)
    kr = jnp.where(lane < MLA_ROPE_DIM, kr, 0.0)
    kr_ref[...] = (kr * c64 + _rot64(kr) * s64).astype(kr_ref.dtype)


def _mla_prep(proj_a, gq, gkv, wq, wkv, c64, s64, tm=256):
    N = proj_a.shape[0]
    tm = min(tm, N)
    row = lambda w: pl.BlockSpec((tm, w), lambda i: (i, 0))
    full = lambda r, c: pl.BlockSpec((r, c), lambda i: (0, 0))
    hq = MLA_HEADS * MLA_QW
    hk = MLA_HEADS * MLA_NOPE_DIM
    return pl.pallas_call(
        _mla_prep_kernel,
        grid=(N // tm,),
        in_specs=[row(A_PAD), full(1, MLA_Q_RANK), full(1, MLA_KV_RANK), full(MLA_Q_RANK, hq),
                  full(MLA_KV_RANK, 2 * hk), row(LANES), row(LANES)],
        out_specs=[row(hq), row(hk), row(LANES), row(hk)],
        out_shape=[jax.ShapeDtypeStruct((N, hq), BF16), jax.ShapeDtypeStruct((N, hk), BF16),
                   jax.ShapeDtypeStruct((N, LANES), BF16), jax.ShapeDtypeStruct((N, hk), BF16)],
        compiler_params=_cparams(("arbitrary",)),
        name="mla_prep",
    )(proj_a, gq.reshape(1, -1), gkv.reshape(1, -1), wq, wkv, c64, s64)


def _softmax_pv(q, k_ref, v_ref, kc):
    tq = q.shape[0]
    S = k_ref.shape[0]
    m = jnp.full((tq, 1), -jnp.inf, F32)
    l = jnp.zeros((tq, 1), F32)
    acc = jnp.zeros((tq, v_ref.shape[1]), F32)
    for c in range(S // kc):
        s = _nt_dot(q, k_ref[c * kc:(c + 1) * kc, :])
        m_new = jnp.maximum(m, jnp.max(s, axis=-1, keepdims=True))
        alpha = jnp.exp(m - m_new)
        p = jnp.exp(s - m_new)
        l = alpha * l + jnp.sum(p, axis=-1, keepdims=True)
        acc = alpha * acc + jnp.dot(p.astype(BF16), v_ref[c * kc:(c + 1) * kc, :],
                                    preferred_element_type=F32)
        m = m_new
    return acc / l


def _mla_attn_kernel(q_ref, kn_ref, kr_ref, v_ref, o_ref, kcat_ref, *, kc):
    @pl.when(pl.program_id(2) == 0)
    def _():
        kcat_ref[:, :LANES] = kn_ref[...]
        kcat_ref[:, LANES:] = kr_ref[...]

    o_ref[...] = _softmax_pv(q_ref[...], kcat_ref, v_ref, kc).astype(o_ref.dtype)


def _mla_attention(qm, kn, kr, vm, batch, seq, tq=1024, kc=1024):
    N = qm.shape[0]
    tq = min(tq, seq)
    kc = min(kc, seq)
    nq = seq // tq
    return pl.pallas_call(
        functools.partial(_mla_attn_kernel, kc=kc),
        grid=(batch, MLA_HEADS, nq),
        in_specs=[
            pl.BlockSpec((tq, MLA_QW), lambda b, h, i: (b * nq + i, h)),
            pl.BlockSpec((seq, LANES), lambda b, h, i: (b, h)),
            pl.BlockSpec((seq, LANES), lambda b, h, i: (b, 0)),
            pl.BlockSpec((seq, LANES), lambda b, h, i: (b, h)),
        ],
        out_specs=pl.BlockSpec((tq, LANES), lambda b, h, i: (b * nq + i, h)),
        out_shape=jax.ShapeDtypeStruct((N, MLA_HEADS * MLA_V_DIM), BF16),
        scratch_shapes=[pltpu.VMEM((seq, MLA_QW), BF16)],
        compiler_params=_cparams(("arbitrary", "arbitrary", "arbitrary")),
        name="mla_attention",
    )(qm, kn, kr, vm)


def _diff_attn_kernel(q_ref, k_ref, v_ref, lam_ref, g_ref, o_ref, *, kc, lam_init):
    q = q_ref[...]
    lane = lax.broadcasted_iota(jnp.int32, q.shape, 1)
    zero = jnp.zeros_like(q)
    o0 = _softmax_pv(jnp.where(lane < DIFF_QK_DIM, q, zero), k_ref, v_ref, kc)
    o1 = _softmax_pv(jnp.where(lane >= DIFF_QK_DIM, q, zero), k_ref, v_ref, kc)
    lp = lam_ref[...]
    lam = (jnp.exp(jnp.sum(lp[0:1] * lp[1:2], axis=-1, keepdims=True))
           - jnp.exp(jnp.sum(lp[2:3] * lp[3:4], axis=-1, keepdims=True)) + lam_init)
    o = o0 - lam * o1
    o = o * lax.rsqrt(jnp.mean(o * o, axis=-1, keepdims=True) + DIFF_NORM_EPS) * g_ref[...]
    o_ref[...] = (o * (1.0 - lam_init)).astype(o_ref.dtype)


def _diff_attention(rop, proj_v, lam_params, subln_g, lam_init, batch, seq, tq=1024, kc=1024):
    N = rop.shape[0]
    tq = min(tq, seq)
    kc = min(kc, seq)
    nq = seq // tq
    qb, kb, vb = R_DIFF_Q // LANES, R_DIFF_K // LANES, V_DIFF // LANES
    return pl.pallas_call(
        functools.partial(_diff_attn_kernel, kc=kc, lam_init=lam_init),
        grid=(batch, DIFF_HEADS, nq),
        in_specs=[
            pl.BlockSpec((tq, LANES), lambda b, h, i: (b * nq + i, qb + h)),
            pl.BlockSpec((seq, LANES), lambda b, h, i: (b, kb + h)),
            pl.BlockSpec((seq, LANES), lambda b, h, i: (b, vb + h)),
            pl.BlockSpec((4, DIFF_QK_DIM), lambda b, h, i: (0, 0)),
            pl.BlockSpec((1, DIFF_V_DIM), lambda b, h, i: (0, 0)),
        ],
        out_specs=pl.BlockSpec((tq, LANES), lambda b, h, i: (b * nq + i, h)),
        out_shape=jax.ShapeDtypeStruct((N, DIFF_HEADS * DIFF_V_DIM), BF16),
        compiler_params=_cparams(("arbitrary", "arbitrary", "arbitrary")),
        name="diff_attention",
    )(rop, rop, proj_v, lam_params, subln_g.reshape(1, -1))


def _banded_kernel(*refs, half_window, blk, blocks_per_seq, grp, use_sink, want_lse):
    q_ref, kp_ref, kc_ref, kn_ref, vp_ref, vc_ref, vn_ref = refs[:7]
    rest = list(refs[7:])
    sink_ref = rest.pop(0) if use_sink else None
    o_ref = rest.pop(0)
    lse_ref = rest.pop(0) if want_lse else None

    il = lax.rem(pl.program_id(0), blocks_per_seq)
    qpos = il * blk + lax.broadcasted_iota(jnp.int32, (blk, 3 * blk), 0)
    kpos = (il - 1) * blk + lax.broadcasted_iota(jnp.int32, (blk, 3 * blk), 1)
    valid = ((jnp.abs(qpos - kpos) <= half_window) & (kpos >= 0) & (kpos < blocks_per_seq * blk))

    heads = range(DIL_HEADS)
    cq = [slice(h * LANES, (h + 1) * LANES) for h in heads]
    ck = [slice((h // grp) * LANES, (h // grp + 1) * LANES) for h in heads]
    scores = []
    for h in heads:
        k = jnp.concatenate([kp_ref[:, ck[h]], kc_ref[:, ck[h]], kn_ref[:, ck[h]]], axis=0)
        scores.append(jnp.where(valid, _nt_dot(q_ref[:, cq[h]], k), -jnp.inf))
    probs, dens, maxes = [], [], []
    for h in heads:
        m = jnp.max(scores[h], axis=-1, keepdims=True)
        if use_sink:
            sk = jnp.max(jnp.broadcast_to(sink_ref[0:1, cq[h]], (blk, LANES)), axis=-1, keepdims=True)
            m = jnp.maximum(m, sk)
        p = jnp.exp(scores[h] - m)
        den = jnp.sum(p, axis=-1, keepdims=True)
        if use_sink:
            den = den + jnp.exp(sk - m)
        probs.append(p.astype(BF16))
        dens.append(den)
        maxes.append(m)
    for h in heads:
        v = jnp.concatenate([vp_ref[:, ck[h]], vc_ref[:, ck[h]], vn_ref[:, ck[h]]], axis=0)
        o = jnp.dot(probs[h], v, preferred_element_type=F32) / dens[h]
        o_ref[:, cq[h]] = o.astype(o_ref.dtype)
        if want_lse:
            lse_ref[:, cq[h]] = jnp.broadcast_to(maxes[h] + jnp.log(dens[h]), (blk, LANES))


def _banded_attention(q_arr, k_arr, v_arr, *, sub_len, half_window, q_col, k_col, v_col,
                      n_col_groups, kv_width, grp, sink=None, want_lse=False):
    rows = q_arr.shape[0]
    blk = min(128, sub_len)
    blocks_per_seq = sub_len // blk
    n_row_blocks = rows // blk
    last = n_row_blocks - 1

    def prev(i):
        return jnp.maximum(i - 1, 0)

    def nxt(i):
        return jnp.minimum(i + 1, last)

    qspec = pl.BlockSpec((blk, HW4), lambda i, c: (i, q_col + c))
    kspecs = [pl.BlockSpec((blk, kv_width), lambda i, c, f=f: (f(i), k_col + c))
              for f in (prev, lambda i: i, nxt)]
    vspecs = [pl.BlockSpec((blk, kv_width), lambda i, c, f=f: (f(i), v_col + c))
              for f in (prev, lambda i: i, nxt)]
    in_specs = [qspec] + kspecs + vspecs
    args = [q_arr, k_arr, k_arr, k_arr, v_arr, v_arr, v_arr]
    if sink is not None:
        in_specs.append(pl.BlockSpec((1, HW4), lambda i, c: (0, c)))
        args.append(sink)
    out_cols = n_col_groups * HW4
    ospec = pl.BlockSpec((blk, HW4), lambda i, c: (i, c))
    out_specs = [ospec]
    out_shape = [jax.ShapeDtypeStruct((rows, out_cols), BF16)]
    if want_lse:
        out_specs.append(ospec)
        out_shape.append(jax.ShapeDtypeStruct((rows, out_cols), F32))
    return pl.pallas_call(
        functools.partial(_banded_kernel, half_window=half_window, blk=blk,
                          blocks_per_seq=blocks_per_seq, grp=grp, use_sink=sink is not None,
                          want_lse=want_lse),
        grid=(n_row_blocks, n_col_groups),
        in_specs=in_specs,
        out_specs=out_specs,
        out_shape=out_shape,
        compiler_params=_cparams(("arbitrary", "arbitrary")),
        name="banded_attention",
    )(*args)


def _dil_combine_kernel(o0, l0, o1, l1, o2, l2, out_ref, scr, *, dils, tm):
    def natural(o_ref, l_ref, d, base):
        for r in range(d):
            for c in range(DIL_HEADS):
                cols = slice(c * LANES, (c + 1) * LANES)
                scr[base + c, pl.ds(r, tm // d, stride=d), :] = o_ref[r, :, cols].astype(F32)
                scr[base + DIL_HEADS + c, pl.ds(r, tm // d, stride=d), :] = l_ref[r, :, cols]

    natural(o1, l1, dils[0], 0)
    natural(o2, l2, dils[1], 2 * DIL_HEADS)
    for c in range(DIL_HEADS):
        cols = slice(c * LANES, (c + 1) * LANES)
        oa, la = o0[:, cols].astype(F32), l0[:, cols]
        ob, lb = scr[c], scr[DIL_HEADS + c]
        oc, lc = scr[2 * DIL_HEADS + c], scr[3 * DIL_HEADS + c]
        m = jnp.maximum(jnp.maximum(la, lb), lc)
        ea, eb, ec = jnp.exp(la - m), jnp.exp(lb - m), jnp.exp(lc - m)
        inv = 1.0 / (ea + eb + ec)
        out_ref[:, cols] = ((ea * inv) * oa + (eb * inv) * ob + (ec * inv) * oc).astype(out_ref.dtype)


def _dil_combine(o0, l0, o1, l1, o2, l2, batch, seq, dils, tm=256):
    N = o0.shape[0]
    tm = min(tm, seq)
    per_b = seq // tm
    row = pl.BlockSpec((tm, HW4), lambda i: (i, 0))
    grouped = [pl.BlockSpec((d, tm // d, HW4), lambda i: (i // per_b, i % per_b, 0)) for d in dils]
    shaped = lambda a, d: a.reshape(batch * d, seq // d, HW4)
    return pl.pallas_call(
        functools.partial(_dil_combine_kernel, dils=dils, tm=tm),
        grid=(N // tm,),
        in_specs=[row, row, grouped[0], grouped[0], grouped[1], grouped[1]],
        out_specs=row,
        out_shape=jax.ShapeDtypeStruct((N, HW4), BF16),
        scratch_shapes=[pltpu.VMEM((4 * DIL_HEADS, tm, LANES), F32)],
        compiler_params=_cparams(("arbitrary",)),
        name="dil_combine",
    )(o0, l0, shaped(o1, dils[0]), shaped(l1, dils[0]), shaped(o2, dils[1]), shaped(l2, dils[1]))


def _merge_kernel(oa, ob, oc, od, wa, wb, wc, wd, ga, gb, gc, gd, o_ref, *scratch):
    scratch = list(scratch)
    acc = None
    for o, w, g in ((oa, wa, ga), (ob, wb, gb), (oc, wc, gc), (od, wd, gd)):
        wt = _resident_bf16(w, scratch)
        t = jax.nn.sigmoid(g[...].astype(F32)) * jnp.dot(o[...], wt, preferred_element_type=F32)
        acc = t if acc is None else acc + t
    o_ref[...] = acc.astype(o_ref.dtype)


def _merge(branch_outs, w_branch, gates, layer, tn=512, tm=1024):
    N = gates.shape[0]
    D = D_MODEL
    tm = min(tm, N)
    oa, ob, oc, od = branch_outs
    in_specs = [pl.BlockSpec((tm, o.shape[1]), lambda j, i: (i, 0)) for o in branch_outs]
    rows = [o.shape[1] for o in branch_outs]
    row_blocks = [0, 1, 2, 3072 // rows[3]]
    in_specs += [_weight_spec(w_branch, layer, rows[r], tn, row_blocks[r]) for r in range(4)]
    nj = D // tn
    in_specs += [pl.BlockSpec((tm, tn), lambda j, i, r=r: (i, r * nj + j)) for r in range(4)]
    return pl.pallas_call(
        _merge_kernel,
        grid=(nj, N // tm),
        in_specs=in_specs,
        out_specs=pl.BlockSpec((tm, tn), lambda j, i: (i, j)),
        out_shape=jax.ShapeDtypeStruct((N, D), BF16),
        scratch_shapes=_weight_scratch(*[(w_branch, (rows[r], tn)) for r in range(4)]),
        compiler_params=_cparams(("arbitrary", "arbitrary")),
        name="branch_merge",
    )(oa, ob, oc, od, w_branch, w_branch, w_branch, w_branch, gates, gates, gates, gates)


VMEM_PITCH = 20


def _pack_pairs(lo, hi):
    lo_b = lax.bitcast_convert_type(lo.astype(BF16).astype(F32), jnp.uint32)
    hi_b = lax.bitcast_convert_type(hi.astype(BF16).astype(F32), jnp.uint32)
    return (lo_b >> 16) | (hi_b & jnp.uint32(0xFFFF0000))


def _unpack_pairs(w):
    lo = lax.bitcast_convert_type(w << 16, F32)
    hi = lax.bitcast_convert_type(w & jnp.uint32(0xFFFF0000), F32)
    return lo, hi


def _store_packed(o_ref, x):
    rows = x.shape[0]
    for j in range(PACK_SLABS):
        o_ref[pl.ds(j, rows, stride=PACK_SLABS), :] = _pack_pairs(
            x[:, j * LANES:(j + 1) * LANES], x[:, PACK_HALF + j * LANES:PACK_HALF + (j + 1) * LANES])


def _load_slab(buf_ref, first_row, rows, j):
    return buf_ref[pl.ds(first_row + j, rows, stride=VMEM_PITCH), :]


def _router_kernel(x_ref, g_ref, mod_ref, rw_ref, rb_ref, hb_ref, hp_ref, idx_ref, wt_ref, *,
                   shift_row, scale_row):
    x = x_ref[...]
    ms = jnp.mean(x * x, axis=-1, keepdims=True)
    y = x * lax.rsqrt(ms + NORM_EPS) * g_ref[...]
    h = y * (1.0 + mod_ref[0, scale_row:scale_row + 1, :]) + mod_ref[0, shift_row:shift_row + 1, :]
    hb_ref[...] = h.astype(hb_ref.dtype)
    _store_packed(hp_ref, h)

    tm = x.shape[0]
    logits = jnp.dot(h, rw_ref[...], preferred_element_type=F32, precision=lax.Precision.HIGHEST)
    lt = logits.T[:N_EXPERTS, :]
    scores = jax.nn.sigmoid(lt)
    choice = scores + rb_ref[...]

    per_group = N_EXPERTS // N_EXPERT_GROUPS
    sub = lax.broadcasted_iota(jnp.int32, (per_group, tm), 0)
    group_score = []
    for g in range(N_EXPERT_GROUPS):
        cg = choice[g * per_group:(g + 1) * per_group, :]
        m1 = jnp.max(cg, axis=0, keepdims=True)
        first = jnp.min(jnp.where(cg == m1, sub, per_group), axis=0, keepdims=True)
        m2 = jnp.max(jnp.where(sub == first, -jnp.inf, cg), axis=0, keepdims=True)
        group_score.append(m1 + m2)
    masked = []
    for g in range(N_EXPERT_GROUPS):
        ahead = jnp.zeros((1, tm), jnp.int32)
        for o in range(N_EXPERT_GROUPS):
            if o == g:
                continue
            better = (group_score[o] >= group_score[g]) if o < g else (group_score[o] > group_score[g])
            ahead = ahead + better.astype(jnp.int32)
        cg = choice[g * per_group:(g + 1) * per_group, :]
        masked.append(jnp.where(ahead < TOP_GROUPS, cg, -jnp.inf))
    cm = jnp.concatenate(masked, axis=0)

    eidx = lax.broadcasted_iota(jnp.int32, (N_EXPERTS, tm), 0)
    rank = jnp.zeros((N_EXPERTS, tm), jnp.int32)
    for e in range(N_EXPERTS):
        row = cm[e:e + 1, :]
        tie = jnp.where(eidx > e, 1, 0)
        rank = rank + jnp.where(row > cm, 1, jnp.where(row == cm, tie, 0))
    sel = rank < TOP_K
    wsel = jnp.where(sel, scores, 0.0)
    wsel = wsel / jnp.sum(wsel, axis=0, keepdims=True) * ROUTED_SCALE
    idx_rows, wt_rows = [], []
    for r in range(TOP_K):
        hit = rank == r
        idx_rows.append(jnp.sum(jnp.where(hit, eidx, 0), axis=0, keepdims=True))
        wt_rows.append(jnp.sum(jnp.where(hit, wsel, 0.0), axis=0, keepdims=True))
    idx_ref[...] = jnp.concatenate(idx_rows, axis=0)
    wt_ref[...] = jnp.concatenate(wt_rows, axis=0)


def _router(x, g, mod, shift_row, scale_row, router_w, router_bias, seq, tm=256):
    N, D = x.shape
    tm = min(tm, seq)
    per_b = seq // tm
    rw = jnp.zeros((D, LANES), F32).at[:, :N_EXPERTS].set(router_w)
    row = pl.BlockSpec((tm, D), lambda i: (i, 0))
    col = lambda r: pl.BlockSpec((r, tm), lambda i: (0, i))
    return pl.pallas_call(
        functools.partial(_router_kernel, shift_row=shift_row, scale_row=scale_row),
        grid=(N // tm,),
        in_specs=[
            row,
            pl.BlockSpec((1, D), lambda i: (0, 0)),
            pl.BlockSpec((1, 6, D), lambda i: (i // per_b, 0, 0)),
            pl.BlockSpec((D, LANES), lambda i: (0, 0)),
            pl.BlockSpec((N_EXPERTS, 1), lambda i: (0, 0)),
        ],
        out_specs=[row, pl.BlockSpec((tm * PACK_SLABS, LANES), lambda i: (i, 0)), col(TOP_K), col(TOP_K)],
        out_shape=[jax.ShapeDtypeStruct((N, D), BF16), jax.ShapeDtypeStruct((N * PACK_SLABS, LANES), jnp.uint32),
                   jax.ShapeDtypeStruct((TOP_K, N), jnp.int32), jax.ShapeDtypeStruct((TOP_K, N), F32)],
        compiler_params=_cparams(("arbitrary",)),
        name="ffn_norm_router",
    )(x, g.reshape(1, D), mod, rw, router_bias.reshape(N_EXPERTS, 1))


def _num_expert_blocks(n_tokens):
    bm = EXPERT_BLOCK_ROWS
    return -(-(n_tokens * TOP_K + N_EXPERTS * (bm - 1)) // bm)


def _dispatch_plan(top_idx):
    K, N = top_idx.shape
    bm = EXPERT_BLOCK_ROWS
    n_blocks = _num_expert_blocks(N)
    experts = jnp.arange(N_EXPERTS, dtype=jnp.int32)
    onehot = top_idx[:, None, :] == experts[None, :, None]
    mask = jnp.any(onehot, axis=0).astype(jnp.int32)
    counts = jnp.sum(mask, axis=1)
    pos = jnp.cumsum(mask, axis=1) - mask
    nblk = (counts + bm - 1) // bm
    blk_end = jnp.cumsum(nblk)
    blk_start = blk_end - nblk
    n_used = blk_end[-1]
    start = jnp.cumsum(counts) - counts
    slot_en = (blk_start * bm)[:, None] + pos
    slot_of = jnp.sum(jnp.where(onehot, slot_en[None], 0), axis=1)

    keys = top_idx * N + jnp.arange(N, dtype=jnp.int32)[None, :]
    tok_sorted = jnp.sort(keys.reshape(-1)) % N

    bidx = jnp.arange(n_blocks + 1, dtype=jnp.int32)
    be = jnp.minimum(jnp.sum((bidx[:, None] >= blk_end[None, :]).astype(jnp.int32), axis=1),
                     N_EXPERTS - 1)
    be = jnp.where(bidx < n_used, be, be[jnp.maximum(n_used - 1, 0)])
    off = jnp.clip(start[be] + (bidx - blk_start[be]) * bm, 0, K * N)
    tok_pad = jnp.concatenate([tok_sorted, jnp.zeros((bm,), jnp.int32)])
    run_end = blk_end[be]
    nxt = jnp.where(run_end < n_used, be[jnp.minimum(run_end, n_blocks)], -1)
    return (be.astype(jnp.int32), nxt.astype(jnp.int32), off.astype(jnp.int32), tok_pad.astype(jnp.int32),
            slot_of.astype(jnp.int32), n_used.astype(jnp.int32))


WEIGHT_CHUNKS = 4


def _expert_kernel(be_ref, nxt_ref, off_ref, tok_ref, nu_ref, h_hbm, wg_hbm, wu_hbm, wd_hbm, y_ref,
                   xbuf0, xbuf1, xs, stg_g, stg_u, stg_d, wg_s, wu_s, wd_s, sem, wsem, *, layer):
    bm = EXPERT_BLOCK_ROWS
    b = pl.program_id(0)
    n_used = nu_ref[0]
    bufs = (xbuf0, xbuf1)

    def weight_copies(e):
        copies = []
        for src, dst in ((wg_hbm, stg_g), (wu_hbm, stg_u), (wd_hbm, stg_d)):
            rows = dst.shape[0] // WEIGHT_CHUNKS
            for c in range(WEIGHT_CHUNKS):
                copies.append(pltpu.make_async_copy(src.at[layer, e, pl.ds(c * rows, rows), :],
                                                    dst.at[pl.ds(c * rows, rows), :], wsem.at[0]))
        return copies

    def start_gather(blk, slot, unrolled):
        base = off_ref[blk]

        def one(r, priority):
            src = pl.multiple_of(tok_ref[base + r] * PACK_SLABS, PACK_SLABS)
            pltpu.make_async_copy(h_hbm.at[pl.ds(src, PACK_SLABS), :],
                                  bufs[slot].at[pl.ds(r * VMEM_PITCH, PACK_SLABS), :],
                                  sem.at[slot]).start(priority=priority)

        if unrolled:
            for r in range(bm):
                one(r, r % 2)
        else:
            def body(i, carry):
                one(2 * i, 0)
                one(2 * i + 1, 1)
                return carry
            lax.fori_loop(0, bm // 2, body, 0)

    def wait_gather(slot):
        pltpu.make_async_copy(h_hbm.at[pl.ds(0, bm * PACK_SLABS), :],
                              bufs[slot].at[pl.ds(0, bm * PACK_SLABS), :], sem.at[slot]).wait()

    @pl.when(b == 0)
    def _():
        start_gather(0, 0, False)
        for cp in weight_copies(be_ref[0]):
            cp.start()

    new_expert = (b == 0) | (be_ref[b] != be_ref[jnp.maximum(b - 1, 0)])

    @pl.when(new_expert & (b < n_used))
    def _():
        for cp in weight_copies(be_ref[b]):
            cp.wait()
        wg_s[...] = stg_g[...].astype(BF16)
        wu_s[...] = stg_u[...].astype(BF16)
        wd_s[...] = stg_d[...].astype(BF16)

        @pl.when(nxt_ref[b] >= 0)
        def _():
            for cp in weight_copies(nxt_ref[b]):
                cp.start()

    def work(slot):
        wait_gather(slot)
        for j in range(PACK_SLABS):
            lo, hi = _unpack_pairs(_load_slab(bufs[slot], 0, bm, j))
            xs[:, j * LANES:(j + 1) * LANES] = lo.astype(BF16)
            xs[:, PACK_HALF + j * LANES:PACK_HALF + (j + 1) * LANES] = hi.astype(BF16)
        start_gather(b + 1, 1 - slot, True)
        x = xs[...]
        g = jnp.dot(x, wg_s[...], preferred_element_type=F32)
        u = jnp.dot(x, wu_s[...], preferred_element_type=F32)
        a = (g * jax.nn.sigmoid(g) * u).astype(BF16)
        _store_packed(y_ref, jnp.dot(a, wd_s[...], preferred_element_type=F32))

    for slot in (0, 1):
        @pl.when((b < n_used) & (b % 2 == slot))
        def _(slot=slot):
            work(slot)

        @pl.when((b == n_used) & (b % 2 == slot))
        def _(slot=slot):
            wait_gather(slot)

    @pl.when(b >= n_used)
    def _():
        y_ref[...] = jnp.zeros(y_ref.shape, y_ref.dtype)


def _expert_ffn(h_packed, w_gate, w_up, w_down, layer, block_expert, next_expert, block_off, tok_pad,
                n_used):
    bm = EXPERT_BLOCK_ROWS
    D = D_MODEL
    n_steps = block_expert.shape[0]
    any_spec = pl.BlockSpec(memory_space=pl.ANY)
    grid_spec = pltpu.PrefetchScalarGridSpec(
        num_scalar_prefetch=5,
        grid=(n_steps,),
        in_specs=[any_spec, any_spec, any_spec, any_spec],
        out_specs=pl.BlockSpec((bm * PACK_SLABS, LANES), lambda b, *_: (b, 0)),
        scratch_shapes=[pltpu.VMEM((bm * VMEM_PITCH, LANES), jnp.uint32),
                        pltpu.VMEM((bm * VMEM_PITCH, LANES), jnp.uint32),
                        pltpu.VMEM((bm, D), BF16),
                        pltpu.VMEM((D, EXPERT_FF), F32), pltpu.VMEM((D, EXPERT_FF), F32),
                        pltpu.VMEM((EXPERT_FF, D), F32),
                        pltpu.VMEM((D, EXPERT_FF), BF16), pltpu.VMEM((D, EXPERT_FF), BF16),
                        pltpu.VMEM((EXPERT_FF, D), BF16),
                        pltpu.SemaphoreType.DMA((2,)), pltpu.SemaphoreType.DMA((1,))],
    )
    return pl.pallas_call(
        functools.partial(_expert_kernel, layer=layer),
        grid_spec=grid_spec,
        out_shape=jax.ShapeDtypeStruct((n_steps * bm * PACK_SLABS, LANES), jnp.uint32),
        compiler_params=_cparams(("arbitrary",)),
        name="expert_ffn",
    )(block_expert, next_expert, block_off, tok_pad, n_used.reshape(1), h_packed, w_gate, w_up, w_down)


def _combine_kernel(slot_ref, y_hbm, x_ref, sh_ref, w_ref, mod_ref, o_ref, ybuf0, ybuf1, sem, *,
                    gate_row, n_tokens):
    tc = COMBINE_TOKENS
    i = pl.program_id(0)
    n_steps = pl.num_programs(0)
    bufs = (ybuf0, ybuf1)

    def start_gather(step, slot, unrolled):
        base = step * tc

        def one(t, k):
            src = pl.multiple_of(slot_ref[k * n_tokens + base + t] * PACK_SLABS, PACK_SLABS)
            pltpu.make_async_copy(y_hbm.at[pl.ds(src, PACK_SLABS), :],
                                  bufs[slot].at[pl.ds((k * tc + t) * VMEM_PITCH, PACK_SLABS), :],
                                  sem.at[slot]).start(priority=k % 2)

        if unrolled:
            for t in range(tc):
                for k in range(TOP_K):
                    one(t, k)
        else:
            def body(t, carry):
                for k in range(TOP_K):
                    one(t, k)
                return carry
            lax.fori_loop(0, tc, body, 0)

    @pl.when(i == 0)
    def _():
        start_gather(0, 0, False)

    def work(slot, prefetch):
        n_rows = TOP_K * tc * PACK_SLABS
        pltpu.make_async_copy(y_hbm.at[pl.ds(0, n_rows), :], bufs[slot].at[pl.ds(0, n_rows), :],
                              sem.at[slot]).wait()
        if prefetch:
            start_gather(i + 1, 1 - slot, True)
        w = w_ref[...]
        wk = [w[:, k:k + 1] for k in range(TOP_K)]
        for j in range(PACK_SLABS):
            c_lo = slice(j * LANES, (j + 1) * LANES)
            c_hi = slice(PACK_HALF + j * LANES, PACK_HALF + (j + 1) * LANES)
            acc_lo = sh_ref[:, c_lo].astype(F32)
            acc_hi = sh_ref[:, c_hi].astype(F32)
            for k in range(TOP_K):
                lo, hi = _unpack_pairs(_load_slab(bufs[slot], k * tc * VMEM_PITCH, tc, j))
                acc_lo = acc_lo + wk[k] * lo
                acc_hi = acc_hi + wk[k] * hi
            o_ref[:, c_lo] = x_ref[:, c_lo] + mod_ref[0, gate_row:gate_row + 1, c_lo] * acc_lo
            o_ref[:, c_hi] = x_ref[:, c_hi] + mod_ref[0, gate_row:gate_row + 1, c_hi] * acc_hi

    for slot in (0, 1):
        for prefetch in (True, False):
            @pl.when((i % 2 == slot) & ((i + 1 < n_steps) == prefetch))
            def _(slot=slot, prefetch=prefetch):
                work(slot, prefetch)


def _combine(y_packed, slot_of, x, shared, w_tok, mod, gate_row, seq):
    N, D = x.shape
    tc = COMBINE_TOKENS
    per_b = seq // tc
    grid_spec = pltpu.PrefetchScalarGridSpec(
        num_scalar_prefetch=1,
        grid=(N // tc,),
        in_specs=[
            pl.BlockSpec(memory_space=pl.ANY),
            pl.BlockSpec((tc, D), lambda i, s: (i, 0)),
            pl.BlockSpec((tc, D), lambda i, s: (i, 0)),
            pl.BlockSpec((tc, TOP_K), lambda i, s: (i, 0)),
            pl.BlockSpec((1, 6, D), lambda i, s: (i // per_b, 0, 0)),
        ],
        out_specs=pl.BlockSpec((tc, D), lambda i, s: (i, 0)),
        scratch_shapes=[pltpu.VMEM((TOP_K * tc * VMEM_PITCH, LANES), jnp.uint32),
                        pltpu.VMEM((TOP_K * tc * VMEM_PITCH, LANES), jnp.uint32),
                        pltpu.SemaphoreType.DMA((2,))],
    )
    return pl.pallas_call(
        functools.partial(_combine_kernel, gate_row=gate_row, n_tokens=N),
        grid_spec=grid_spec,
        out_shape=jax.ShapeDtypeStruct((N, D), F32),
        compiler_params=_cparams(("arbitrary",)),
        name="expert_combine",
    )(slot_of.reshape(-1), y_packed, x, shared, w_tok, mod)


def _mla_weights(w_uq, w_ukv):
    qd = MLA_NOPE_DIM + MLA_ROPE_DIM
    wq = w_uq.reshape(MLA_Q_RANK, MLA_HEADS, qd)
    wq = jnp.concatenate([wq, jnp.zeros((MLA_Q_RANK, MLA_HEADS, MLA_QW - qd), wq.dtype)], axis=-1)
    wkv = w_ukv.reshape(MLA_KV_RANK, MLA_HEADS, MLA_NOPE_DIM + MLA_V_DIM)
    wkv = jnp.concatenate([wkv[:, :, :MLA_NOPE_DIM].reshape(MLA_KV_RANK, -1),
                           wkv[:, :, MLA_NOPE_DIM:].reshape(MLA_KV_RANK, -1)], axis=-1)
    return wq.reshape(MLA_Q_RANK, MLA_HEADS * MLA_QW).astype(BF16), wkv.astype(BF16)


def kernel(x, c, positions, w_ada, b_ada, mix_norm_g, ffn_norm_g, w_in, mla_q_norm_g, mla_w_uq,
           mla_kv_norm_g, mla_w_ukv, diff_lambda, diff_subln_g, swa_sink, w_branch, w_out,
           router_w, router_bias, expert_w_gate, expert_w_up, expert_w_down,
           shared_w_gate, shared_w_up, shared_w_down, final_norm_g):
    B, S, D = x.shape
    N = B * S
    xf = x.reshape(N, D)
    mod_all = _ada_mod(c, w_ada, b_ada)
    tabs = _rope_tables(positions)
    c64, s64 = tabs[2], tabs[3]

    for l in range(DEPTH):
        mod = mod_all[l, :B].reshape(B, 6, D)

        h = _norm_mod(xf, mix_norm_g[l], mod, 0, 1, S)
        proj_r = _matmul_w_in(h, w_in, l, R_TILE_COLS, "proj_rotary")
        proj_x = _matmul_w_in(h, w_in, l, X_TILE_COLS, "proj_dilated")
        proj_v = _matmul_w_in(h, w_in, l, V_TILE_COLS, "proj_value")
        proj_a = _matmul(h, w_in, tn=A_PAD // 3, name="proj_latent", layer=l, n_cols=A_PAD)
        gates = _matmul(h, w_in[l, :, OFF_G:].astype(BF16), tn=1024, name="proj_gates")
        rop = _rope_section(proj_r, proj_v, tabs)
        dil_groups = _dil_rope_section(proj_x, tabs[0], tabs[1], B, S, tuple(d for _, d in DIL_PAIRS[1:]))

        wq, wkv = _mla_weights(mla_w_uq[l], mla_w_ukv[l])
        qm, kn, kr, vm = _mla_prep(proj_a, mla_q_norm_g[l], mla_kv_norm_g[l], wq, wkv, c64, s64)
        out_a = _mla_attention(qm, kn, kr, vm, B, S)

        lam_init = 0.8 - 0.6 * math.exp(-0.3 * l)
        out_b = _diff_attention(rop, proj_v, diff_lambda[l], diff_subln_g[l], lam_init, B, S)

        sink = jnp.repeat(swa_sink[l].astype(F32), LANES).reshape(1, SWA_HEADS * LANES)
        (out_c,) = _banded_attention(
            rop, rop, proj_v, sub_len=S, half_window=SWA_HALF_WINDOW,
            q_col=R_SWA_Q // HW4, k_col=R_SWA_K // LANES, v_col=V_SWA // LANES,
            n_col_groups=SWA_KV_HEADS, kv_width=LANES, grp=SWA_HEADS // SWA_KV_HEADS, sink=sink)

        window0, _ = DIL_PAIRS[0]
        d_res = list(_banded_attention(
            rop, rop, proj_v, sub_len=S, half_window=window0 // 2, q_col=R_DIL0_Q // HW4,
            k_col=R_DIL0_K // HW4, v_col=V_DIL0 // HW4, n_col_groups=1, kv_width=HW4, grp=1, want_lse=True))
        for (window, dil), (qg, kg, vg) in zip(DIL_PAIRS[1:], dil_groups):
            flat = lambda a: a.reshape(N, HW4)
            d_res += _banded_attention(
                flat(qg), flat(kg), flat(vg), sub_len=S // dil, half_window=window // (2 * dil),
                q_col=0, k_col=0, v_col=0, n_col_groups=1, kv_width=HW4, grp=1, want_lse=True)
        out_d = _dil_combine(*d_res, B, S, tuple(d for _, d in DIL_PAIRS[1:]))

        merged = _merge((out_a, out_b, out_c, out_d), w_branch, gates, l)
        xf = _matmul_residual(merged, w_out, xf, mod, 2, S, layer=l)

        hb, hp, top_idx, top_w = _router(xf, ffn_norm_g[l], mod, 3, 4, router_w[l], router_bias[l], S)
        block_expert, next_expert, block_off, tok_pad, slot_of, n_used = _dispatch_plan(top_idx)
        y = _expert_ffn(hp, expert_w_gate, expert_w_up, expert_w_down, l, block_expert, next_expert,
                        block_off, tok_pad, n_used)
        act = _matmul_glu(hb, shared_w_gate, shared_w_up, layer=l)
        shared = _matmul(act, shared_w_down, tn=1024, name="shared_down", layer=l)
        xf = _combine(y, slot_of, xf, shared, top_w.T, mod, 5, S)

    return _final_norm(xf, final_norm_g).reshape(B, S, D)
```

```python
import functools
import math

import numpy as np
import jax
import jax.numpy as jnp
from jax import lax
from jax.experimental import pallas as pl
from jax.experimental.pallas import tpu as pltpu

F32 = jnp.float32
BF16 = jnp.bfloat16

D_MODEL = 4096
DEPTH = 2
HEAD_DIM = 128
ROPE_THETA = 10000.0
NORM_EPS = 1e-6

MLA_HEADS = 8
MLA_Q_RANK = 768
MLA_KV_RANK = 256
MLA_NOPE_DIM = 128
MLA_ROPE_DIM = 64
MLA_V_DIM = 128

DIFF_HEADS = 8
DIFF_QK_DIM = 64
DIFF_V_DIM = 128
DIFF_NORM_EPS = 1e-5

SWA_HEADS = 8
SWA_KV_HEADS = 2
SWA_HALF_WINDOW = 128

DIL_PAIRS = ((128, 1), (512, 4), (2048, 16))
DIL_HEADS = 4

N_EXPERTS = 64
N_EXPERT_GROUPS = 8
TOP_GROUPS = 4
TOP_K = 8
EXPERT_FF = 256
SHARED_FF = 1024
ROUTED_SCALE = 2.5

A_COLS = MLA_Q_RANK + MLA_KV_RANK + MLA_ROPE_DIM
B_COLS = 3 * DIFF_HEADS * DIFF_V_DIM
C_COLS = (SWA_HEADS + 2 * SWA_KV_HEADS) * HEAD_DIM
D_GROUP_COLS = 3 * DIL_HEADS * HEAD_DIM
D_COLS = len(DIL_PAIRS) * D_GROUP_COLS
OFF_B = A_COLS
OFF_C = OFF_B + B_COLS
OFF_D = OFF_C + C_COLS
OFF_G = OFF_D + D_COLS

LANES = 128
HW4 = DIL_HEADS * HEAD_DIM

R_DIL0_Q = 0
R_DIL0_K = HW4
R_DIFF_Q = 2 * HW4
R_DIFF_K = R_DIFF_Q + 1024
R_SWA_Q = R_DIFF_K + 1024
R_IN_COLS = R_SWA_Q + 1024
R_SWA_K = R_IN_COLS
R_COLS = R_SWA_K + 256
X_GROUP = 3 * HW4
X_COLS = 2 * X_GROUP
V_DIL0 = 0
V_DIFF = HW4
V_SWA_K = V_DIFF + 1024
V_SWA = V_SWA_K + 256
V_COLS = 2048
A_PAD = 1152
PROJ_TILE = 512

R_TILE_COLS = ([OFF_D, OFF_D + 512] + [OFF_B + 512 * t for t in range(4)] + [OFF_C, OFF_C + 512])
X_TILE_COLS = [OFF_D + D_GROUP_COLS + 512 * t for t in range(6)]
V_TILE_COLS = [OFF_D + 2 * HW4, OFF_B + 2048, OFF_B + 2560, OFF_C + 1024]
G_TILE_COLS = [OFF_G + 512 * t for t in range(4 * D_MODEL // 512)]
A_TILE_COLS = [0, A_PAD // 3, 2 * A_PAD // 3]

EXPERT_BLOCK_ROWS = 256
COMBINE_TOKENS = 32
PACK_HALF = D_MODEL // 2
PACK_SLABS = PACK_HALF // LANES
VMEM_LIMIT = 56 * 1024 * 1024


def _cparams(sem, vmem=VMEM_LIMIT):
    return pltpu.CompilerParams(dimension_semantics=sem, vmem_limit_bytes=vmem)


def _nt_dot(a, b):
    return lax.dot_general(a, b, (((1,), (1,)), ((), ())), preferred_element_type=F32)


ADA_ROWS = 128
ADA_STREAMS = 16


def _ada_kernel(ct_ref, *refs, batch):
    w_refs, (b_ref, o_ref) = refs[:ADA_STREAMS], refs[ADA_STREAMS:]
    k = pl.program_id(1)
    ct = ct_ref[...]
    s = ct * jax.nn.sigmoid(ct)
    sub = ADA_ROWS // ADA_STREAMS
    rows = []
    for b in range(batch):
        acc = None
        for i, w_ref in enumerate(w_refs):
            t = jnp.sum(w_ref[0] * s[i * sub:(i + 1) * sub, b:b + 1], axis=0, keepdims=True)
            acc = t if acc is None else acc + t
        rows.append(acc)
    rows += [jnp.zeros_like(rows[0])] * (8 - batch)
    part = jnp.concatenate(rows, axis=0)

    @pl.when(k == 0)
    def _():
        o_ref[0] = part + b_ref[0]

    @pl.when(k > 0)
    def _():
        o_ref[0] += part


def _ada_mod(c, w_ada, b_ada):
    B, D = c.shape
    ct = jnp.zeros((D, 8), F32).at[:, :B].set(c.T)
    n6 = w_ada.shape[-1]
    sub = ADA_ROWS // ADA_STREAMS
    w_specs = [pl.BlockSpec((1, sub, n6), lambda l, k, i=i: (l, k * ADA_STREAMS + i, 0))
               for i in range(ADA_STREAMS)]
    return pl.pallas_call(
        functools.partial(_ada_kernel, batch=B),
        grid=(DEPTH, D // ADA_ROWS),
        in_specs=[pl.BlockSpec((ADA_ROWS, 8), lambda l, k: (k, 0))] + w_specs
        + [pl.BlockSpec((1, 1, n6), lambda l, k: (l, 0, 0))],
        out_specs=pl.BlockSpec((1, 8, n6), lambda l, k: (l, 0, 0)),
        out_shape=jax.ShapeDtypeStruct((DEPTH, 8, n6), F32),
        compiler_params=_cparams(("arbitrary", "arbitrary")),
        name="ada_mod",
    )(ct, *([w_ada] * ADA_STREAMS), b_ada.reshape(DEPTH, 1, n6))


def _norm_mod_kernel(x_ref, g_ref, mod_ref, o_ref, *, shift_row, scale_row):
    x = x_ref[...]
    ms = jnp.mean(x * x, axis=-1, keepdims=True)
    y = x * lax.rsqrt(ms + NORM_EPS) * g_ref[...]
    h = y * (1.0 + mod_ref[0, scale_row:scale_row + 1, :]) + mod_ref[0, shift_row:shift_row + 1, :]
    o_ref[...] = h.astype(o_ref.dtype)


def _norm_mod(x, g, mod, shift_row, scale_row, seq, tm=256):
    N, D = x.shape
    tm = min(tm, seq)
    per_b = seq // tm
    return pl.pallas_call(
        functools.partial(_norm_mod_kernel, shift_row=shift_row, scale_row=scale_row),
        grid=(N // tm,),
        in_specs=[
            pl.BlockSpec((tm, D), lambda i: (i, 0)),
            pl.BlockSpec((1, D), lambda i: (0, 0)),
            pl.BlockSpec((1, 6, D), lambda i: (i // per_b, 0, 0)),
        ],
        out_specs=pl.BlockSpec((tm, D), lambda i: (i, 0)),
        out_shape=jax.ShapeDtypeStruct((N, D), BF16),
        compiler_params=_cparams(("arbitrary",)),
        name="norm_mod",
    )(x, g.reshape(1, D), mod)


def _weight_scratch(*weights_and_tiles):
    return [pltpu.VMEM(tile, BF16) for w, tile in weights_and_tiles if w.dtype != BF16]


def _weight_spec(w, layer, rows, tn, row_block=0):
    if w.ndim == 2:
        return pl.BlockSpec((rows, tn), lambda j, i: (row_block, j))
    return pl.BlockSpec((None, rows, tn), lambda j, i: (layer, row_block, j))


def _resident_bf16(b_ref, scratch):
    if b_ref.dtype == BF16:
        return b_ref[...]
    s_ref = scratch.pop(0)

    @pl.when(pl.program_id(1) == 0)
    def _():
        s_ref[...] = b_ref[...].astype(BF16)

    return s_ref[...]


def _mm_kernel(a_ref, b_ref, o_ref, *scratch):
    b = _resident_bf16(b_ref, list(scratch))
    o_ref[...] = jnp.dot(a_ref[...], b, preferred_element_type=F32).astype(o_ref.dtype)


def _matmul(a, b, tn, out_dtype=BF16, tm=1024, name="matmul", layer=0, n_cols=None):
    M, K = a.shape
    Nc = b.shape[-1] if n_cols is None else n_cols
    tm = min(tm, M)
    return pl.pallas_call(
        _mm_kernel,
        grid=(Nc // tn, M // tm),
        in_specs=[pl.BlockSpec((tm, K), lambda j, i: (i, 0)), _weight_spec(b, layer, K, tn)],
        out_specs=pl.BlockSpec((tm, tn), lambda j, i: (i, j)),
        out_shape=jax.ShapeDtypeStruct((M, Nc), out_dtype),
        scratch_shapes=_weight_scratch((b, (K, tn))),
        compiler_params=_cparams(("arbitrary", "arbitrary")),
        name=name,
    )(a, b)


W_IN_ROW_ALIGN = 64


def _mm_rows_kernel(starts_ref, a_ref, bt_ref, o_ref, scr):
    @pl.when(pl.program_id(1) == 0)
    def _():
        scr[...] = bt_ref[...].astype(BF16)

    o_ref[...] = _nt_dot(a_ref[...], scr[...]).astype(o_ref.dtype)


def _matmul_w_in(a, w_in_t, layer, tile_cols, name, tn=PROJ_TILE, tm=1024):
    M, K = a.shape
    tm = min(tm, M)
    assert all(c % W_IN_ROW_ALIGN == 0 for c in tile_cols)
    grid_spec = pltpu.PrefetchScalarGridSpec(
        num_scalar_prefetch=1,
        grid=(len(tile_cols), M // tm),
        in_specs=[
            pl.BlockSpec((tm, K), lambda j, i, s: (i, 0)),
            pl.BlockSpec((None, pl.Element(tn), pl.Element(K)),
                         lambda j, i, s: (layer, pl.multiple_of(s[j], W_IN_ROW_ALIGN), 0)),
        ],
        out_specs=pl.BlockSpec((tm, tn), lambda j, i, s: (i, j)),
        scratch_shapes=[pltpu.VMEM((tn, K), BF16)],
    )
    return pl.pallas_call(
        _mm_rows_kernel,
        grid_spec=grid_spec,
        out_shape=jax.ShapeDtypeStruct((M, len(tile_cols) * tn), BF16),
        compiler_params=_cparams(("arbitrary", "arbitrary")),
        name=name,
    )(jnp.asarray(tile_cols, jnp.int32), a, w_in_t)


def _mm_glu_kernel(a_ref, bg_ref, bu_ref, o_ref, *scratch):
    scratch = list(scratch)
    bg = _resident_bf16(bg_ref, scratch)
    bu = _resident_bf16(bu_ref, scratch)
    a = a_ref[...]
    g = jnp.dot(a, bg, preferred_element_type=F32)
    u = jnp.dot(a, bu, preferred_element_type=F32)
    o_ref[...] = (g * jax.nn.sigmoid(g) * u).astype(o_ref.dtype)


def _matmul_glu(a, bg, bu, tn=256, tm=1024, layer=0):
    M, K = a.shape
    Nc = bg.shape[-1]
    tm = min(tm, M)
    return pl.pallas_call(
        _mm_glu_kernel,
        grid=(Nc // tn, M // tm),
        in_specs=[
            pl.BlockSpec((tm, K), lambda j, i: (i, 0)),
            _weight_spec(bg, layer, K, tn),
            _weight_spec(bu, layer, K, tn),
        ],
        out_specs=pl.BlockSpec((tm, tn), lambda j, i: (i, j)),
        out_shape=jax.ShapeDtypeStruct((M, Nc), BF16),
        scratch_shapes=_weight_scratch((bg, (K, tn)), (bu, (K, tn))),
        compiler_params=_cparams(("arbitrary", "arbitrary")),
        name="shared_glu",
    )(a, bg, bu)


def _mm_residual_kernel(a_ref, b_ref, x_ref, mod_ref, o_ref, *scratch, gate_row):
    b = _resident_bf16(b_ref, list(scratch))
    acc = jnp.dot(a_ref[...], b, preferred_element_type=F32)
    o_ref[...] = x_ref[...] + mod_ref[0, gate_row:gate_row + 1, :] * acc


def _matmul_residual(a, b, x, mod, gate_row, seq, tn=512, tm=1024, layer=0):
    M, K = a.shape
    Nc = b.shape[-1]
    tm = min(tm, seq)
    per_b = seq // tm
    return pl.pallas_call(
        functools.partial(_mm_residual_kernel, gate_row=gate_row),
        grid=(Nc // tn, M // tm),
        in_specs=[
            pl.BlockSpec((tm, K), lambda j, i: (i, 0)),
            _weight_spec(b, layer, K, tn),
            pl.BlockSpec((tm, tn), lambda j, i: (i, j)),
            pl.BlockSpec((1, 6, tn), lambda j, i: (i // per_b, 0, j)),
        ],
        out_specs=pl.BlockSpec((tm, tn), lambda j, i: (i, j)),
        out_shape=jax.ShapeDtypeStruct((M, Nc), F32),
        scratch_shapes=_weight_scratch((b, (K, tn))),
        compiler_params=_cparams(("arbitrary", "arbitrary")),
        name="out_proj_residual",
    )(a, b, x, mod)


def _rope_table_kernel(pos_ref, inv128_ref, inv64_ref, c128_ref, s128_ref, c64_ref, s64_ref):
    pos = pos_ref[...]
    lane = lax.broadcasted_iota(jnp.int32, pos.shape, 1)
    a = pos * inv128_ref[...]
    c128_ref[...] = jnp.cos(a)
    s128_ref[...] = jnp.where(lane < 64, -jnp.sin(a), jnp.sin(a))
    a = pos * inv64_ref[...]
    c64_ref[...] = jnp.cos(a)
    s64_ref[...] = jnp.where((lane & 63) < 32, -jnp.sin(a), jnp.sin(a))


def _rope_tables(positions, tm=512):
    N = positions.size
    tm = min(tm, N)
    pos = jnp.broadcast_to(positions.astype(F32).reshape(N, 1), (N, LANES))

    def inv(half):
        f = np.float32(ROPE_THETA) ** (-(np.arange(half, dtype=np.float32) / np.float32(half)))
        return jnp.asarray(np.tile(f.astype(np.float32), LANES // half).reshape(1, LANES))

    tab = jax.ShapeDtypeStruct((N, LANES), F32)
    row = pl.BlockSpec((tm, LANES), lambda i: (i, 0))
    one = pl.BlockSpec((1, LANES), lambda i: (0, 0))
    return pl.pallas_call(
        _rope_table_kernel,
        grid=(N // tm,),
        in_specs=[row, one, one],
        out_specs=[row, row, row, row],
        out_shape=[tab, tab, tab, tab],
        compiler_params=_cparams(("arbitrary",)),
        name="rope_tables",
    )(pos, inv(64), inv(32))


def _rot128(x):
    return pltpu.roll(x, 64, 1)


def _rot64(x):
    lane = lax.broadcasted_iota(jnp.int32, x.shape, 1)
    return jnp.where((lane & 63) < 32, pltpu.roll(x, 96, 1), pltpu.roll(x, 32, 1))


def _r_block_plan():
    plan = [(128, HEAD_DIM ** -0.5)] * 4 + [(128, 1.0)] * 4
    plan += [(64, DIFF_QK_DIM ** -0.5)] * 8 + [(64, 1.0)] * 8
    plan += [(128, HEAD_DIM ** -0.5)] * 8 + [(128, 1.0)] * 2
    return plan


def _rope_kernel(p_ref, swak_ref, c128_ref, s128_ref, c64_ref, s64_ref, o_ref):
    c128, s128, c64, s64 = c128_ref[...], s128_ref[...], c64_ref[...], s64_ref[...]
    n_in = R_IN_COLS // LANES
    for blk, (flavour, scale) in enumerate(_r_block_plan()):
        if blk < n_in:
            x = p_ref[:, blk * LANES:(blk + 1) * LANES].astype(F32)
        else:
            x = swak_ref[:, (blk - n_in) * LANES:(blk - n_in + 1) * LANES].astype(F32)
        if flavour == 128:
            y = x * c128 + _rot128(x) * s128
        else:
            y = x * c64 + _rot64(x) * s64
        if scale != 1.0:
            y = y * scale
        o_ref[:, blk * LANES:(blk + 1) * LANES] = y.astype(o_ref.dtype)


def _rope_section(proj_r, proj_v, tabs, tm=256):
    N = proj_r.shape[0]
    tm = min(tm, N)
    tab = pl.BlockSpec((tm, LANES), lambda i: (i, 0))
    return pl.pallas_call(
        _rope_kernel,
        grid=(N // tm,),
        in_specs=[pl.BlockSpec((tm, R_IN_COLS), lambda i: (i, 0)),
                  pl.BlockSpec((tm, 256), lambda i: (i, V_SWA_K // 256)), tab, tab, tab, tab],
        out_specs=pl.BlockSpec((tm, R_COLS), lambda i: (i, 0)),
        out_shape=jax.ShapeDtypeStruct((N, R_COLS), BF16),
        compiler_params=_cparams(("arbitrary",)),
        name="rope_section",
    )(proj_r, proj_v, *tabs)


def _dil_rope_kernel(x_ref, c128_ref, s128_ref, *refs, dils, tm):
    scr = refs[-1]
    outs = refs[:-1]
    c128, s128 = c128_ref[...], s128_ref[...]
    for g, d in enumerate(dils):
        for part in range(3):
            o_ref = outs[g * 3 + part]
            base = (g * 3 + part) * DIL_HEADS
            for c in range(DIL_HEADS):
                col = g * X_GROUP + part * HW4 + c * LANES
                x = x_ref[:, col:col + LANES].astype(F32)
                if part < 2:
                    x = x * c128 + _rot128(x) * s128
                if part == 0:
                    x = x * (HEAD_DIM ** -0.5)
                scr[base + c] = x
            for r in range(d):
                for c in range(DIL_HEADS):
                    o_ref[r, :, c * LANES:(c + 1) * LANES] = (
                        scr[base + c, pl.ds(r, tm // d, stride=d), :].astype(o_ref.dtype))


def _dil_rope_section(proj_x, c128, s128, batch, seq, dils, tm=256):
    N = proj_x.shape[0]
    tm = min(tm, seq)
    per_b = seq // tm
    tab = pl.BlockSpec((tm, LANES), lambda i: (i, 0))
    out_specs, out_shape = [], []
    for d in dils:
        for _ in range(3):
            out_specs.append(pl.BlockSpec((d, tm // d, HW4), lambda i: (i // per_b, i % per_b, 0)))
            out_shape.append(jax.ShapeDtypeStruct((batch * d, seq // d, HW4), BF16))
    outs = pl.pallas_call(
        functools.partial(_dil_rope_kernel, dils=dils, tm=tm),
        grid=(N // tm,),
        in_specs=[pl.BlockSpec((tm, X_COLS), lambda i: (i, 0)), tab, tab],
        out_specs=out_specs,
        out_shape=out_shape,
        scratch_shapes=[pltpu.VMEM((len(dils) * 3 * DIL_HEADS, tm, LANES), F32)],
        compiler_params=_cparams(("arbitrary",)),
        name="dil_rope_section",
    )(proj_x, c128, s128)
    return [outs[3 * g:3 * g + 3] for g in range(len(dils))]


MLA_QW = 256


def _mla_prep_kernel(a_ref, gq_ref, gkv_ref, wq_ref, wkv_ref, c64_ref, s64_ref,
                     q_ref, kn_ref, kr_ref, v_ref):
    a = a_ref[...].astype(F32)
    c64, s64 = c64_ref[...], s64_ref[...]
    scale = (MLA_NOPE_DIM + MLA_ROPE_DIM) ** -0.5

    cq = a[:, :MLA_Q_RANK]
    cq = cq * lax.rsqrt(jnp.mean(cq * cq, axis=-1, keepdims=True) + NORM_EPS) * gq_ref[...]
    q = jnp.dot(cq.astype(BF16), wq_ref[...], preferred_element_type=F32)
    for h in range(MLA_HEADS):
        qn = q[:, h * MLA_QW:h * MLA_QW + LANES]
        qr = q[:, h * MLA_QW + LANES:(h + 1) * MLA_QW]
        qr = qr * c64 + _rot64(qr) * s64
        q_ref[:, h * MLA_QW:h * MLA_QW + LANES] = (qn * scale).astype(q_ref.dtype)
        q_ref[:, h * MLA_QW + LANES:(h + 1) * MLA_QW] = (qr * scale).astype(q_ref.dtype)

    ckv = a[:, MLA_Q_RANK:MLA_Q_RANK + MLA_KV_RANK]
    ckv = ckv * lax.rsqrt(jnp.mean(ckv * ckv, axis=-1, keepdims=True) + NORM_EPS) * gkv_ref[...]
    kv = jnp.dot(ckv.astype(BF16), wkv_ref[...], preferred_element_type=F32)
    kn_ref[...] = kv[:, :MLA_HEADS * MLA_NOPE_DIM].astype(kn_ref.dtype)
    v_ref[...] = kv[:, MLA_HEADS * MLA_NOPE_DIM:].astype(v_ref.dtype)

    kr = a[:, MLA_Q_RANK + MLA_KV_RANK:]
    lane = lax.broadcasted_iota(jnp.int32, kr.shape, 1)
    kr = jnp.where(lane < MLA_ROPE_DIM, kr, 0.0)
    kr_ref[...] = (kr * c64 + _rot64(kr) * s64).astype(kr_ref.dtype)


def _mla_prep(proj_a, gq, gkv, wq, wkv, c64, s64, tm=256):
    N = proj_a.shape[0]
    tm = min(tm, N)
    row = lambda w: pl.BlockSpec((tm, w), lambda i: (i, 0))
    full = lambda r, c: pl.BlockSpec((r, c), lambda i: (0, 0))
    hq = MLA_HEADS * MLA_QW
    hk = MLA_HEADS * MLA_NOPE_DIM
    return pl.pallas_call(
        _mla_prep_kernel,
        grid=(N // tm,),
        in_specs=[row(A_PAD), full(1, MLA_Q_RANK), full(1, MLA_KV_RANK), full(MLA_Q_RANK, hq),
                  full(MLA_KV_RANK, 2 * hk), row(LANES), row(LANES)],
        out_specs=[row(hq), row(hk), row(LANES), row(hk)],
        out_shape=[jax.ShapeDtypeStruct((N, hq), BF16), jax.ShapeDtypeStruct((N, hk), BF16),
                   jax.ShapeDtypeStruct((N, LANES), BF16), jax.ShapeDtypeStruct((N, hk), BF16)],
        compiler_params=_cparams(("arbitrary",)),
        name="mla_prep",
    )(proj_a, gq.reshape(1, -1), gkv.reshape(1, -1), wq, wkv, c64, s64)


def _softmax_pv(q, k_ref, v_ref, kc):
    tq = q.shape[0]
    S = k_ref.shape[0]
    m = jnp.full((tq, 1), -jnp.inf, F32)
    l = jnp.zeros((tq, 1), F32)
    acc = jnp.zeros((tq, v_ref.shape[1]), F32)
    for c in range(S // kc):
        s = _nt_dot(q, k_ref[c * kc:(c + 1) * kc, :])
        m_new = jnp.maximum(m, jnp.max(s, axis=-1, keepdims=True))
        alpha = jnp.exp(m - m_new)
        p = jnp.exp(s - m_new)
        l = alpha * l + jnp.sum(p, axis=-1, keepdims=True)
        acc = alpha * acc + jnp.dot(p.astype(BF16), v_ref[c * kc:(c + 1) * kc, :],
                                    preferred_element_type=F32)
        m = m_new
    return acc / l


def _mla_attn_kernel(q_ref, kn_ref, kr_ref, v_ref, o_ref, kcat_ref, *, kc):
    @pl.when(pl.program_id(2) == 0)
    def _():
        kcat_ref[:, :LANES] = kn_ref[...]
        kcat_ref[:, LANES:] = kr_ref[...]

    o_ref[...] = _softmax_pv(q_ref[...], kcat_ref, v_ref, kc).astype(o_ref.dtype)


def _mla_attention(qm, kn, kr, vm, batch, seq, tq=1024, kc=1024):
    N = qm.shape[0]
    tq = min(tq, seq)
    kc = min(kc, seq)
    nq = seq // tq
    return pl.pallas_call(
        functools.partial(_mla_attn_kernel, kc=kc),
        grid=(batch, MLA_HEADS, nq),
        in_specs=[
            pl.BlockSpec((tq, MLA_QW), lambda b, h, i: (b * nq + i, h)),
            pl.BlockSpec((seq, LANES), lambda b, h, i: (b, h)),
            pl.BlockSpec((seq, LANES), lambda b, h, i: (b, 0)),
            pl.BlockSpec((seq, LANES), lambda b, h, i: (b, h)),
        ],
        out_specs=pl.BlockSpec((tq, LANES), lambda b, h, i: (b * nq + i, h)),
        out_shape=jax.ShapeDtypeStruct((N, MLA_HEADS * MLA_V_DIM), BF16),
        scratch_shapes=[pltpu.VMEM((seq, MLA_QW), BF16)],
        compiler_params=_cparams(("arbitrary", "arbitrary", "arbitrary")),
        name="mla_attention",
    )(qm, kn, kr, vm)


def _diff_attn_kernel(q_ref, k_ref, v_ref, lam_ref, g_ref, o_ref, *, kc, lam_init):
    q = q_ref[...]
    lane = lax.broadcasted_iota(jnp.int32, q.shape, 1)
    zero = jnp.zeros_like(q)
    o0 = _softmax_pv(jnp.where(lane < DIFF_QK_DIM, q, zero), k_ref, v_ref, kc)
    o1 = _softmax_pv(jnp.where(lane >= DIFF_QK_DIM, q, zero), k_ref, v_ref, kc)
    lp = lam_ref[...]
    lam = (jnp.exp(jnp.sum(lp[0:1] * lp[1:2], axis=-1, keepdims=True))
           - jnp.exp(jnp.sum(lp[2:3] * lp[3:4], axis=-1, keepdims=True)) + lam_init)
    o = o0 - lam * o1
    o = o * lax.rsqrt(jnp.mean(o * o, axis=-1, keepdims=True) + DIFF_NORM_EPS) * g_ref[...]
    o_ref[...] = (o * (1.0 - lam_init)).astype(o_ref.dtype)


def _diff_attention(rop, proj_v, lam_params, subln_g, lam_init, batch, seq, tq=1024, kc=1024):
    N = rop.shape[0]
    tq = min(tq, seq)
    kc = min(kc, seq)
    nq = seq // tq
    qb, kb, vb = R_DIFF_Q // LANES, R_DIFF_K // LANES, V_DIFF // LANES
    return pl.pallas_call(
        functools.partial(_diff_attn_kernel, kc=kc, lam_init=lam_init),
        grid=(batch, DIFF_HEADS, nq),
        in_specs=[
            pl.BlockSpec((tq, LANES), lambda b, h, i: (b * nq + i, qb + h)),
            pl.BlockSpec((seq, LANES), lambda b, h, i: (b, kb + h)),
            pl.BlockSpec((seq, LANES), lambda b, h, i: (b, vb + h)),
            pl.BlockSpec((4, DIFF_QK_DIM), lambda b, h, i: (0, 0)),
            pl.BlockSpec((1, DIFF_V_DIM), lambda b, h, i: (0, 0)),
        ],
        out_specs=pl.BlockSpec((tq, LANES), lambda b, h, i: (b * nq + i, h)),
        out_shape=jax.ShapeDtypeStruct((N, DIFF_HEADS * DIFF_V_DIM), BF16),
        compiler_params=_cparams(("arbitrary", "arbitrary", "arbitrary")),
        name="diff_attention",
    )(rop, rop, proj_v, lam_params, subln_g.reshape(1, -1))


def _banded_kernel(*refs, half_window, blk, blocks_per_seq, grp, use_sink, want_lse):
    q_ref, kp_ref, kc_ref, kn_ref, vp_ref, vc_ref, vn_ref = refs[:7]
    rest = list(refs[7:])
    sink_ref = rest.pop(0) if use_sink else None
    o_ref = rest.pop(0)
    lse_ref = rest.pop(0) if want_lse else None

    il = lax.rem(pl.program_id(0), blocks_per_seq)
    qpos = il * blk + lax.broadcasted_iota(jnp.int32, (blk, 3 * blk), 0)
    kpos = (il - 1) * blk + lax.broadcasted_iota(jnp.int32, (blk, 3 * blk), 1)
    valid = ((jnp.abs(qpos - kpos) <= half_window) & (kpos >= 0) & (kpos < blocks_per_seq * blk))

    heads = range(DIL_HEADS)
    cq = [slice(h * LANES, (h + 1) * LANES) for h in heads]
    ck = [slice((h // grp) * LANES, (h // grp + 1) * LANES) for h in heads]
    scores = []
    for h in heads:
        k = jnp.concatenate([kp_ref[:, ck[h]], kc_ref[:, ck[h]], kn_ref[:, ck[h]]], axis=0)
        scores.append(jnp.where(valid, _nt_dot(q_ref[:, cq[h]], k), -jnp.inf))
    probs, dens, maxes = [], [], []
    for h in heads:
        m = jnp.max(scores[h], axis=-1, keepdims=True)
        if use_sink:
            sk = jnp.max(jnp.broadcast_to(sink_ref[0:1, cq[h]], (blk, LANES)), axis=-1, keepdims=True)
            m = jnp.maximum(m, sk)
        p = jnp.exp(scores[h] - m)
        den = jnp.sum(p, axis=-1, keepdims=True)
        if use_sink:
            den = den + jnp.exp(sk - m)
        probs.append(p.astype(BF16))
        dens.append(den)
        maxes.append(m)
    for h in heads:
        v = jnp.concatenate([vp_ref[:, ck[h]], vc_ref[:, ck[h]], vn_ref[:, ck[h]]], axis=0)
        o = jnp.dot(probs[h], v, preferred_element_type=F32) / dens[h]
        o_ref[:, cq[h]] = o.astype(o_ref.dtype)
        if want_lse:
            lse_ref[:, cq[h]] = jnp.broadcast_to(maxes[h] + jnp.log(dens[h]), (blk, LANES))


def _banded_attention(q_arr, k_arr, v_arr, *, sub_len, half_window, q_col, k_col, v_col,
                      n_col_groups, kv_width, grp, sink=None, want_lse=False):
    rows = q_arr.shape[0]
    blk = min(128, sub_len)
    blocks_per_seq = sub_len // blk
    n_row_blocks = rows // blk
    last = n_row_blocks - 1

    def prev(i):
        return jnp.maximum(i - 1, 0)

    def nxt(i):
        return jnp.minimum(i + 1, last)

    qspec = pl.BlockSpec((blk, HW4), lambda i, c: (i, q_col + c))
    kspecs = [pl.BlockSpec((blk, kv_width), lambda i, c, f=f: (f(i), k_col + c))
              for f in (prev, lambda i: i, nxt)]
    vspecs = [pl.BlockSpec((blk, kv_width), lambda i, c, f=f: (f(i), v_col + c))
              for f in (prev, lambda i: i, nxt)]
    in_specs = [qspec] + kspecs + vspecs
    args = [q_arr, k_arr, k_arr, k_arr, v_arr, v_arr, v_arr]
    if sink is not None:
        in_specs.append(pl.BlockSpec((1, HW4), lambda i, c: (0, c)))
        args.append(sink)
    out_cols = n_col_groups * HW4
    ospec = pl.BlockSpec((blk, HW4), lambda i, c: (i, c))
    out_specs = [ospec]
    out_shape = [jax.ShapeDtypeStruct((rows, out_cols), BF16)]
    if want_lse:
        out_specs.append(ospec)
        out_shape.append(jax.ShapeDtypeStruct((rows, out_cols), F32))
    return pl.pallas_call(
        functools.partial(_banded_kernel, half_window=half_window, blk=blk,
                          blocks_per_seq=blocks_per_seq, grp=grp, use_sink=sink is not None,
                          want_lse=want_lse),
        grid=(n_row_blocks, n_col_groups),
        in_specs=in_specs,
        out_specs=out_specs,
        out_shape=out_shape,
        compiler_params=_cparams(("arbitrary", "arbitrary")),
        name="banded_attention",
    )(*args)


def _dil_combine_kernel(o0, l0, o1, l1, o2, l2, out_ref, scr, *, dils, tm):
    def natural(o_ref, l_ref, d, base):
        for r in range(d):
            for c in range(DIL_HEADS):
                cols = slice(c * LANES, (c + 1) * LANES)
                scr[base + c, pl.ds(r, tm // d, stride=d), :] = o_ref[r, :, cols].astype(F32)
                scr[base + DIL_HEADS + c, pl.ds(r, tm // d, stride=d), :] = l_ref[r, :, cols]

    natural(o1, l1, dils[0], 0)
    natural(o2, l2, dils[1], 2 * DIL_HEADS)
    for c in range(DIL_HEADS):
        cols = slice(c * LANES, (c + 1) * LANES)
        oa, la = o0[:, cols].astype(F32), l0[:, cols]
        ob, lb = scr[c], scr[DIL_HEADS + c]
        oc, lc = scr[2 * DIL_HEADS + c], scr[3 * DIL_HEADS + c]
        m = jnp.maximum(jnp.maximum(la, lb), lc)
        ea, eb, ec = jnp.exp(la - m), jnp.exp(lb - m), jnp.exp(lc - m)
        inv = 1.0 / (ea + eb + ec)
        out_ref[:, cols] = ((ea * inv) * oa + (eb * inv) * ob + (ec * inv) * oc).astype(out_ref.dtype)


def _dil_combine(o0, l0, o1, l1, o2, l2, batch, seq, dils, tm=256):
    N = o0.shape[0]
    tm = min(tm, seq)
    per_b = seq // tm
    row = pl.BlockSpec((tm, HW4), lambda i: (i, 0))
    grouped = [pl.BlockSpec((d, tm // d, HW4), lambda i: (i // per_b, i % per_b, 0)) for d in dils]
    shaped = lambda a, d: a.reshape(batch * d, seq // d, HW4)
    return pl.pallas_call(
        functools.partial(_dil_combine_kernel, dils=dils, tm=tm),
        grid=(N // tm,),
        in_specs=[row, row, grouped[0], grouped[0], grouped[1], grouped[1]],
        out_specs=row,
        out_shape=jax.ShapeDtypeStruct((N, HW4), BF16),
        scratch_shapes=[pltpu.VMEM((4 * DIL_HEADS, tm, LANES), F32)],
        compiler_params=_cparams(("arbitrary",)),
        name="dil_combine",
    )(o0, l0, shaped(o1, dils[0]), shaped(l1, dils[0]), shaped(o2, dils[1]), shaped(l2, dils[1]))


def _merge_kernel(oa, ob, oc, od, wa, wb, wc, wd, ga, gb, gc, gd, o_ref, *scratch):
    scratch = list(scratch)
    acc = None
    for o, w, g in ((oa, wa, ga), (ob, wb, gb), (oc, wc, gc), (od, wd, gd)):
        wt = _resident_bf16(w, scratch)
        gate = 0.5 * jnp.tanh(0.5 * g[...].astype(F32)) + 0.5
        t = gate * jnp.dot(o[...], wt, preferred_element_type=F32)
        acc = t if acc is None else acc + t
    o_ref[...] = acc.astype(o_ref.dtype)


def _merge(branch_outs, w_branch, gates, layer, tn=512, tm=1024):
    N = gates.shape[0]
    D = D_MODEL
    tm = min(tm, N)
    oa, ob, oc, od = branch_outs
    in_specs = [pl.BlockSpec((tm, o.shape[1]), lambda j, i: (i, 0)) for o in branch_outs]
    rows = [o.shape[1] for o in branch_outs]
    row_blocks = [0, 1, 2, 3072 // rows[3]]
    in_specs += [_weight_spec(w_branch, layer, rows[r], tn, row_blocks[r]) for r in range(4)]
    nj = D // tn
    in_specs += [pl.BlockSpec((tm, tn), lambda j, i, r=r: (i, r * nj + j)) for r in range(4)]
    return pl.pallas_call(
        _merge_kernel,
        grid=(nj, N // tm),
        in_specs=in_specs,
        out_specs=pl.BlockSpec((tm, tn), lambda j, i: (i, j)),
        out_shape=jax.ShapeDtypeStruct((N, D), BF16),
        scratch_shapes=_weight_scratch(*[(w_branch, (rows[r], tn)) for r in range(4)]),
        compiler_params=_cparams(("arbitrary", "arbitrary")),
        name="branch_merge",
    )(oa, ob, oc, od, w_branch, w_branch, w_branch, w_branch, gates, gates, gates, gates)


VMEM_PITCH = 20


def _pack_pairs(lo, hi):
    lo_b = lax.bitcast_convert_type(lo.astype(BF16).astype(F32), jnp.uint32)
    hi_b = lax.bitcast_convert_type(hi.astype(BF16).astype(F32), jnp.uint32)
    return (lo_b >> 16) | (hi_b & jnp.uint32(0xFFFF0000))


def _unpack_pairs(w):
    lo = lax.bitcast_convert_type(w << 16, F32)
    hi = lax.bitcast_convert_type(w & jnp.uint32(0xFFFF0000), F32)
    return lo, hi


def _store_packed(o_ref, x):
    rows = x.shape[0]
    for j in range(PACK_SLABS):
        o_ref[pl.ds(j, rows, stride=PACK_SLABS), :] = _pack_pairs(
            x[:, j * LANES:(j + 1) * LANES], x[:, PACK_HALF + j * LANES:PACK_HALF + (j + 1) * LANES])


def _load_slab(buf_ref, first_row, rows, j):
    return buf_ref[pl.ds(first_row + j, rows, stride=VMEM_PITCH), :]


def _router_kernel(x_ref, g_ref, mod_ref, rw_ref, rb_ref, hb_ref, hp_ref, idx_ref, wt_ref, *,
                   shift_row, scale_row):
    x = x_ref[...]
    ms = jnp.mean(x * x, axis=-1, keepdims=True)
    y = x * lax.rsqrt(ms + NORM_EPS) * g_ref[...]
    h = y * (1.0 + mod_ref[0, scale_row:scale_row + 1, :]) + mod_ref[0, shift_row:shift_row + 1, :]
    hb_ref[...] = h.astype(hb_ref.dtype)
    _store_packed(hp_ref, h)

    tm = x.shape[0]
    logits = jnp.dot(h, rw_ref[...], preferred_element_type=F32, precision=lax.Precision.HIGHEST)
    lt = logits.T[:N_EXPERTS, :]
    scores = jax.nn.sigmoid(lt)
    choice = scores + rb_ref[...]

    per_group = N_EXPERTS // N_EXPERT_GROUPS
    sub = lax.broadcasted_iota(jnp.int32, (per_group, tm), 0)
    group_score = []
    for g in range(N_EXPERT_GROUPS):
        cg = choice[g * per_group:(g + 1) * per_group, :]
        m1 = jnp.max(cg, axis=0, keepdims=True)
        first = jnp.min(jnp.where(cg == m1, sub, per_group), axis=0, keepdims=True)
        m2 = jnp.max(jnp.where(sub == first, -jnp.inf, cg), axis=0, keepdims=True)
        group_score.append(m1 + m2)
    masked = []
    for g in range(N_EXPERT_GROUPS):
        ahead = jnp.zeros((1, tm), jnp.int32)
        for o in range(N_EXPERT_GROUPS):
            if o == g:
                continue
            better = (group_score[o] >= group_score[g]) if o < g else (group_score[o] > group_score[g])
            ahead = ahead + better.astype(jnp.int32)
        cg = choice[g * per_group:(g + 1) * per_group, :]
        masked.append(jnp.where(ahead < TOP_GROUPS, cg, -jnp.inf))
    cm = jnp.concatenate(masked, axis=0)

    eidx = lax.broadcasted_iota(jnp.int32, (N_EXPERTS, tm), 0)
    rank = jnp.zeros((N_EXPERTS, tm), jnp.int32)
    for e in range(N_EXPERTS):
        row = cm[e:e + 1, :]
        tie = jnp.where(eidx > e, 1, 0)
        rank = rank + jnp.where(row > cm, 1, jnp.where(row == cm, tie, 0))
    sel = rank < TOP_K
    wsel = jnp.where(sel, scores, 0.0)
    wsel = wsel / jnp.sum(wsel, axis=0, keepdims=True) * ROUTED_SCALE
    idx_rows, wt_rows = [], []
    for r in range(TOP_K):
        hit = rank == r
        idx_rows.append(jnp.sum(jnp.where(hit, eidx, 0), axis=0, keepdims=True))
        wt_rows.append(jnp.sum(jnp.where(hit, wsel, 0.0), axis=0, keepdims=True))
    idx_ref[...] = jnp.concatenate(idx_rows, axis=0)
    wt_ref[...] = jnp.concatenate(wt_rows, axis=0)


def _router(x, g, mod, shift_row, scale_row, router_w, router_bias, seq, tm=256):
    N, D = x.shape
    tm = min(tm, seq)
    per_b = seq // tm
    rw = jnp.zeros((D, LANES), F32).at[:, :N_EXPERTS].set(router_w)
    row = pl.BlockSpec((tm, D), lambda i: (i, 0))
    col = lambda r: pl.BlockSpec((r, tm), lambda i: (0, i))
    return pl.pallas_call(
        functools.partial(_router_kernel, shift_row=shift_row, scale_row=scale_row),
        grid=(N // tm,),
        in_specs=[
            row,
            pl.BlockSpec((1, D), lambda i: (0, 0)),
            pl.BlockSpec((1, 6, D), lambda i: (i // per_b, 0, 0)),
            pl.BlockSpec((D, LANES), lambda i: (0, 0)),
            pl.BlockSpec((N_EXPERTS, 1), lambda i: (0, 0)),
        ],
        out_specs=[row, pl.BlockSpec((tm * PACK_SLABS, LANES), lambda i: (i, 0)), col(TOP_K), col(TOP_K)],
        out_shape=[jax.ShapeDtypeStruct((N, D), BF16), jax.ShapeDtypeStruct((N * PACK_SLABS, LANES), jnp.uint32),
                   jax.ShapeDtypeStruct((TOP_K, N), jnp.int32), jax.ShapeDtypeStruct((TOP_K, N), F32)],
        compiler_params=_cparams(("arbitrary",)),
        name="ffn_norm_router",
    )(x, g.reshape(1, D), mod, rw, router_bias.reshape(N_EXPERTS, 1))


def _num_expert_blocks(n_tokens):
    bm = EXPERT_BLOCK_ROWS
    return -(-(n_tokens * TOP_K + N_EXPERTS * (bm - 1)) // bm)


def _dispatch_plan(top_idx):
    K, N = top_idx.shape
    bm = EXPERT_BLOCK_ROWS
    n_blocks = _num_expert_blocks(N)
    experts = jnp.arange(N_EXPERTS, dtype=jnp.int32)
    onehot = top_idx[:, None, :] == experts[None, :, None]
    mask = jnp.any(onehot, axis=0).astype(jnp.int32)
    counts = jnp.sum(mask, axis=1)
    pos = jnp.cumsum(mask, axis=1) - mask
    nblk = (counts + bm - 1) // bm
    blk_end = jnp.cumsum(nblk)
    blk_start = blk_end - nblk
    n_used = blk_end[-1]
    start = jnp.cumsum(counts) - counts
    slot_en = (blk_start * bm)[:, None] + pos
    slot_of = jnp.sum(jnp.where(onehot, slot_en[None], 0), axis=1)

    keys = top_idx * N + jnp.arange(N, dtype=jnp.int32)[None, :]
    tok_sorted = jnp.sort(keys.reshape(-1)) % N

    bidx = jnp.arange(n_blocks + GATHER_AHEAD, dtype=jnp.int32)
    be = jnp.minimum(jnp.sum((bidx[:, None] >= blk_end[None, :]).astype(jnp.int32), axis=1),
                     N_EXPERTS - 1)
    be = jnp.where(bidx < n_used, be, be[jnp.maximum(n_used - 1, 0)])
    off = jnp.clip(start[be] + (bidx - blk_start[be]) * bm, 0, K * N)
    tok_pad = jnp.concatenate([tok_sorted, jnp.zeros((bm,), jnp.int32)])
    run_end = blk_end[be]
    nxt = jnp.where(run_end < n_used, be[jnp.minimum(run_end, n_blocks)], -1)
    return (be.astype(jnp.int32), nxt.astype(jnp.int32), off.astype(jnp.int32), tok_pad.astype(jnp.int32),
            slot_of.astype(jnp.int32), n_used.astype(jnp.int32))


WEIGHT_CHUNKS = 4
GATHER_AHEAD = 2
GATHER_BUFS = GATHER_AHEAD + 1


def _expert_kernel(be_ref, nxt_ref, off_ref, tok_ref, nu_ref, h_hbm, wg_hbm, wu_hbm, wd_hbm, y_ref,
                   xbuf0, xbuf1, xbuf2, xs, stg_g, stg_u, stg_d, wg_s, wu_s, wd_s, sem, wsem, *, layer):
    bm = EXPERT_BLOCK_ROWS
    b = pl.program_id(0)
    n_used = nu_ref[0]
    bufs = (xbuf0, xbuf1, xbuf2)

    def weight_copies(e):
        copies = []
        for src, dst in ((wg_hbm, stg_g), (wu_hbm, stg_u), (wd_hbm, stg_d)):
            rows = dst.shape[0] // WEIGHT_CHUNKS
            for c in range(WEIGHT_CHUNKS):
                copies.append(pltpu.make_async_copy(src.at[layer, e, pl.ds(c * rows, rows), :],
                                                    dst.at[pl.ds(c * rows, rows), :], wsem.at[0]))
        return copies

    def start_gather(blk, slot, unrolled):
        base = off_ref[blk]

        def one(r, priority):
            src = pl.multiple_of(tok_ref[base + r] * PACK_SLABS, PACK_SLABS)
            pltpu.make_async_copy(h_hbm.at[pl.ds(src, PACK_SLABS), :],
                                  bufs[slot].at[pl.ds(r * VMEM_PITCH, PACK_SLABS), :],
                                  sem.at[slot]).start(priority=priority)

        if unrolled:
            for r in range(bm):
                one(r, r % 2)
        else:
            def body(i, carry):
                one(2 * i, 0)
                one(2 * i + 1, 1)
                return carry
            lax.fori_loop(0, bm // 2, body, 0)

    def wait_gather(slot):
        pltpu.make_async_copy(h_hbm.at[pl.ds(0, bm * PACK_SLABS), :],
                              bufs[slot].at[pl.ds(0, bm * PACK_SLABS), :], sem.at[slot]).wait()

    @pl.when(b == 0)
    def _():
        for ahead in range(GATHER_AHEAD):
            start_gather(ahead, ahead, False)
        for cp in weight_copies(be_ref[0]):
            cp.start()

    new_expert = (b == 0) | (be_ref[b] != be_ref[jnp.maximum(b - 1, 0)])

    @pl.when(new_expert & (b < n_used))
    def _():
        for cp in weight_copies(be_ref[b]):
            cp.wait()
        wg_s[...] = stg_g[...].astype(BF16)
        wu_s[...] = stg_u[...].astype(BF16)
        wd_s[...] = stg_d[...].astype(BF16)

        @pl.when(nxt_ref[b] >= 0)
        def _():
            for cp in weight_copies(nxt_ref[b]):
                cp.start()

    def work(slot):
        wait_gather(slot)
        for j in range(PACK_SLABS):
            lo, hi = _unpack_pairs(_load_slab(bufs[slot], 0, bm, j))
            xs[:, j * LANES:(j + 1) * LANES] = lo.astype(BF16)
            xs[:, PACK_HALF + j * LANES:PACK_HALF + (j + 1) * LANES] = hi.astype(BF16)
        start_gather(b + GATHER_AHEAD, (slot + GATHER_AHEAD) % GATHER_BUFS, True)
        x = xs[...]
        g = jnp.dot(x, wg_s[...], preferred_element_type=F32)
        u = jnp.dot(x, wu_s[...], preferred_element_type=F32)
        a = (g * jax.nn.sigmoid(g) * u).astype(BF16)
        _store_packed(y_ref, jnp.dot(a, wd_s[...], preferred_element_type=F32))

    for slot in range(GATHER_BUFS):
        @pl.when((b < n_used) & (b % GATHER_BUFS == slot))
        def _(slot=slot):
            work(slot)

        @pl.when((b >= n_used) & (b < n_used + GATHER_AHEAD) & (b % GATHER_BUFS == slot))
        def _(slot=slot):
            wait_gather(slot)

    @pl.when(b >= n_used)
    def _():
        y_ref[...] = jnp.zeros(y_ref.shape, y_ref.dtype)


def _expert_ffn(h_packed, w_gate, w_up, w_down, layer, block_expert, next_expert, block_off, tok_pad,
                n_used):
    bm = EXPERT_BLOCK_ROWS
    D = D_MODEL
    n_steps = block_expert.shape[0]
    any_spec = pl.BlockSpec(memory_space=pl.ANY)
    grid_spec = pltpu.PrefetchScalarGridSpec(
        num_scalar_prefetch=5,
        grid=(n_steps,),
        in_specs=[any_spec, any_spec, any_spec, any_spec],
        out_specs=pl.BlockSpec((bm * PACK_SLABS, LANES), lambda b, *_: (b, 0)),
        scratch_shapes=[pltpu.VMEM((bm * VMEM_PITCH, LANES), jnp.uint32)] * GATHER_BUFS + [
                        pltpu.VMEM((bm, D), BF16),
                        pltpu.VMEM((D, EXPERT_FF), F32), pltpu.VMEM((D, EXPERT_FF), F32),
                        pltpu.VMEM((EXPERT_FF, D), F32),
                        pltpu.VMEM((D, EXPERT_FF), BF16), pltpu.VMEM((D, EXPERT_FF), BF16),
                        pltpu.VMEM((EXPERT_FF, D), BF16),
                        pltpu.SemaphoreType.DMA((GATHER_BUFS,)), pltpu.SemaphoreType.DMA((1,))],
    )
    return pl.pallas_call(
        functools.partial(_expert_kernel, layer=layer),
        grid_spec=grid_spec,
        out_shape=jax.ShapeDtypeStruct((n_steps * bm * PACK_SLABS, LANES), jnp.uint32),
        compiler_params=_cparams(("arbitrary",)),
        name="expert_ffn",
    )(block_expert, next_expert, block_off, tok_pad, n_used.reshape(1), h_packed, w_gate, w_up, w_down)


def _combine_kernel(slot_ref, y_hbm, x_ref, sh_ref, w_ref, mod_ref, *refs, gate_row, n_tokens, tail):
    refs = list(refs)
    ng_ref = refs.pop(0) if tail else None
    nmod_ref = refs.pop(0) if tail == "next" else None
    o_ref = refs.pop(0)
    h_ref = refs.pop(0) if tail == "next" else None
    ybuf0, ybuf1, ybuf2, sem = refs
    tc = COMBINE_TOKENS
    i = pl.program_id(0)
    n_steps = pl.num_programs(0)
    bufs = (ybuf0, ybuf1, ybuf2)

    def start_gather(step, slot, unrolled):
        base = step * tc

        def one(t, k):
            src = pl.multiple_of(slot_ref[k * n_tokens + base + t] * PACK_SLABS, PACK_SLABS)
            pltpu.make_async_copy(y_hbm.at[pl.ds(src, PACK_SLABS), :],
                                  bufs[slot].at[pl.ds((k * tc + t) * VMEM_PITCH, PACK_SLABS), :],
                                  sem.at[slot]).start(priority=k % 2)

        if unrolled:
            for t in range(tc):
                for k in range(TOP_K):
                    one(t, k)
        else:
            def body(t, carry):
                for k in range(TOP_K):
                    one(t, k)
                return carry
            lax.fori_loop(0, tc, body, 0)

    @pl.when(i == 0)
    def _():
        for ahead in range(GATHER_AHEAD):
            start_gather(ahead, ahead, False)

    def work(slot, prefetch):
        n_rows = TOP_K * tc * PACK_SLABS
        pltpu.make_async_copy(y_hbm.at[pl.ds(0, n_rows), :], bufs[slot].at[pl.ds(0, n_rows), :],
                              sem.at[slot]).wait()
        if prefetch:
            start_gather(i + GATHER_AHEAD, (slot + GATHER_AHEAD) % GATHER_BUFS, True)
        w = w_ref[...]
        wk = [w[:, k:k + 1] for k in range(TOP_K)]
        for j in range(PACK_SLABS):
            c_lo = slice(j * LANES, (j + 1) * LANES)
            c_hi = slice(PACK_HALF + j * LANES, PACK_HALF + (j + 1) * LANES)
            acc_lo = sh_ref[:, c_lo].astype(F32)
            acc_hi = sh_ref[:, c_hi].astype(F32)
            for k in range(TOP_K):
                lo, hi = _unpack_pairs(_load_slab(bufs[slot], k * tc * VMEM_PITCH, tc, j))
                acc_lo = acc_lo + wk[k] * lo
                acc_hi = acc_hi + wk[k] * hi
            o_ref[:, c_lo] = x_ref[:, c_lo] + mod_ref[0, gate_row:gate_row + 1, c_lo] * acc_lo
            o_ref[:, c_hi] = x_ref[:, c_hi] + mod_ref[0, gate_row:gate_row + 1, c_hi] * acc_hi
        if tail:
            xo = o_ref[...]
            y = xo * lax.rsqrt(jnp.mean(xo * xo, axis=-1, keepdims=True) + NORM_EPS) * ng_ref[...]
            if tail == "final":
                o_ref[...] = y
            else:
                h_ref[...] = (y * (1.0 + nmod_ref[0, 1:2, :]) + nmod_ref[0, 0:1, :]).astype(h_ref.dtype)

    for slot in range(GATHER_BUFS):
        for prefetch in (True, False):
            @pl.when((i % GATHER_BUFS == slot) & ((i + GATHER_AHEAD < n_steps) == prefetch))
            def _(slot=slot, prefetch=prefetch):
                work(slot, prefetch)


def _combine(y_packed, slot_of, x, shared, w_tok, mod, gate_row, seq, next_norm=None, final_norm_g=None):
    N, D = x.shape
    tc = COMBINE_TOKENS
    per_b = seq // tc
    row = pl.BlockSpec((tc, D), lambda i, s: (i, 0))
    mod_spec = pl.BlockSpec((1, 6, D), lambda i, s: (i // per_b, 0, 0))
    gain_spec = pl.BlockSpec((1, D), lambda i, s: (0, 0))
    in_specs = [pl.BlockSpec(memory_space=pl.ANY), row, row,
                pl.BlockSpec((tc, TOP_K), lambda i, s: (i, 0)), mod_spec]
    args = [slot_of.reshape(-1), y_packed, x, shared, w_tok, mod]
    out_specs, out_shape, tail = [row], [jax.ShapeDtypeStruct((N, D), F32)], None
    if next_norm is not None:
        tail = "next"
        in_specs += [gain_spec, mod_spec]
        args += [next_norm[0].reshape(1, D), next_norm[1]]
        out_specs.append(row)
        out_shape.append(jax.ShapeDtypeStruct((N, D), BF16))
    elif final_norm_g is not None:
        tail = "final"
        in_specs.append(gain_spec)
        args.append(final_norm_g.reshape(1, D))
    grid_spec = pltpu.PrefetchScalarGridSpec(
        num_scalar_prefetch=1,
        grid=(N // tc,),
        in_specs=in_specs,
        out_specs=out_specs,
        scratch_shapes=[pltpu.VMEM((TOP_K * tc * VMEM_PITCH, LANES), jnp.uint32)] * GATHER_BUFS
        + [pltpu.SemaphoreType.DMA((GATHER_BUFS,))],
    )
    return pl.pallas_call(
        functools.partial(_combine_kernel, gate_row=gate_row, n_tokens=N, tail=tail),
        grid_spec=grid_spec,
        out_shape=out_shape,
        compiler_params=_cparams(("arbitrary",)),
        name="expert_combine",
    )(*args)


def _mla_weights(w_uq, w_ukv):
    qd = MLA_NOPE_DIM + MLA_ROPE_DIM
    wq = w_uq.reshape(MLA_Q_RANK, MLA_HEADS, qd)
    wq = jnp.concatenate([wq, jnp.zeros((MLA_Q_RANK, MLA_HEADS, MLA_QW - qd), wq.dtype)], axis=-1)
    wkv = w_ukv.reshape(MLA_KV_RANK, MLA_HEADS, MLA_NOPE_DIM + MLA_V_DIM)
    wkv = jnp.concatenate([wkv[:, :, :MLA_NOPE_DIM].reshape(MLA_KV_RANK, -1),
                           wkv[:, :, MLA_NOPE_DIM:].reshape(MLA_KV_RANK, -1)], axis=-1)
    return wq.reshape(MLA_Q_RANK, MLA_HEADS * MLA_QW).astype(BF16), wkv.astype(BF16)


def kernel(x, c, positions, w_ada, b_ada, mix_norm_g, ffn_norm_g, w_in, mla_q_norm_g, mla_w_uq,
           mla_kv_norm_g, mla_w_ukv, diff_lambda, diff_subln_g, swa_sink, w_branch, w_out,
           router_w, router_bias, expert_w_gate, expert_w_up, expert_w_down,
           shared_w_gate, shared_w_up, shared_w_down, final_norm_g):
    B, S, D = x.shape
    N = B * S
    xf = x.reshape(N, D)
    mod_all = _ada_mod(c, w_ada, b_ada)
    w_in_t = jnp.swapaxes(w_in, 1, 2)
    tabs = _rope_tables(positions)
    c64, s64 = tabs[2], tabs[3]

    mods = [mod_all[l, :B].reshape(B, 6, D) for l in range(DEPTH)]
    h = _norm_mod(xf, mix_norm_g[0], mods[0], 0, 1, S)
    for l in range(DEPTH):
        mod = mods[l]

        proj_r = _matmul_w_in(h, w_in_t, l, R_TILE_COLS, "proj_rotary")
        proj_x = _matmul_w_in(h, w_in_t, l, X_TILE_COLS, "proj_dilated")
        proj_v = _matmul_w_in(h, w_in_t, l, V_TILE_COLS, "proj_value")
        proj_a = _matmul_w_in(h, w_in_t, l, A_TILE_COLS, "proj_latent", tn=A_PAD // 3)
        gates = _matmul_w_in(h, w_in_t, l, G_TILE_COLS, "proj_gates")
        rop = _rope_section(proj_r, proj_v, tabs)
        dil_groups = _dil_rope_section(proj_x, tabs[0], tabs[1], B, S, tuple(d for _, d in DIL_PAIRS[1:]))

        wq, wkv = _mla_weights(mla_w_uq[l], mla_w_ukv[l])
        qm, kn, kr, vm = _mla_prep(proj_a, mla_q_norm_g[l], mla_kv_norm_g[l], wq, wkv, c64, s64)
        out_a = _mla_attention(qm, kn, kr, vm, B, S)

        lam_init = 0.8 - 0.6 * math.exp(-0.3 * l)
        out_b = _diff_attention(rop, proj_v, diff_lambda[l], diff_subln_g[l], lam_init, B, S)

        sink = jnp.repeat(swa_sink[l].astype(F32), LANES).reshape(1, SWA_HEADS * LANES)
        (out_c,) = _banded_attention(
            rop, rop, proj_v, sub_len=S, half_window=SWA_HALF_WINDOW,
            q_col=R_SWA_Q // HW4, k_col=R_SWA_K // LANES, v_col=V_SWA // LANES,
            n_col_groups=SWA_KV_HEADS, kv_width=LANES, grp=SWA_HEADS // SWA_KV_HEADS, sink=sink)

        window0, _ = DIL_PAIRS[0]
        d_res = list(_banded_attention(
            rop, rop, proj_v, sub_len=S, half_window=window0 // 2, q_col=R_DIL0_Q // HW4,
            k_col=R_DIL0_K // HW4, v_col=V_DIL0 // HW4, n_col_groups=1, kv_width=HW4, grp=1, want_lse=True))
        for (window, dil), (qg, kg, vg) in zip(DIL_PAIRS[1:], dil_groups):
            flat = lambda a: a.reshape(N, HW4)
            d_res += _banded_attention(
                flat(qg), flat(kg), flat(vg), sub_len=S // dil, half_window=window // (2 * dil),
                q_col=0, k_col=0, v_col=0, n_col_groups=1, kv_width=HW4, grp=1, want_lse=True)
        out_d = _dil_combine(*d_res, B, S, tuple(d for _, d in DIL_PAIRS[1:]))

        merged = _merge((out_a, out_b, out_c, out_d), w_branch, gates, l)
        xf = _matmul_residual(merged, w_out, xf, mod, 2, S, layer=l)

        hb, hp, top_idx, top_w = _router(xf, ffn_norm_g[l], mod, 3, 4, router_w[l], router_bias[l], S)
        block_expert, next_expert, block_off, tok_pad, slot_of, n_used = _dispatch_plan(top_idx)
        y = _expert_ffn(hp, expert_w_gate, expert_w_up, expert_w_down, l, block_expert, next_expert,
                        block_off, tok_pad, n_used)
        act = _matmul_glu(hb, shared_w_gate, shared_w_up, layer=l)
        shared = _matmul(act, shared_w_down, tn=1024, name="shared_down", layer=l)
        if l + 1 < DEPTH:
            xf, h = _combine(y, slot_of, xf, shared, top_w.T, mod, 5, S,
                             next_norm=(mix_norm_g[l + 1], mods[l + 1]))
        else:
            (out,) = _combine(y, slot_of, xf, shared, top_w.T, mod, 5, S, final_norm_g=final_norm_g)

    return out.reshape(B, S, D)
```

```python
import functools
import math

import numpy as np
import jax
import jax.numpy as jnp
from jax import lax
from jax.experimental import pallas as pl
from jax.experimental.pallas import tpu as pltpu

F32 = jnp.float32
BF16 = jnp.bfloat16

D_MODEL = 4096
DEPTH = 2
HEAD_DIM = 128
ROPE_THETA = 10000.0
NORM_EPS = 1e-6

MLA_HEADS = 8
MLA_Q_RANK = 768
MLA_KV_RANK = 256
MLA_NOPE_DIM = 128
MLA_ROPE_DIM = 64
MLA_V_DIM = 128

DIFF_HEADS = 8
DIFF_QK_DIM = 64
DIFF_V_DIM = 128
DIFF_NORM_EPS = 1e-5

SWA_HEADS = 8
SWA_KV_HEADS = 2
SWA_HALF_WINDOW = 128

DIL_PAIRS = ((128, 1), (512, 4), (2048, 16))
DIL_HEADS = 4

N_EXPERTS = 64
N_EXPERT_GROUPS = 8
TOP_GROUPS = 4
TOP_K = 8
EXPERT_FF = 256
SHARED_FF = 1024
ROUTED_SCALE = 2.5

A_COLS = MLA_Q_RANK + MLA_KV_RANK + MLA_ROPE_DIM
B_COLS = 3 * DIFF_HEADS * DIFF_V_DIM
C_COLS = (SWA_HEADS + 2 * SWA_KV_HEADS) * HEAD_DIM
D_GROUP_COLS = 3 * DIL_HEADS * HEAD_DIM
D_COLS = len(DIL_PAIRS) * D_GROUP_COLS
OFF_B = A_COLS
OFF_C = OFF_B + B_COLS
OFF_D = OFF_C + C_COLS
OFF_G = OFF_D + D_COLS

LANES = 128
HW4 = DIL_HEADS * HEAD_DIM

R_DIL0_Q = 0
R_DIL0_K = HW4
R_DIFF_Q = 2 * HW4
R_DIFF_K = R_DIFF_Q + 1024
R_SWA_Q = R_DIFF_K + 1024
R_IN_COLS = R_SWA_Q + 1024
R_SWA_K = R_IN_COLS
R_COLS = R_SWA_K + 256
X_GROUP = 3 * HW4
X_COLS = 2 * X_GROUP
V_DIL0 = 0
V_DIFF = HW4
V_SWA_K = V_DIFF + 1024
V_SWA = V_SWA_K + 256
V_COLS = 2048
A_PAD = 1152
PROJ_TILE = 512

R_TILE_COLS = ([OFF_D, OFF_D + 512] + [OFF_B + 512 * t for t in range(4)] + [OFF_C, OFF_C + 512])
X_TILE_COLS = [OFF_D + D_GROUP_COLS + 512 * t for t in range(6)]
V_TILE_COLS = [OFF_D + 2 * HW4, OFF_B + 2048, OFF_B + 2560, OFF_C + 1024]
G_TILE_COLS = [OFF_G + 512 * t for t in range(4 * D_MODEL // 512)]
A_TILE_COLS = [0, A_PAD // 3, 2 * A_PAD // 3]

EXPERT_BLOCK_ROWS = 256
COMBINE_TOKENS = 32
PACK_HALF = D_MODEL // 2
PACK_SLABS = PACK_HALF // LANES
VMEM_LIMIT = 56 * 1024 * 1024


def _cparams(sem, vmem=VMEM_LIMIT):
    return pltpu.CompilerParams(dimension_semantics=sem, vmem_limit_bytes=vmem)


def _nt_dot(a, b):
    return lax.dot_general(a, b, (((1,), (1,)), ((), ())), preferred_element_type=F32)


ADA_ROWS = 128
ADA_STREAMS = 16


def _ada_kernel(ct_ref, *refs, batch):
    w_refs, (b_ref, o_ref, acc_ref) = refs[:ADA_STREAMS], refs[ADA_STREAMS:]
    k = pl.program_id(1)
    ct = ct_ref[...]
    s = ct * jax.nn.sigmoid(ct)
    sub = ADA_ROWS // ADA_STREAMS

    @pl.when(k == 0)
    def _():
        acc_ref[...] = jnp.zeros(acc_ref.shape, acc_ref.dtype)

    chunk = 16 * LANES
    scol = [[s[i * sub:(i + 1) * sub, b:b + 1] for i in range(ADA_STREAMS)] for b in range(batch)]
    for c0 in range(0, acc_ref.shape[-1], chunk):
        cols = slice(c0, c0 + chunk)
        accs = [acc_ref[b, :, cols] for b in range(batch)]
        for i, w_ref in enumerate(w_refs):
            w = w_ref[0, :, cols]
            accs = [accs[b] + w * scol[b][i] for b in range(batch)]
        for b in range(batch):
            acc_ref[b, :, cols] = accs[b]

    @pl.when(k == pl.num_programs(1) - 1)
    def _():
        rows = [jnp.sum(acc_ref[b], axis=0, keepdims=True) for b in range(batch)]
        rows += [jnp.zeros_like(rows[0])] * (8 - batch)
        o_ref[0] = jnp.concatenate(rows, axis=0) + b_ref[0]


def _ada_mod(c, w_ada, b_ada):
    B, D = c.shape
    ct = jnp.zeros((D, 8), F32).at[:, :B].set(c.T)
    n6 = w_ada.shape[-1]
    sub = ADA_ROWS // ADA_STREAMS
    w_specs = [pl.BlockSpec((1, sub, n6), lambda l, k, i=i: (l, k * ADA_STREAMS + i, 0))
               for i in range(ADA_STREAMS)]
    return pl.pallas_call(
        functools.partial(_ada_kernel, batch=B),
        grid=(DEPTH, D // ADA_ROWS),
        in_specs=[pl.BlockSpec((ADA_ROWS, 8), lambda l, k: (k, 0))] + w_specs
        + [pl.BlockSpec((1, 1, n6), lambda l, k: (l, 0, 0))],
        out_specs=pl.BlockSpec((1, 8, n6), lambda l, k: (l, 0, 0)),
        out_shape=jax.ShapeDtypeStruct((DEPTH, 8, n6), F32),
        scratch_shapes=[pltpu.VMEM((B, sub, n6), F32)],
        compiler_params=_cparams(("arbitrary", "arbitrary")),
        name="ada_mod",
    )(ct, *([w_ada] * ADA_STREAMS), b_ada.reshape(DEPTH, 1, n6))


def _norm_mod_kernel(x_ref, g_ref, mod_ref, o_ref, *, shift_row, scale_row):
    x = x_ref[...]
    ms = jnp.mean(x * x, axis=-1, keepdims=True)
    y = x * lax.rsqrt(ms + NORM_EPS) * g_ref[...]
    h = y * (1.0 + mod_ref[0, scale_row:scale_row + 1, :]) + mod_ref[0, shift_row:shift_row + 1, :]
    o_ref[...] = h.astype(o_ref.dtype)


def _norm_mod(x, g, mod, shift_row, scale_row, seq, tm=256):
    N, D = x.shape
    tm = min(tm, seq)
    per_b = seq // tm
    return pl.pallas_call(
        functools.partial(_norm_mod_kernel, shift_row=shift_row, scale_row=scale_row),
        grid=(N // tm,),
        in_specs=[
            pl.BlockSpec((tm, D), lambda i: (i, 0)),
            pl.BlockSpec((1, D), lambda i: (0, 0)),
            pl.BlockSpec((1, 6, D), lambda i: (i // per_b, 0, 0)),
        ],
        out_specs=pl.BlockSpec((tm, D), lambda i: (i, 0)),
        out_shape=jax.ShapeDtypeStruct((N, D), BF16),
        compiler_params=_cparams(("arbitrary",)),
        name="norm_mod",
    )(x, g.reshape(1, D), mod)


def _weight_scratch(*weights_and_tiles):
    return [pltpu.VMEM(tile, BF16) for w, tile in weights_and_tiles if w.dtype != BF16]


def _weight_spec(w, layer, rows, tn, row_block=0):
    if w.ndim == 2:
        return pl.BlockSpec((rows, tn), lambda j, i: (row_block, j))
    return pl.BlockSpec((None, rows, tn), lambda j, i: (layer, row_block, j))


def _resident_bf16(b_ref, scratch):
    if b_ref.dtype == BF16:
        return b_ref[...]
    s_ref = scratch.pop(0)

    @pl.when(pl.program_id(1) == 0)
    def _():
        s_ref[...] = b_ref[...].astype(BF16)

    return s_ref[...]


def _mm_kernel(a_ref, b_ref, o_ref, *scratch):
    b = _resident_bf16(b_ref, list(scratch))
    o_ref[...] = jnp.dot(a_ref[...], b, preferred_element_type=F32).astype(o_ref.dtype)


def _matmul(a, b, tn, out_dtype=BF16, tm=1024, name="matmul", layer=0, n_cols=None):
    M, K = a.shape
    Nc = b.shape[-1] if n_cols is None else n_cols
    tm = min(tm, M)
    return pl.pallas_call(
        _mm_kernel,
        grid=(Nc // tn, M // tm),
        in_specs=[pl.BlockSpec((tm, K), lambda j, i: (i, 0)), _weight_spec(b, layer, K, tn)],
        out_specs=pl.BlockSpec((tm, tn), lambda j, i: (i, j)),
        out_shape=jax.ShapeDtypeStruct((M, Nc), out_dtype),
        scratch_shapes=_weight_scratch((b, (K, tn))),
        compiler_params=_cparams(("arbitrary", "arbitrary")),
        name=name,
    )(a, b)


W_IN_ROW_ALIGN = 64


def _mm_rows_kernel(starts_ref, a_ref, bt_ref, o_ref, scr):
    @pl.when(pl.program_id(1) == 0)
    def _():
        scr[...] = bt_ref[...].astype(BF16)

    o_ref[...] = _nt_dot(a_ref[...], scr[...]).astype(o_ref.dtype)


def _matmul_w_in(a, w_in_t, layer, tile_cols, name, tn=PROJ_TILE, tm=1024):
    M, K = a.shape
    tm = min(tm, M)
    assert all(c % W_IN_ROW_ALIGN == 0 for c in tile_cols)
    grid_spec = pltpu.PrefetchScalarGridSpec(
        num_scalar_prefetch=1,
        grid=(len(tile_cols), M // tm),
        in_specs=[
            pl.BlockSpec((tm, K), lambda j, i, s: (i, 0)),
            pl.BlockSpec((None, pl.Element(tn), pl.Element(K)),
                         lambda j, i, s: (layer, pl.multiple_of(s[j], W_IN_ROW_ALIGN), 0)),
        ],
        out_specs=pl.BlockSpec((tm, tn), lambda j, i, s: (i, j)),
        scratch_shapes=[pltpu.VMEM((tn, K), BF16)],
    )
    return pl.pallas_call(
        _mm_rows_kernel,
        grid_spec=grid_spec,
        out_shape=jax.ShapeDtypeStruct((M, len(tile_cols) * tn), BF16),
        compiler_params=_cparams(("arbitrary", "arbitrary")),
        name=name,
    )(jnp.asarray(tile_cols, jnp.int32), a, w_in_t)


def _mm_glu_kernel(a_ref, bg_ref, bu_ref, o_ref, *scratch):
    scratch = list(scratch)
    bg = _resident_bf16(bg_ref, scratch)
    bu = _resident_bf16(bu_ref, scratch)
    a = a_ref[...]
    g = jnp.dot(a, bg, preferred_element_type=F32)
    u = jnp.dot(a, bu, preferred_element_type=F32)
    o_ref[...] = (g * jax.nn.sigmoid(g) * u).astype(o_ref.dtype)


def _matmul_glu(a, bg, bu, tn=256, tm=1024, layer=0):
    M, K = a.shape
    Nc = bg.shape[-1]
    tm = min(tm, M)
    return pl.pallas_call(
        _mm_glu_kernel,
        grid=(Nc // tn, M // tm),
        in_specs=[
            pl.BlockSpec((tm, K), lambda j, i: (i, 0)),
            _weight_spec(bg, layer, K, tn),
            _weight_spec(bu, layer, K, tn),
        ],
        out_specs=pl.BlockSpec((tm, tn), lambda j, i: (i, j)),
        out_shape=jax.ShapeDtypeStruct((M, Nc), BF16),
        scratch_shapes=_weight_scratch((bg, (K, tn)), (bu, (K, tn))),
        compiler_params=_cparams(("arbitrary", "arbitrary")),
        name="shared_glu",
    )(a, bg, bu)


def _mm_residual_kernel(a_ref, b_ref, x_ref, mod_ref, o_ref, *scratch, gate_row):
    b = _resident_bf16(b_ref, list(scratch))
    acc = jnp.dot(a_ref[...], b, preferred_element_type=F32)
    o_ref[...] = x_ref[...] + mod_ref[0, gate_row:gate_row + 1, :] * acc


def _matmul_residual(a, b, x, mod, gate_row, seq, tn=512, tm=1024, layer=0):
    M, K = a.shape
    Nc = b.shape[-1]
    tm = min(tm, seq)
    per_b = seq // tm
    return pl.pallas_call(
        functools.partial(_mm_residual_kernel, gate_row=gate_row),
        grid=(Nc // tn, M // tm),
        in_specs=[
            pl.BlockSpec((tm, K), lambda j, i: (i, 0)),
            _weight_spec(b, layer, K, tn),
            pl.BlockSpec((tm, tn), lambda j, i: (i, j)),
            pl.BlockSpec((1, 6, tn), lambda j, i: (i // per_b, 0, j)),
        ],
        out_specs=pl.BlockSpec((tm, tn), lambda j, i: (i, j)),
        out_shape=jax.ShapeDtypeStruct((M, Nc), F32),
        scratch_shapes=_weight_scratch((b, (K, tn))),
        compiler_params=_cparams(("arbitrary", "arbitrary")),
        name="out_proj_residual",
    )(a, b, x, mod)


def _rope_table_kernel(pos_ref, inv128_ref, inv64_ref, c128_ref, s128_ref, c64_ref, s64_ref):
    pos = pos_ref[...]
    lane = lax.broadcasted_iota(jnp.int32, pos.shape, 1)
    a = pos * inv128_ref[...]
    c128_ref[...] = jnp.cos(a)
    s128_ref[...] = jnp.where(lane < 64, -jnp.sin(a), jnp.sin(a))
    a = pos * inv64_ref[...]
    c64_ref[...] = jnp.cos(a)
    s64_ref[...] = jnp.where((lane & 63) < 32, -jnp.sin(a), jnp.sin(a))


def _rope_tables(positions, tm=512):
    N = positions.size
    tm = min(tm, N)
    pos = jnp.broadcast_to(positions.astype(F32).reshape(N, 1), (N, LANES))

    def inv(half):
        f = np.float32(ROPE_THETA) ** (-(np.arange(half, dtype=np.float32) / np.float32(half)))
        return jnp.asarray(np.tile(f.astype(np.float32), LANES // half).reshape(1, LANES))

    tab = jax.ShapeDtypeStruct((N, LANES), F32)
    row = pl.BlockSpec((tm, LANES), lambda i: (i, 0))
    one = pl.BlockSpec((1, LANES), lambda i: (0, 0))
    return pl.pallas_call(
        _rope_table_kernel,
        grid=(N // tm,),
        in_specs=[row, one, one],
        out_specs=[row, row, row, row],
        out_shape=[tab, tab, tab, tab],
        compiler_params=_cparams(("arbitrary",)),
        name="rope_tables",
    )(pos, inv(64), inv(32))


def _rot128(x):
    return pltpu.roll(x, 64, 1)


def _rot64(x):
    lane = lax.broadcasted_iota(jnp.int32, x.shape, 1)
    return jnp.where((lane & 63) < 32, pltpu.roll(x, 96, 1), pltpu.roll(x, 32, 1))


def _r_block_plan():
    plan = [(128, HEAD_DIM ** -0.5)] * 4 + [(128, 1.0)] * 4
    plan += [(64, DIFF_QK_DIM ** -0.5)] * 8 + [(64, 1.0)] * 8
    plan += [(128, HEAD_DIM ** -0.5)] * 8 + [(128, 1.0)] * 2
    return plan


def _rope_kernel(p_ref, swak_ref, c128_ref, s128_ref, c64_ref, s64_ref, o_ref):
    c128, s128, c64, s64 = c128_ref[...], s128_ref[...], c64_ref[...], s64_ref[...]
    n_in = R_IN_COLS // LANES
    for blk, (flavour, scale) in enumerate(_r_block_plan()):
        if blk < n_in:
            x = p_ref[:, blk * LANES:(blk + 1) * LANES].astype(F32)
        else:
            x = swak_ref[:, (blk - n_in) * LANES:(blk - n_in + 1) * LANES].astype(F32)
        if flavour == 128:
            y = x * c128 + _rot128(x) * s128
        else:
            y = x * c64 + _rot64(x) * s64
        if scale != 1.0:
            y = y * scale
        o_ref[:, blk * LANES:(blk + 1) * LANES] = y.astype(o_ref.dtype)


def _rope_section(proj_r, proj_v, tabs, tm=256):
    N = proj_r.shape[0]
    tm = min(tm, N)
    tab = pl.BlockSpec((tm, LANES), lambda i: (i, 0))
    return pl.pallas_call(
        _rope_kernel,
        grid=(N // tm,),
        in_specs=[pl.BlockSpec((tm, R_IN_COLS), lambda i: (i, 0)),
                  pl.BlockSpec((tm, 256), lambda i: (i, V_SWA_K // 256)), tab, tab, tab, tab],
        out_specs=pl.BlockSpec((tm, R_COLS), lambda i: (i, 0)),
        out_shape=jax.ShapeDtypeStruct((N, R_COLS), BF16),
        compiler_params=_cparams(("arbitrary",)),
        name="rope_section",
    )(proj_r, proj_v, *tabs)


def _dil_rope_kernel(x_ref, c128_ref, s128_ref, *refs, dils, tm):
    scr = refs[-1]
    outs = refs[:-1]
    c128, s128 = c128_ref[...], s128_ref[...]
    for g, d in enumerate(dils):
        for part in range(3):
            o_ref = outs[g * 3 + part]
            base = (g * 3 + part) * DIL_HEADS
            for c in range(DIL_HEADS):
                col = g * X_GROUP + part * HW4 + c * LANES
                x = x_ref[:, col:col + LANES].astype(F32)
                if part < 2:
                    x = x * c128 + _rot128(x) * s128
                if part == 0:
                    x = x * (HEAD_DIM ** -0.5)
                scr[base + c] = x
            for r in range(d):
                for c in range(DIL_HEADS):
                    o_ref[r, :, c * LANES:(c + 1) * LANES] = (
                        scr[base + c, pl.ds(r, tm // d, stride=d), :].astype(o_ref.dtype))


def _dil_rope_section(proj_x, c128, s128, batch, seq, dils, tm=256):
    N = proj_x.shape[0]
    tm = min(tm, seq)
    per_b = seq // tm
    tab = pl.BlockSpec((tm, LANES), lambda i: (i, 0))
    out_specs, out_shape = [], []
    for d in dils:
        for _ in range(3):
            out_specs.append(pl.BlockSpec((d, tm // d, HW4), lambda i: (i // per_b, i % per_b, 0)))
            out_shape.append(jax.ShapeDtypeStruct((batch * d, seq // d, HW4), BF16))
    outs = pl.pallas_call(
        functools.partial(_dil_rope_kernel, dils=dils, tm=tm),
        grid=(N // tm,),
        in_specs=[pl.BlockSpec((tm, X_COLS), lambda i: (i, 0)), tab, tab],
        out_specs=out_specs,
        out_shape=out_shape,
        scratch_shapes=[pltpu.VMEM((len(dils) * 3 * DIL_HEADS, tm, LANES), F32)],
        compiler_params=_cparams(("arbitrary",)),
        name="dil_rope_section",
    )(proj_x, c128, s128)
    return [outs[3 * g:3 * g + 3] for g in range(len(dils))]


MLA_QW = 256


def _mla_prep_kernel(a_ref, gq_ref, gkv_ref, wq_ref, wkv_ref, c64_ref, s64_ref,
                     q_ref, kn_ref, kr_ref, v_ref):
    a = a_ref[...].astype(F32)
    c64, s64 = c64_ref[...], s64_ref[...]
    scale = (MLA_NOPE_DIM + MLA_ROPE_DIM) ** -0.5

    cq = a[:, :MLA_Q_RANK]
    cq = cq * lax.rsqrt(jnp.mean(cq * cq, axis=-1, keepdims=True) + NORM_EPS) * gq_ref[...]
    q = jnp.dot(cq.astype(BF16), wq_ref[...], preferred_element_type=F32)
    for h in range(MLA_HEADS):
        qn = q[:, h * MLA_QW:h * MLA_QW + LANES]
        qr = q[:, h * MLA_QW + LANES:(h + 1) * MLA_QW]
        qr = qr * c64 + _rot64(qr) * s64
        q_ref[:, h * MLA_QW:h * MLA_QW + LANES] = (qn * scale).astype(q_ref.dtype)
        q_ref[:, h * MLA_QW + LANES:(h + 1) * MLA_QW] = (qr * scale).astype(q_ref.dtype)

    ckv = a[:, MLA_Q_RANK:MLA_Q_RANK + MLA_KV_RANK]
    ckv = ckv * lax.rsqrt(jnp.mean(ckv * ckv, axis=-1, keepdims=True) + NORM_EPS) * gkv_ref[...]
    kv = jnp.dot(ckv.astype(BF16), wkv_ref[...], preferred_element_type=F32)
    kn_ref[...] = kv[:, :MLA_HEADS * MLA_NOPE_DIM].astype(kn_ref.dtype)
    v_ref[...] = kv[:, MLA_HEADS * MLA_NOPE_DIM:].astype(v_ref.dtype)

    kr = a[:, MLA_Q_RANK + MLA_KV_RANK:]
    lane = lax.broadcasted_iota(jnp.int32, kr.shape, 1)
    kr = jnp.where(lane < MLA_ROPE_DIM, kr, 0.0)
    kr_ref[...] = (kr * c64 + _rot64(kr) * s64).astype(kr_ref.dtype)


def _mla_prep(proj_a, gq, gkv, wq, wkv, c64, s64, tm=256):
    N = proj_a.shape[0]
    tm = min(tm, N)
    row = lambda w: pl.BlockSpec((tm, w), lambda i: (i, 0))
    full = lambda r, c: pl.BlockSpec((r, c), lambda i: (0, 0))
    hq = MLA_HEADS * MLA_QW
    hk = MLA_HEADS * MLA_NOPE_DIM
    return pl.pallas_call(
        _mla_prep_kernel,
        grid=(N // tm,),
        in_specs=[row(A_PAD), full(1, MLA_Q_RANK), full(1, MLA_KV_RANK), full(MLA_Q_RANK, hq),
                  full(MLA_KV_RANK, 2 * hk), row(LANES), row(LANES)],
        out_specs=[row(hq), row(hk), row(LANES), row(hk)],
        out_shape=[jax.ShapeDtypeStruct((N, hq), BF16), jax.ShapeDtypeStruct((N, hk), BF16),
                   jax.ShapeDtypeStruct((N, LANES), BF16), jax.ShapeDtypeStruct((N, hk), BF16)],
        compiler_params=_cparams(("arbitrary",)),
        name="mla_prep",
    )(proj_a, gq.reshape(1, -1), gkv.reshape(1, -1), wq, wkv, c64, s64)


def _softmax_pv(q, k_ref, v_ref, kc):
    tq = q.shape[0]
    S = k_ref.shape[0]
    m = jnp.full((tq, 1), -jnp.inf, F32)
    l = jnp.zeros((tq, 1), F32)
    acc = jnp.zeros((tq, v_ref.shape[1]), F32)
    for c in range(S // kc):
        s = _nt_dot(q, k_ref[c * kc:(c + 1) * kc, :])
        m_new = jnp.maximum(m, jnp.max(s, axis=-1, keepdims=True))
        alpha = jnp.exp(m - m_new)
        p = jnp.exp(s - m_new)
        l = alpha * l + jnp.sum(p, axis=-1, keepdims=True)
        acc = alpha * acc + jnp.dot(p.astype(BF16), v_ref[c * kc:(c + 1) * kc, :],
                                    preferred_element_type=F32)
        m = m_new
    return acc / l


def _mla_attn_kernel(q_ref, kn_ref, kr_ref, v_ref, o_ref, kcat_ref, *, kc):
    @pl.when(pl.program_id(2) == 0)
    def _():
        kcat_ref[:, :LANES] = kn_ref[...]
        kcat_ref[:, LANES:] = kr_ref[...]

    o_ref[...] = _softmax_pv(q_ref[...], kcat_ref, v_ref, kc).astype(o_ref.dtype)


def _mla_attention(qm, kn, kr, vm, batch, seq, tq=1024, kc=1024):
    N = qm.shape[0]
    tq = min(tq, seq)
    kc = min(kc, seq)
    nq = seq // tq
    return pl.pallas_call(
        functools.partial(_mla_attn_kernel, kc=kc),
        grid=(batch, MLA_HEADS, nq),
        in_specs=[
            pl.BlockSpec((tq, MLA_QW), lambda b, h, i: (b * nq + i, h)),
            pl.BlockSpec((seq, LANES), lambda b, h, i: (b, h)),
            pl.BlockSpec((seq, LANES), lambda b, h, i: (b, 0)),
            pl.BlockSpec((seq, LANES), lambda b, h, i: (b, h)),
        ],
        out_specs=pl.BlockSpec((tq, LANES), lambda b, h, i: (b * nq + i, h)),
        out_shape=jax.ShapeDtypeStruct((N, MLA_HEADS * MLA_V_DIM), BF16),
        scratch_shapes=[pltpu.VMEM((seq, MLA_QW), BF16)],
        compiler_params=_cparams(("arbitrary", "arbitrary", "arbitrary")),
        name="mla_attention",
    )(qm, kn, kr, vm)


def _diff_attn_kernel(q_ref, k_ref, v_ref, lam_ref, g_ref, o_ref, *, kc, lam_init):
    q = q_ref[...]
    lane = lax.broadcasted_iota(jnp.int32, q.shape, 1)
    zero = jnp.zeros_like(q)
    o0 = _softmax_pv(jnp.where(lane < DIFF_QK_DIM, q, zero), k_ref, v_ref, kc)
    o1 = _softmax_pv(jnp.where(lane >= DIFF_QK_DIM, q, zero), k_ref, v_ref, kc)
    lp = lam_ref[...]
    lam = (jnp.exp(jnp.sum(lp[0:1] * lp[1:2], axis=-1, keepdims=True))
           - jnp.exp(jnp.sum(lp[2:3] * lp[3:4], axis=-1, keepdims=True)) + lam_init)
    o = o0 - lam * o1
    o = o * lax.rsqrt(jnp.mean(o * o, axis=-1, keepdims=True) + DIFF_NORM_EPS) * g_ref[...]
    o_ref[...] = (o * (1.0 - lam_init)).astype(o_ref.dtype)


def _diff_attention(rop, proj_v, lam_params, subln_g, lam_init, batch, seq, tq=1024, kc=1024):
    N = rop.shape[0]
    tq = min(tq, seq)
    kc = min(kc, seq)
    nq = seq // tq
    qb, kb, vb = R_DIFF_Q // LANES, R_DIFF_K // LANES, V_DIFF // LANES
    return pl.pallas_call(
        functools.partial(_diff_attn_kernel, kc=kc, lam_init=lam_init),
        grid=(batch, DIFF_HEADS, nq),
        in_specs=[
            pl.BlockSpec((tq, LANES), lambda b, h, i: (b * nq + i, qb + h)),
            pl.BlockSpec((seq, LANES), lambda b, h, i: (b, kb + h)),
            pl.BlockSpec((seq, LANES), lambda b, h, i: (b, vb + h)),
            pl.BlockSpec((4, DIFF_QK_DIM), lambda b, h, i: (0, 0)),
            pl.BlockSpec((1, DIFF_V_DIM), lambda b, h, i: (0, 0)),
        ],
        out_specs=pl.BlockSpec((tq, LANES), lambda b, h, i: (b * nq + i, h)),
        out_shape=jax.ShapeDtypeStruct((N, DIFF_HEADS * DIFF_V_DIM), BF16),
        compiler_params=_cparams(("arbitrary", "arbitrary", "arbitrary")),
        name="diff_attention",
    )(rop, rop, proj_v, lam_params, subln_g.reshape(1, -1))


def _banded_kernel(*refs, half_window, blk, blocks_per_seq, grp, use_sink, want_lse):
    q_ref, kp_ref, kc_ref, kn_ref, vp_ref, vc_ref, vn_ref = refs[:7]
    rest = list(refs[7:])
    sink_ref = rest.pop(0) if use_sink else None
    o_ref = rest.pop(0)
    lse_ref = rest.pop(0) if want_lse else None

    il = lax.rem(pl.program_id(0), blocks_per_seq)
    qpos = il * blk + lax.broadcasted_iota(jnp.int32, (blk, 3 * blk), 0)
    kpos = (il - 1) * blk + lax.broadcasted_iota(jnp.int32, (blk, 3 * blk), 1)
    valid = ((jnp.abs(qpos - kpos) <= half_window) & (kpos >= 0) & (kpos < blocks_per_seq * blk))

    heads = range(DIL_HEADS)
    cq = [slice(h * LANES, (h + 1) * LANES) for h in heads]
    ck = [slice((h // grp) * LANES, (h // grp + 1) * LANES) for h in heads]
    scores = []
    for h in heads:
        k = jnp.concatenate([kp_ref[:, ck[h]], kc_ref[:, ck[h]], kn_ref[:, ck[h]]], axis=0)
        scores.append(jnp.where(valid, _nt_dot(q_ref[:, cq[h]], k), -jnp.inf))
    probs, dens, maxes = [], [], []
    for h in heads:
        m = jnp.max(scores[h], axis=-1, keepdims=True)
        if use_sink:
            sk = jnp.max(jnp.broadcast_to(sink_ref[0:1, cq[h]], (blk, LANES)), axis=-1, keepdims=True)
            m = jnp.maximum(m, sk)
        p = jnp.exp(scores[h] - m)
        den = jnp.sum(p, axis=-1, keepdims=True)
        if use_sink:
            den = den + jnp.exp(sk - m)
        probs.append(p.astype(BF16))
        dens.append(den)
        maxes.append(m)
    for h in heads:
        v = jnp.concatenate([vp_ref[:, ck[h]], vc_ref[:, ck[h]], vn_ref[:, ck[h]]], axis=0)
        o = jnp.dot(probs[h], v, preferred_element_type=F32) / dens[h]
        o_ref[:, cq[h]] = o.astype(o_ref.dtype)
        if want_lse:
            lse_ref[:, cq[h]] = jnp.broadcast_to(maxes[h] + jnp.log(dens[h]), (blk, LANES))


def _banded_attention(q_arr, k_arr, v_arr, *, sub_len, half_window, q_col, k_col, v_col,
                      n_col_groups, kv_width, grp, sink=None, want_lse=False):
    rows = q_arr.shape[0]
    blk = min(128, sub_len)
    blocks_per_seq = sub_len // blk
    n_row_blocks = rows // blk
    last = n_row_blocks - 1

    def prev(i):
        return jnp.maximum(i - 1, 0)

    def nxt(i):
        return jnp.minimum(i + 1, last)

    qspec = pl.BlockSpec((blk, HW4), lambda i, c: (i, q_col + c))
    kspecs = [pl.BlockSpec((blk, kv_width), lambda i, c, f=f: (f(i), k_col + c))
              for f in (prev, lambda i: i, nxt)]
    vspecs = [pl.BlockSpec((blk, kv_width), lambda i, c, f=f: (f(i), v_col + c))
              for f in (prev, lambda i: i, nxt)]
    in_specs = [qspec] + kspecs + vspecs
    args = [q_arr, k_arr, k_arr, k_arr, v_arr, v_arr, v_arr]
    if sink is not None:
        in_specs.append(pl.BlockSpec((1, HW4), lambda i, c: (0, c)))
        args.append(sink)
    out_cols = n_col_groups * HW4
    ospec = pl.BlockSpec((blk, HW4), lambda i, c: (i, c))
    out_specs = [ospec]
    out_shape = [jax.ShapeDtypeStruct((rows, out_cols), BF16)]
    if want_lse:
        out_specs.append(ospec)
        out_shape.append(jax.ShapeDtypeStruct((rows, out_cols), F32))
    return pl.pallas_call(
        functools.partial(_banded_kernel, half_window=half_window, blk=blk,
                          blocks_per_seq=blocks_per_seq, grp=grp, use_sink=sink is not None,
                          want_lse=want_lse),
        grid=(n_row_blocks, n_col_groups),
        in_specs=in_specs,
        out_specs=out_specs,
        out_shape=out_shape,
        compiler_params=_cparams(("arbitrary", "arbitrary")),
        name="banded_attention",
    )(*args)


def _dil_combine_kernel(o0, l0, o1, l1, o2, l2, out_ref, scr, *, dils, tm):
    def natural(o_ref, l_ref, d, base):
        for r in range(d):
            for c in range(DIL_HEADS):
                cols = slice(c * LANES, (c + 1) * LANES)
                scr[base + c, pl.ds(r, tm // d, stride=d), :] = o_ref[r, :, cols].astype(F32)
                scr[base + DIL_HEADS + c, pl.ds(r, tm // d, stride=d), :] = l_ref[r, :, cols]

    natural(o1, l1, dils[0], 0)
    natural(o2, l2, dils[1], 2 * DIL_HEADS)
    for c in range(DIL_HEADS):
        cols = slice(c * LANES, (c + 1) * LANES)
        oa, la = o0[:, cols].astype(F32), l0[:, cols]
        ob, lb = scr[c], scr[DIL_HEADS + c]
        oc, lc = scr[2 * DIL_HEADS + c], scr[3 * DIL_HEADS + c]
        m = jnp.maximum(jnp.maximum(la, lb), lc)
        ea, eb, ec = jnp.exp(la - m), jnp.exp(lb - m), jnp.exp(lc - m)
        inv = 1.0 / (ea + eb + ec)
        out_ref[:, cols] = ((ea * inv) * oa + (eb * inv) * ob + (ec * inv) * oc).astype(out_ref.dtype)


def _dil_combine(o0, l0, o1, l1, o2, l2, batch, seq, dils, tm=256):
    N = o0.shape[0]
    tm = min(tm, seq)
    per_b = seq // tm
    row = pl.BlockSpec((tm, HW4), lambda i: (i, 0))
    grouped = [pl.BlockSpec((d, tm // d, HW4), lambda i: (i // per_b, i % per_b, 0)) for d in dils]
    shaped = lambda a, d: a.reshape(batch * d, seq // d, HW4)
    return pl.pallas_call(
        functools.partial(_dil_combine_kernel, dils=dils, tm=tm),
        grid=(N // tm,),
        in_specs=[row, row, grouped[0], grouped[0], grouped[1], grouped[1]],
        out_specs=row,
        out_shape=jax.ShapeDtypeStruct((N, HW4), BF16),
        scratch_shapes=[pltpu.VMEM((4 * DIL_HEADS, tm, LANES), F32)],
        compiler_params=_cparams(("arbitrary",)),
        name="dil_combine",
    )(o0, l0, shaped(o1, dils[0]), shaped(l1, dils[0]), shaped(o2, dils[1]), shaped(l2, dils[1]))


def _merge_kernel(oa, ob, oc, od, wa, wb, wc, wd, ga, gb, gc, gd, o_ref, *scratch):
    scratch = list(scratch)
    acc = None
    for o, w, g in ((oa, wa, ga), (ob, wb, gb), (oc, wc, gc), (od, wd, gd)):
        wt = _resident_bf16(w, scratch)
        gate = 0.5 * jnp.tanh(0.5 * g[...].astype(F32)) + 0.5
        t = gate * jnp.dot(o[...], wt, preferred_element_type=F32)
        acc = t if acc is None else acc + t
    o_ref[...] = acc.astype(o_ref.dtype)


def _merge(branch_outs, w_branch, gates, layer, tn=512, tm=1024):
    N = gates.shape[0]
    D = D_MODEL
    tm = min(tm, N)
    oa, ob, oc, od = branch_outs
    in_specs = [pl.BlockSpec((tm, o.shape[1]), lambda j, i: (i, 0)) for o in branch_outs]
    rows = [o.shape[1] for o in branch_outs]
    row_blocks = [0, 1, 2, 3072 // rows[3]]
    in_specs += [_weight_spec(w_branch, layer, rows[r], tn, row_blocks[r]) for r in range(4)]
    nj = D // tn
    in_specs += [pl.BlockSpec((tm, tn), lambda j, i, r=r: (i, r * nj + j)) for r in range(4)]
    return pl.pallas_call(
        _merge_kernel,
        grid=(nj, N // tm),
        in_specs=in_specs,
        out_specs=pl.BlockSpec((tm, tn), lambda j, i: (i, j)),
        out_shape=jax.ShapeDtypeStruct((N, D), BF16),
        scratch_shapes=_weight_scratch(*[(w_branch, (rows[r], tn)) for r in range(4)]),
        compiler_params=_cparams(("arbitrary", "arbitrary")),
        name="branch_merge",
    )(oa, ob, oc, od, w_branch, w_branch, w_branch, w_branch, gates, gates, gates, gates)


VMEM_PITCH = 20


def _pack_pairs(lo, hi):
    lo_b = lax.bitcast_convert_type(lo.astype(BF16).astype(F32), jnp.uint32)
    hi_b = lax.bitcast_convert_type(hi.astype(BF16).astype(F32), jnp.uint32)
    return (lo_b >> 16) | (hi_b & jnp.uint32(0xFFFF0000))


def _unpack_pairs(w):
    lo = lax.bitcast_convert_type(w << 16, F32)
    hi = lax.bitcast_convert_type(w & jnp.uint32(0xFFFF0000), F32)
    return lo, hi


def _store_packed(o_ref, x):
    rows = x.shape[0]
    for j in range(PACK_SLABS):
        o_ref[pl.ds(j, rows, stride=PACK_SLABS), :] = _pack_pairs(
            x[:, j * LANES:(j + 1) * LANES], x[:, PACK_HALF + j * LANES:PACK_HALF + (j + 1) * LANES])


def _load_slab(buf_ref, first_row, rows, j):
    return buf_ref[pl.ds(first_row + j, rows, stride=VMEM_PITCH), :]


def _router_kernel(x_ref, g_ref, mod_ref, rw_ref, rb_ref, hb_ref, hp_ref, idx_ref, wt_ref, *,
                   shift_row, scale_row):
    x = x_ref[...]
    ms = jnp.mean(x * x, axis=-1, keepdims=True)
    y = x * lax.rsqrt(ms + NORM_EPS) * g_ref[...]
    h = y * (1.0 + mod_ref[0, scale_row:scale_row + 1, :]) + mod_ref[0, shift_row:shift_row + 1, :]
    hb_ref[...] = h.astype(hb_ref.dtype)
    _store_packed(hp_ref, h)

    tm = x.shape[0]
    logits = jnp.dot(h, rw_ref[...], preferred_element_type=F32, precision=lax.Precision.HIGHEST)
    lt = logits.T[:N_EXPERTS, :]
    scores = jax.nn.sigmoid(lt)
    choice = scores + rb_ref[...]

    per_group = N_EXPERTS // N_EXPERT_GROUPS
    sub = lax.broadcasted_iota(jnp.int32, (per_group, tm), 0)
    group_score = []
    for g in range(N_EXPERT_GROUPS):
        cg = choice[g * per_group:(g + 1) * per_group, :]
        m1 = jnp.max(cg, axis=0, keepdims=True)
        first = jnp.min(jnp.where(cg == m1, sub, per_group), axis=0, keepdims=True)
        m2 = jnp.max(jnp.where(sub == first, -jnp.inf, cg), axis=0, keepdims=True)
        group_score.append(m1 + m2)
    masked = []
    for g in range(N_EXPERT_GROUPS):
        ahead = jnp.zeros((1, tm), jnp.int32)
        for o in range(N_EXPERT_GROUPS):
            if o == g:
                continue
            better = (group_score[o] >= group_score[g]) if o < g else (group_score[o] > group_score[g])
            ahead = ahead + better.astype(jnp.int32)
        cg = choice[g * per_group:(g + 1) * per_group, :]
        masked.append(jnp.where(ahead < TOP_GROUPS, cg, -jnp.inf))
    cm = jnp.concatenate(masked, axis=0)

    eidx = lax.broadcasted_iota(jnp.int32, (N_EXPERTS, tm), 0)
    rank = jnp.zeros((N_EXPERTS, tm), jnp.int32)
    for e in range(N_EXPERTS):
        row = cm[e:e + 1, :]
        tie = jnp.where(eidx > e, 1, 0)
        rank = rank + jnp.where(row > cm, 1, jnp.where(row == cm, tie, 0))
    sel = rank < TOP_K
    wsel = jnp.where(sel, scores, 0.0)
    wsel = wsel / jnp.sum(wsel, axis=0, keepdims=True) * ROUTED_SCALE
    idx_rows, wt_rows = [], []
    for r in range(TOP_K):
        hit = rank == r
        idx_rows.append(jnp.sum(jnp.where(hit, eidx, 0), axis=0, keepdims=True))
        wt_rows.append(jnp.sum(jnp.where(hit, wsel, 0.0), axis=0, keepdims=True))
    idx_ref[...] = jnp.concatenate(idx_rows, axis=0)
    wt_ref[...] = jnp.concatenate(wt_rows, axis=0)


def _router(x, g, mod, shift_row, scale_row, router_w, router_bias, seq, tm=256):
    N, D = x.shape
    tm = min(tm, seq)
    per_b = seq // tm
    rw = jnp.zeros((D, LANES), F32).at[:, :N_EXPERTS].set(router_w)
    row = pl.BlockSpec((tm, D), lambda i: (i, 0))
    col = lambda r: pl.BlockSpec((r, tm), lambda i: (0, i))
    return pl.pallas_call(
        functools.partial(_router_kernel, shift_row=shift_row, scale_row=scale_row),
        grid=(N // tm,),
        in_specs=[
            row,
            pl.BlockSpec((1, D), lambda i: (0, 0)),
            pl.BlockSpec((1, 6, D), lambda i: (i // per_b, 0, 0)),
            pl.BlockSpec((D, LANES), lambda i: (0, 0)),
            pl.BlockSpec((N_EXPERTS, 1), lambda i: (0, 0)),
        ],
        out_specs=[row, pl.BlockSpec((tm * PACK_SLABS, LANES), lambda i: (i, 0)), col(TOP_K), col(TOP_K)],
        out_shape=[jax.ShapeDtypeStruct((N, D), BF16), jax.ShapeDtypeStruct((N * PACK_SLABS, LANES), jnp.uint32),
                   jax.ShapeDtypeStruct((TOP_K, N), jnp.int32), jax.ShapeDtypeStruct((TOP_K, N), F32)],
        compiler_params=_cparams(("arbitrary",)),
        name="ffn_norm_router",
    )(x, g.reshape(1, D), mod, rw, router_bias.reshape(N_EXPERTS, 1))


def _num_expert_blocks(n_tokens):
    bm = EXPERT_BLOCK_ROWS
    return -(-(n_tokens * TOP_K + N_EXPERTS * (bm - 1)) // bm)


def _dispatch_plan(top_idx):
    K, N = top_idx.shape
    bm = EXPERT_BLOCK_ROWS
    n_blocks = _num_expert_blocks(N)
    experts = jnp.arange(N_EXPERTS, dtype=jnp.int32)
    onehot = top_idx[:, None, :] == experts[None, :, None]
    mask = jnp.any(onehot, axis=0).astype(jnp.int32)
    counts = jnp.sum(mask, axis=1)
    pos = jnp.cumsum(mask, axis=1) - mask
    nblk = (counts + bm - 1) // bm
    blk_end = jnp.cumsum(nblk)
    blk_start = blk_end - nblk
    n_used = blk_end[-1]
    start = jnp.cumsum(counts) - counts
    slot_en = (blk_start * bm)[:, None] + pos
    slot_of = jnp.sum(jnp.where(onehot, slot_en[None], 0), axis=1)

    keys = top_idx * N + jnp.arange(N, dtype=jnp.int32)[None, :]
    tok_sorted = jnp.sort(keys.reshape(-1)) % N

    bidx = jnp.arange(n_blocks + GATHER_AHEAD, dtype=jnp.int32)
    be = jnp.minimum(jnp.sum((bidx[:, None] >= blk_end[None, :]).astype(jnp.int32), axis=1),
                     N_EXPERTS - 1)
    be = jnp.where(bidx < n_used, be, be[jnp.maximum(n_used - 1, 0)])
    off = jnp.clip(start[be] + (bidx - blk_start[be]) * bm, 0, K * N)
    tok_pad = jnp.concatenate([tok_sorted, jnp.zeros((bm,), jnp.int32)])
    run_end = blk_end[be]
    nxt = jnp.where(run_end < n_used, be[jnp.minimum(run_end, n_blocks)], -1)
    return (be.astype(jnp.int32), nxt.astype(jnp.int32), off.astype(jnp.int32), tok_pad.astype(jnp.int32),
            slot_of.astype(jnp.int32), n_used.astype(jnp.int32))


WEIGHT_CHUNKS = 4
GATHER_AHEAD = 2
GATHER_BUFS = GATHER_AHEAD + 1


def _expert_kernel(be_ref, nxt_ref, off_ref, tok_ref, nu_ref, h_hbm, wg_hbm, wu_hbm, wd_hbm, y_ref,
                   xbuf0, xbuf1, xbuf2, xs, stg_g, stg_u, stg_d, wg_s, wu_s, wd_s, sem, wsem, *, layer):
    bm = EXPERT_BLOCK_ROWS
    b = pl.program_id(0)
    n_used = nu_ref[0]
    bufs = (xbuf0, xbuf1, xbuf2)

    def weight_copies(e):
        copies = []
        for src, dst in ((wg_hbm, stg_g), (wu_hbm, stg_u), (wd_hbm, stg_d)):
            rows = dst.shape[0] // WEIGHT_CHUNKS
            for c in range(WEIGHT_CHUNKS):
                copies.append(pltpu.make_async_copy(src.at[layer, e, pl.ds(c * rows, rows), :],
                                                    dst.at[pl.ds(c * rows, rows), :], wsem.at[0]))
        return copies

    def start_gather(blk, slot, unrolled):
        base = off_ref[blk]

        def one(r, priority):
            src = pl.multiple_of(tok_ref[base + r] * PACK_SLABS, PACK_SLABS)
            pltpu.make_async_copy(h_hbm.at[pl.ds(src, PACK_SLABS), :],
                                  bufs[slot].at[pl.ds(r * VMEM_PITCH, PACK_SLABS), :],
                                  sem.at[slot]).start(priority=priority)

        if unrolled:
            for r in range(bm):
                one(r, r % 2)
        else:
            def body(i, carry):
                one(2 * i, 0)
                one(2 * i + 1, 1)
                return carry
            lax.fori_loop(0, bm // 2, body, 0)

    def wait_gather(slot):
        pltpu.make_async_copy(h_hbm.at[pl.ds(0, bm * PACK_SLABS), :],
                              bufs[slot].at[pl.ds(0, bm * PACK_SLABS), :], sem.at[slot]).wait()

    @pl.when(b == 0)
    def _():
        for ahead in range(GATHER_AHEAD):
            start_gather(ahead, ahead, False)
        for cp in weight_copies(be_ref[0]):
            cp.start()

    new_expert = (b == 0) | (be_ref[b] != be_ref[jnp.maximum(b - 1, 0)])

    @pl.when(new_expert & (b < n_used))
    def _():
        for cp in weight_copies(be_ref[b]):
            cp.wait()
        wg_s[...] = stg_g[...].astype(BF16)
        wu_s[...] = stg_u[...].astype(BF16)
        wd_s[...] = stg_d[...].astype(BF16)

        @pl.when(nxt_ref[b] >= 0)
        def _():
            for cp in weight_copies(nxt_ref[b]):
                cp.start()

    def work(slot):
        wait_gather(slot)
        for j in range(PACK_SLABS):
            lo, hi = _unpack_pairs(_load_slab(bufs[slot], 0, bm, j))
            xs[:, j * LANES:(j + 1) * LANES] = lo.astype(BF16)
            xs[:, PACK_HALF + j * LANES:PACK_HALF + (j + 1) * LANES] = hi.astype(BF16)
        start_gather(b + GATHER_AHEAD, (slot + GATHER_AHEAD) % GATHER_BUFS, True)
        x = xs[...]
        g = jnp.dot(x, wg_s[...], preferred_element_type=F32)
        u = jnp.dot(x, wu_s[...], preferred_element_type=F32)
        a = (g * jax.nn.sigmoid(g) * u).astype(BF16)
        _store_packed(y_ref, jnp.dot(a, wd_s[...], preferred_element_type=F32))

    for slot in range(GATHER_BUFS):
        @pl.when((b < n_used) & (b % GATHER_BUFS == slot))
        def _(slot=slot):
            work(slot)

        @pl.when((b >= n_used) & (b < n_used + GATHER_AHEAD) & (b % GATHER_BUFS == slot))
        def _(slot=slot):
            wait_gather(slot)

    @pl.when(b >= n_used)
    def _():
        y_ref[...] = jnp.zeros(y_ref.shape, y_ref.dtype)


def _expert_ffn(h_packed, w_gate, w_up, w_down, layer, block_expert, next_expert, block_off, tok_pad,
                n_used):
    bm = EXPERT_BLOCK_ROWS
    D = D_MODEL
    n_steps = block_expert.shape[0]
    any_spec = pl.BlockSpec(memory_space=pl.ANY)
    grid_spec = pltpu.PrefetchScalarGridSpec(
        num_scalar_prefetch=5,
        grid=(n_steps,),
        in_specs=[any_spec, any_spec, any_spec, any_spec],
        out_specs=pl.BlockSpec((bm * PACK_SLABS, LANES), lambda b, *_: (b, 0)),
        scratch_shapes=[pltpu.VMEM((bm * VMEM_PITCH, LANES), jnp.uint32)] * GATHER_BUFS + [
                        pltpu.VMEM((bm, D), BF16),
                        pltpu.VMEM((D, EXPERT_FF), F32), pltpu.VMEM((D, EXPERT_FF), F32),
                        pltpu.VMEM((EXPERT_FF, D), F32),
                        pltpu.VMEM((D, EXPERT_FF), BF16), pltpu.VMEM((D, EXPERT_FF), BF16),
                        pltpu.VMEM((EXPERT_FF, D), BF16),
                        pltpu.SemaphoreType.DMA((GATHER_BUFS,)), pltpu.SemaphoreType.DMA((1,))],
    )
    return pl.pallas_call(
        functools.partial(_expert_kernel, layer=layer),
        grid_spec=grid_spec,
        out_shape=jax.ShapeDtypeStruct((n_steps * bm * PACK_SLABS, LANES), jnp.uint32),
        compiler_params=_cparams(("arbitrary",)),
        name="expert_ffn",
    )(block_expert, next_expert, block_off, tok_pad, n_used.reshape(1), h_packed, w_gate, w_up, w_down)


def _combine_kernel(slot_ref, y_hbm, x_ref, sh_ref, w_ref, mod_ref, *refs, gate_row, n_tokens, tail):
    refs = list(refs)
    ng_ref = refs.pop(0) if tail else None
    nmod_ref = refs.pop(0) if tail == "next" else None
    o_ref = refs.pop(0)
    h_ref = refs.pop(0) if tail == "next" else None
    ybuf0, ybuf1, ybuf2, sem = refs
    tc = COMBINE_TOKENS
    i = pl.program_id(0)
    n_steps = pl.num_programs(0)
    bufs = (ybuf0, ybuf1, ybuf2)

    def start_gather(step, slot, unrolled):
        base = step * tc

        def one(t, k):
            src = pl.multiple_of(slot_ref[k * n_tokens + base + t] * PACK_SLABS, PACK_SLABS)
            pltpu.make_async_copy(y_hbm.at[pl.ds(src, PACK_SLABS), :],
                                  bufs[slot].at[pl.ds((k * tc + t) * VMEM_PITCH, PACK_SLABS), :],
                                  sem.at[slot]).start(priority=k % 2)

        if unrolled:
            for t in range(tc):
                for k in range(TOP_K):
                    one(t, k)
        else:
            def body(t, carry):
                for k in range(TOP_K):
                    one(t, k)
                return carry
            lax.fori_loop(0, tc, body, 0)

    @pl.when(i == 0)
    def _():
        for ahead in range(GATHER_AHEAD):
            start_gather(ahead, ahead, False)

    def work(slot, prefetch):
        n_rows = TOP_K * tc * PACK_SLABS
        pltpu.make_async_copy(y_hbm.at[pl.ds(0, n_rows), :], bufs[slot].at[pl.ds(0, n_rows), :],
                              sem.at[slot]).wait()
        if prefetch:
            start_gather(i + GATHER_AHEAD, (slot + GATHER_AHEAD) % GATHER_BUFS, True)
        w = w_ref[...]
        wk = [w[:, k:k + 1] for k in range(TOP_K)]
        for j in range(PACK_SLABS):
            c_lo = slice(j * LANES, (j + 1) * LANES)
            c_hi = slice(PACK_HALF + j * LANES, PACK_HALF + (j + 1) * LANES)
            acc_lo = sh_ref[:, c_lo].astype(F32)
            acc_hi = sh_ref[:, c_hi].astype(F32)
            for k in range(TOP_K):
                lo, hi = _unpack_pairs(_load_slab(bufs[slot], k * tc * VMEM_PITCH, tc, j))
                acc_lo = acc_lo + wk[k] * lo
                acc_hi = acc_hi + wk[k] * hi
            o_ref[:, c_lo] = x_ref[:, c_lo] + mod_ref[0, gate_row:gate_row + 1, c_lo] * acc_lo
            o_ref[:, c_hi] = x_ref[:, c_hi] + mod_ref[0, gate_row:gate_row + 1, c_hi] * acc_hi
        if tail:
            xo = o_ref[...]
            y = xo * lax.rsqrt(jnp.mean(xo * xo, axis=-1, keepdims=True) + NORM_EPS) * ng_ref[...]
            if tail == "final":
                o_ref[...] = y
            else:
                h_ref[...] = (y * (1.0 + nmod_ref[0, 1:2, :]) + nmod_ref[0, 0:1, :]).astype(h_ref.dtype)

    for slot in range(GATHER_BUFS):
        for prefetch in (True, False):
            @pl.when((i % GATHER_BUFS == slot) & ((i + GATHER_AHEAD < n_steps) == prefetch))
            def _(slot=slot, prefetch=prefetch):
                work(slot, prefetch)


def _combine(y_packed, slot_of, x, shared, w_tok, mod, gate_row, seq, next_norm=None, final_norm_g=None):
    N, D = x.shape
    tc = COMBINE_TOKENS
    per_b = seq // tc
    row = pl.BlockSpec((tc, D), lambda i, s: (i, 0))
    mod_spec = pl.BlockSpec((1, 6, D), lambda i, s: (i // per_b, 0, 0))
    gain_spec = pl.BlockSpec((1, D), lambda i, s: (0, 0))
    in_specs = [pl.BlockSpec(memory_space=pl.ANY), row, row,
                pl.BlockSpec((tc, TOP_K), lambda i, s: (i, 0)), mod_spec]
    args = [slot_of.reshape(-1), y_packed, x, shared, w_tok, mod]
    out_specs, out_shape, tail = [row], [jax.ShapeDtypeStruct((N, D), F32)], None
    if next_norm is not None:
        tail = "next"
        in_specs += [gain_spec, mod_spec]
        args += [next_norm[0].reshape(1, D), next_norm[1]]
        out_specs.append(row)
        out_shape.append(jax.ShapeDtypeStruct((N, D), BF16))
    elif final_norm_g is not None:
        tail = "final"
        in_specs.append(gain_spec)
        args.append(final_norm_g.reshape(1, D))
    grid_spec = pltpu.PrefetchScalarGridSpec(
        num_scalar_prefetch=1,
        grid=(N // tc,),
        in_specs=in_specs,
        out_specs=out_specs,
        scratch_shapes=[pltpu.VMEM((TOP_K * tc * VMEM_PITCH, LANES), jnp.uint32)] * GATHER_BUFS
        + [pltpu.SemaphoreType.DMA((GATHER_BUFS,))],
    )
    return pl.pallas_call(
        functools.partial(_combine_kernel, gate_row=gate_row, n_tokens=N, tail=tail),
        grid_spec=grid_spec,
        out_shape=out_shape,
        compiler_params=_cparams(("arbitrary",)),
        name="expert_combine",
    )(*args)


def _mla_weights(w_uq, w_ukv):
    qd = MLA_NOPE_DIM + MLA_ROPE_DIM
    wq = w_uq.reshape(MLA_Q_RANK, MLA_HEADS, qd)
    wq = jnp.concatenate([wq, jnp.zeros((MLA_Q_RANK, MLA_HEADS, MLA_QW - qd), wq.dtype)], axis=-1)
    wkv = w_ukv.reshape(MLA_KV_RANK, MLA_HEADS, MLA_NOPE_DIM + MLA_V_DIM)
    wkv = jnp.concatenate([wkv[:, :, :MLA_NOPE_DIM].reshape(MLA_KV_RANK, -1),
                           wkv[:, :, MLA_NOPE_DIM:].reshape(MLA_KV_RANK, -1)], axis=-1)
    return wq.reshape(MLA_Q_RANK, MLA_HEADS * MLA_QW).astype(BF16), wkv.astype(BF16)


def kernel(x, c, positions, w_ada, b_ada, mix_norm_g, ffn_norm_g, w_in, mla_q_norm_g, mla_w_uq,
           mla_kv_norm_g, mla_w_ukv, diff_lambda, diff_subln_g, swa_sink, w_branch, w_out,
           router_w, router_bias, expert_w_gate, expert_w_up, expert_w_down,
           shared_w_gate, shared_w_up, shared_w_down, final_norm_g):
    B, S, D = x.shape
    N = B * S
    xf = x.reshape(N, D)
    mod_all = _ada_mod(c, w_ada, b_ada)
    w_in_t = jnp.swapaxes(w_in, 1, 2)
    tabs = _rope_tables(positions)
    c64, s64 = tabs[2], tabs[3]

    mods = [mod_all[l, :B].reshape(B, 6, D) for l in range(DEPTH)]
    h = _norm_mod(xf, mix_norm_g[0], mods[0], 0, 1, S)
    for l in range(DEPTH):
        mod = mods[l]

        proj_r = _matmul_w_in(h, w_in_t, l, R_TILE_COLS, "proj_rotary")
        proj_x = _matmul_w_in(h, w_in_t, l, X_TILE_COLS, "proj_dilated")
        proj_v = _matmul_w_in(h, w_in_t, l, V_TILE_COLS, "proj_value")
        proj_a = _matmul_w_in(h, w_in_t, l, A_TILE_COLS, "proj_latent", tn=A_PAD // 3)
        gates = _matmul_w_in(h, w_in_t, l, G_TILE_COLS, "proj_gates")
        rop = _rope_section(proj_r, proj_v, tabs)
        dil_groups = _dil_rope_section(proj_x, tabs[0], tabs[1], B, S, tuple(d for _, d in DIL_PAIRS[1:]))

        wq, wkv = _mla_weights(mla_w_uq[l], mla_w_ukv[l])
        qm, kn, kr, vm = _mla_prep(proj_a, mla_q_norm_g[l], mla_kv_norm_g[l], wq, wkv, c64, s64)
        out_a = _mla_attention(qm, kn, kr, vm, B, S)

        lam_init = 0.8 - 0.6 * math.exp(-0.3 * l)
        out_b = _diff_attention(rop, proj_v, diff_lambda[l], diff_subln_g[l], lam_init, B, S)

        sink = jnp.repeat(swa_sink[l].astype(F32), LANES).reshape(1, SWA_HEADS * LANES)
        (out_c,) = _banded_attention(
            rop, rop, proj_v, sub_len=S, half_window=SWA_HALF_WINDOW,
            q_col=R_SWA_Q // HW4, k_col=R_SWA_K // LANES, v_col=V_SWA // LANES,
            n_col_groups=SWA_KV_HEADS, kv_width=LANES, grp=SWA_HEADS // SWA_KV_HEADS, sink=sink)

        window0, _ = DIL_PAIRS[0]
        d_res = list(_banded_attention(
            rop, rop, proj_v, sub_len=S, half_window=window0 // 2, q_col=R_DIL0_Q // HW4,
            k_col=R_DIL0_K // HW4, v_col=V_DIL0 // HW4, n_col_groups=1, kv_width=HW4, grp=1, want_lse=True))
        for (window, dil), (qg, kg, vg) in zip(DIL_PAIRS[1:], dil_groups):
            flat = lambda a: a.reshape(N, HW4)
            d_res += _banded_attention(
                flat(qg), flat(kg), flat(vg), sub_len=S // dil, half_window=window // (2 * dil),
                q_col=0, k_col=0, v_col=0, n_col_groups=1, kv_width=HW4, grp=1, want_lse=True)
        out_d = _dil_combine(*d_res, B, S, tuple(d for _, d in DIL_PAIRS[1:]))

        merged = _merge((out_a, out_b, out_c, out_d), w_branch, gates, l)
        xf = _matmul_residual(merged, w_out, xf, mod, 2, S, layer=l)

        hb, hp, top_idx, top_w = _router(xf, ffn_norm_g[l], mod, 3, 4, router_w[l], router_bias[l], S)
        block_expert, next_expert, block_off, tok_pad, slot_of, n_used = _dispatch_plan(top_idx)
        y = _expert_ffn(hp, expert_w_gate, expert_w_up, expert_w_down, l, block_expert, next_expert,
                        block_off, tok_pad, n_used)
        act = _matmul_glu(hb, shared_w_gate, shared_w_up, layer=l)
        shared = _matmul(act, shared_w_down, tn=1024, name="shared_down", layer=l)
        if l + 1 < DEPTH:
            xf, h = _combine(y, slot_of, xf, shared, top_w.T, mod, 5, S,
                             next_norm=(mix_norm_g[l + 1], mods[l + 1]))
        else:
            (out,) = _combine(y, slot_of, xf, shared, top_w.T, mod, 5, S, final_norm_g=final_norm_g)

    return out.reshape(B, S, D)
```
